```python
import math
import jax
import jax.numpy as jnp
from jax import lax
import numpy as np

D_MODEL = 2048
BATCH = 8
SEQ = 2048
DEPTH = 2
DEC_BATCH = 128
DEC_SEQ = 8
PAST_LEN = 2048
PAGE_SIZE = 128

SB_HEAD_DIM = 64
SB_WIDTH = D_MODEL // 2
SB_HEADS = SB_WIDTH // SB_HEAD_DIM
SB_Q_BLOCK = 128
SB_LOGIT_BIAS_INIT = -6.0
SSM_GROUP_CH = 16
SSM_WIDTH = D_MODEL // 4
SSM_GROUPS = SSM_WIDTH // SSM_GROUP_CH
SSM_STATE = 64
SSM_DT_MIN = 0.001
SSM_DT_MAX = 0.1
DN_HEAD_DIM = 128
DN_WIDTH = D_MODEL // 4
DN_HEADS = DN_WIDTH // DN_HEAD_DIM
DN_CONV = 4
DN_CHUNK = 64
D_FF = 11 * D_MODEL // 4
FFN_CONV = 3
NORM_EPS = 1e-6
SPLIT_SIZES = (SB_WIDTH, SB_WIDTH, SB_WIDTH, SSM_WIDTH, 3 * DN_WIDTH, DN_HEADS, DN_HEADS, DN_WIDTH)
IN_COLS = 3 * SB_WIDTH + SSM_WIDTH + 3 * DN_WIDTH + 2 * DN_HEADS + DN_WIDTH

kernel_name = "hymba_sb_s5_gdn_convffn_step"


def rms_norm(x, g):
    xf = x.astype(jnp.float32)
    y = xf * lax.rsqrt(jnp.mean(xf * xf, axis=-1, keepdims=True) + NORM_EPS) * g.astype(jnp.float32)
    return y.astype(x.dtype)


def l2_norm(x):
    return x * lax.rsqrt(jnp.sum(x * x, axis=-1, keepdims=True) + NORM_EPS)


def causal_dwconv(x, buf, w):
    t = x.shape[1]
    xcat = jnp.concatenate([buf.astype(x.dtype), x], axis=1)
    y = xcat[:, 0:t] * w[0]
    for i in range(1, w.shape[0]):
        y = y + xcat[:, i:i + t] * w[i]
    return y, xcat[:, t:]


def stick_breaking_attention(q, k, v, bias):
    bsz, tq, nh, dh = q.shape
    tk = k.shape[1]
    blk = min(SB_Q_BLOCK, tq)
    nb = -(-tq // blk)
    pad = nb * blk - tq
    q_pos = (tk - tq) + jnp.arange(nb * blk)
    qb = jnp.pad(q, ((0, 0), (0, pad), (0, 0), (0, 0))).reshape(bsz, nb, blk, nh, dh).transpose(1, 0, 2, 3, 4)
    pb = q_pos.reshape(nb, blk)
    k_pos = jnp.arange(tk)
    scale = dh ** -0.5
    bias_f = bias.astype(jnp.float32)[None, :, None, None]

    def one_block(args):
        q_blk, pos = args
        z = jnp.einsum("bqhd,bkhd->bhqk", q_blk, k, preferred_element_type=jnp.float32) * scale + bias_f
        mask = k_pos[None, :] < pos[:, None]
        log_1m = jnp.where(mask, jax.nn.log_sigmoid(-z), 0.0)
        log_w = jax.nn.log_sigmoid(z) + lax.cumsum(log_1m, axis=3, reverse=True) - log_1m
        w = jnp.where(mask, jnp.exp(log_w), 0.0)
        return jnp.einsum("bhqk,bkhd->bqhd", w.astype(v.dtype), v)

    o = lax.map(one_block, (qb, pb))
    return o.transpose(1, 0, 2, 3, 4).reshape(bsz, nb * blk, nh, dh)[:, :tq]


def s5_mixer(u, h0_re, h0_im, p):
    bsz, t, _ = u.shape
    f32 = jnp.float32
    uf = u.astype(f32)
    ug = uf.reshape(bsz, t, SSM_GROUPS, SSM_GROUP_CH)
    lam_re = p["ssm_lambda_re"].astype(f32)
    lam_im = p["ssm_lambda_im"].astype(f32)
    dt = jnp.exp(p["ssm_log_step"].astype(f32))[:, None]
    mag = jnp.exp(lam_re * dt)
    ab_re = mag * jnp.cos(lam_im * dt)
    ab_im = mag * jnp.sin(lam_im * dt)
    den = lam_re * lam_re + lam_im * lam_im
    xr = ab_re - 1.0
    f_re = (xr * lam_re + ab_im * lam_im) / den
    f_im = (ab_im * lam_re - xr * lam_im) / den
    b_re = p["ssm_b_re"].astype(f32)
    b_im = p["ssm_b_im"].astype(f32)
    bb_re = f_re[..., None] * b_re - f_im[..., None] * b_im
    bb_im = f_re[..., None] * b_im + f_im[..., None] * b_re
    bu_re = jnp.einsum("btgc,gnc->btgn", ug, bb_re)
    bu_im = jnp.einsum("btgc,gnc->btgn", ug, bb_im)
    h0r = h0_re.astype(f32)
    h0i = h0_im.astype(f32)
    bu_re = bu_re.at[:, 0].add(ab_re * h0r - ab_im * h0i)
    bu_im = bu_im.at[:, 0].add(ab_re * h0i + ab_im * h0r)
    a_re = jnp.broadcast_to(ab_re, bu_re.shape)
    a_im = jnp.broadcast_to(ab_im, bu_im.shape)

    def combine(e1, e2):
        a1r, a1i, b1r, b1i = e1
        a2r, a2i, b2r, b2i = e2
        return (a1r * a2r - a1i * a2i, a1r * a2i + a1i * a2r,
                a2r * b1r - a2i * b1i + b2r, a2r * b1i + a2i * b1r + b2i)

    _, _, h_re, h_im = lax.associative_scan(combine, (a_re, a_im, bu_re, bu_im), axis=1)
    y = (jnp.einsum("btgn,gcn->btgc", h_re, p["ssm_c_re"].astype(f32))
         - jnp.einsum("btgn,gcn->btgc", h_im, p["ssm_c_im"].astype(f32)))
    y = y.reshape(bsz, t, SSM_WIDTH) + p["ssm_d"].astype(f32) * uf
    y = jax.nn.gelu(y)
    y = y * jax.nn.sigmoid(y @ p["ssm_w_glu"].astype(f32) + p["ssm_b_glu"].astype(f32))
    return y.astype(u.dtype), h_re[:, -1].astype(u.dtype), h_im[:, -1].astype(u.dtype)


def _to_chunks(x, nc, c):
    shp = x.shape
    x = x.reshape((shp[0], nc, c, shp[2]) + shp[3:])
    return jnp.moveaxis(x, 3, 1)


def gated_delta_chunked(q, k, v, beta, g, s0):
    f32 = jnp.float32
    bsz, t, nh, dk = q.shape
    dv = v.shape[-1]
    c = min(DN_CHUNK, t)
    nc = -(-t // c)
    pad = nc * c - t
    q = q.astype(f32) * dk ** -0.5
    k, v, beta, g = k.astype(f32), v.astype(f32), beta.astype(f32), g.astype(f32)
    padt = lambda a: jnp.pad(a, ((0, 0), (0, pad)) + ((0, 0),) * (a.ndim - 2))
    qc, kc, vc = (_to_chunks(padt(a), nc, c) for a in (q, k, v))
    bc, gc = (_to_chunks(padt(a), nc, c) for a in (beta, g))
    g_cum = jnp.cumsum(gc, axis=-1)
    idx = jnp.arange(c)
    lower = idx[:, None] >= idx[None, :]
    strict = idx[:, None] > idx[None, :]
    decay = jnp.exp(jnp.where(lower, g_cum[..., :, None] - g_cum[..., None, :], -jnp.inf))
    k_beta = kc * bc[..., None]
    v_beta = vc * bc[..., None]
    a_mat = jnp.einsum("bhnid,bhnjd->bhnij", k_beta, kc) * jnp.where(strict, decay, 0.0)
    eye = jnp.eye(c, dtype=f32)
    t_mat = lax.linalg.triangular_solve(eye + a_mat, jnp.broadcast_to(eye, a_mat.shape),
                                        left_side=True, lower=True)
    u = t_mat @ v_beta
    w = t_mat @ (k_beta * jnp.exp(g_cum)[..., None])
    attn = jnp.einsum("bhnid,bhnjd->bhnij", qc, kc) * decay
    xs = tuple(jnp.moveaxis(a, 2, 0) for a in (qc, kc, u, w, g_cum, attn))

    def chunk_step(s, inp):
        q_i, k_i, u_i, w_i, g_i, attn_i = inp
        v_new = u_i - jnp.einsum("bhck,bhkv->bhcv", w_i, s)
        o_i = (jnp.einsum("bhck,bhkv->bhcv", q_i * jnp.exp(g_i)[..., None], s)
               + jnp.einsum("bhij,bhjv->bhiv", attn_i, v_new))
        g_last = g_i[..., -1:]
        k_dec = k_i * jnp.exp(g_last - g_i)[..., None]
        s = s * jnp.exp(g_last)[..., None] + jnp.einsum("bhck,bhcv->bhkv", k_dec, v_new)
        return s, o_i

    s_fin, o = lax.scan(chunk_step, s0.astype(f32), xs)
    o = o.transpose(1, 0, 3, 2, 4).reshape(bsz, nc * c, nh, dv)[:, :t]
    return o, s_fin


def gated_deltanet(qkv_raw, b_raw, a_raw, gate, conv_buf, s0, p):
    bsz, t, _ = qkv_raw.shape
    qkv, new_buf = causal_dwconv(qkv_raw, conv_buf, p["dn_conv_w"])
    qkv = jax.nn.silu(qkv).astype(jnp.float32)
    q, k, v = jnp.split(qkv, 3, axis=-1)
    q = l2_norm(q.reshape(bsz, t, DN_HEADS, DN_HEAD_DIM))
    k = l2_norm(k.reshape(bsz, t, DN_HEADS, DN_HEAD_DIM))
    v = v.reshape(bsz, t, DN_HEADS, DN_HEAD_DIM)
    beta = jax.nn.sigmoid(b_raw.astype(jnp.float32))
    g = -jnp.exp(p["dn_a_log"].astype(jnp.float32)) * jax.nn.softplus(
        a_raw.astype(jnp.float32) + p["dn_dt_bias"].astype(jnp.float32))
    o, s_new = gated_delta_chunked(q, k, v, beta, g, s0)
    o = rms_norm(o, p["dn_norm"]) * jax.nn.silu(gate.astype(jnp.float32).reshape(bsz, t, DN_HEADS, DN_HEAD_DIM))
    return o.reshape(bsz, t, DN_WIDTH).astype(qkv_raw.dtype), s_new.astype(qkv_raw.dtype), new_buf


def hybrid_layer(x, k_past, v_past, ssm_re0, ssm_im0, dn_s0, dn_buf0, ffn_buf0, p):
    bsz, t, _ = x.shape
    h = rms_norm(x, p["norm_mix"])
    proj = h @ p["w_in"]
    q_a, k_a, v_a, u_b, qkv_c, beta_c, alpha_c, gate_c = jnp.split(
        proj, [int(s) for s in np.cumsum(SPLIT_SIZES)[:-1]], axis=-1)
    q_a = rms_norm(q_a.reshape(bsz, t, SB_HEADS, SB_HEAD_DIM), p["sb_q_norm"])
    k_a = rms_norm(k_a.reshape(bsz, t, SB_HEADS, SB_HEAD_DIM), p["sb_k_norm"])
    v_a = v_a.reshape(bsz, t, SB_HEADS, SB_HEAD_DIM)
    if k_past is None:
        k_all, v_all = k_a, v_a
    else:
        k_all = jnp.concatenate([k_past.astype(k_a.dtype), k_a], axis=1)
        v_all = jnp.concatenate([v_past.astype(v_a.dtype), v_a], axis=1)
    o_a = stick_breaking_attention(q_a, k_all, v_all, p["sb_logit_bias"]).reshape(bsz, t, SB_WIDTH)
    o_b, ssm_re, ssm_im = s5_mixer(u_b, ssm_re0, ssm_im0, p)
    o_c, dn_s, dn_buf = gated_deltanet(qkv_c, beta_c, alpha_c, gate_c, dn_buf0, dn_s0, p)
    mix = jnp.concatenate([rms_norm(o_a, p["norm_out_sb"]), rms_norm(o_b, p["norm_out_ssm"]), o_c], axis=-1)
    x = x + mix @ p["w_out"]
    h = rms_norm(x, p["norm_ffn"])
    up_gate, up_val = jnp.split(h @ p["ffn_w_in"], 2, axis=-1)
    up_gate_c, ffn_buf = causal_dwconv(up_gate, ffn_buf0, p["ffn_conv_w"])
    x = x + (jax.nn.silu(up_gate_c) * up_val) @ p["ffn_w_out"]
    return x, (k_a, v_a, ssm_re, ssm_im, dn_s, dn_buf, ffn_buf)


def setup_inputs(seed: int = 0) -> dict:
    key = jax.random.key(seed)
    ks = iter(jax.random.split(key, 48))
    f32 = jnp.float32

    def nrm(shape, scale=1.0):
        return jax.random.normal(next(ks), shape, f32) * scale

    def gain(shape):
        return 1.0 + nrm(shape, 0.01)

    def unif(shape, lo, hi):
        return jax.random.uniform(next(ks), shape, f32, lo, hi)

    n_pages = PAST_LEN // PAGE_SIZE
    n_used = DEC_BATCH * n_pages
    n_pool = n_used + (n_used + 3) // 4
    page_table = jax.random.permutation(next(ks), n_pool)[:n_used].reshape(DEC_BATCH, n_pages).astype(jnp.int32)
    dn_dt = jnp.exp(unif((DEPTH, DN_HEADS), math.log(SSM_DT_MIN), math.log(SSM_DT_MAX)))
    return {
        "x_prompt": nrm((BATCH, SEQ, D_MODEL)),
        "x_sample": nrm((DEC_BATCH, DEC_SEQ, D_MODEL)),
        "cache_k": nrm((DEPTH, n_pool, PAGE_SIZE, SB_HEADS, SB_HEAD_DIM)),
        "cache_v": nrm((DEPTH, n_pool, PAGE_SIZE, SB_HEADS, SB_HEAD_DIM)),
        "page_table": page_table,
        "state_ssm_re": nrm((DEPTH, DEC_BATCH, SSM_GROUPS, SSM_STATE), 0.3),
        "state_ssm_im": nrm((DEPTH, DEC_BATCH, SSM_GROUPS, SSM_STATE), 0.3),
        "state_delta": nrm((DEPTH, DEC_BATCH, DN_HEADS, DN_HEAD_DIM, DN_HEAD_DIM), DN_HEAD_DIM ** -0.5),
        "state_delta_conv": nrm((DEPTH, DEC_BATCH, DN_CONV - 1, 3 * DN_WIDTH)),
        "state_ffn_conv": nrm((DEPTH, DEC_BATCH, FFN_CONV - 1, D_FF)),
        "norm_mix": gain((DEPTH, D_MODEL)),
        "w_in": nrm((DEPTH, D_MODEL, IN_COLS), D_MODEL ** -0.5),
        "sb_q_norm": gain((DEPTH, SB_HEAD_DIM)),
        "sb_k_norm": gain((DEPTH, SB_HEAD_DIM)),
        "sb_logit_bias": SB_LOGIT_BIAS_INIT + nrm((DEPTH, SB_HEADS), 0.1),
        "ssm_lambda_re": -0.5 + nrm((DEPTH, SSM_GROUPS, SSM_STATE), 0.01),
        "ssm_lambda_im": jnp.pi * jnp.arange(SSM_STATE, dtype=f32) + nrm((DEPTH, SSM_GROUPS, SSM_STATE), 0.01),
        "ssm_log_step": unif((DEPTH, SSM_GROUPS), math.log(SSM_DT_MIN), math.log(SSM_DT_MAX)),
        "ssm_b_re": nrm((DEPTH, SSM_GROUPS, SSM_STATE, SSM_GROUP_CH), (2 * SSM_GROUP_CH) ** -0.5),
        "ssm_b_im": nrm((DEPTH, SSM_GROUPS, SSM_STATE, SSM_GROUP_CH), (2 * SSM_GROUP_CH) ** -0.5),
        "ssm_c_re": nrm((DEPTH, SSM_GROUPS, SSM_GROUP_CH, SSM_STATE), SSM_STATE ** -0.5),
        "ssm_c_im": nrm((DEPTH, SSM_GROUPS, SSM_GROUP_CH, SSM_STATE), SSM_STATE ** -0.5),
        "ssm_d": nrm((DEPTH, SSM_WIDTH)),
        "ssm_w_glu": nrm((DEPTH, SSM_WIDTH, SSM_WIDTH), SSM_WIDTH ** -0.5),
        "ssm_b_glu": nrm((DEPTH, SSM_WIDTH), 0.01),
        "dn_conv_w": nrm((DEPTH, DN_CONV, 3 * DN_WIDTH), DN_CONV ** -0.5),
        "dn_a_log": jnp.log(unif((DEPTH, DN_HEADS), 1.0, 16.0)),
        "dn_dt_bias": dn_dt + jnp.log(-jnp.expm1(-dn_dt)),
        "dn_norm": gain((DEPTH, DN_HEAD_DIM)),
        "norm_out_sb": gain((DEPTH, SB_WIDTH)),
        "norm_out_ssm": gain((DEPTH, SSM_WIDTH)),
        "w_out": nrm((DEPTH, D_MODEL, D_MODEL), D_MODEL ** -0.5),
        "norm_ffn": gain((DEPTH, D_MODEL)),
        "ffn_w_in": nrm((DEPTH, D_MODEL, 2 * D_FF), D_MODEL ** -0.5),
        "ffn_conv_w": nrm((DEPTH, FFN_CONV, D_FF), FFN_CONV ** -0.5),
        "ffn_w_out": nrm((DEPTH, D_FF, D_MODEL), D_FF ** -0.5),
    }


def reference(x_prompt, x_sample, cache_k, cache_v, page_table, state_ssm_re, state_ssm_im,
              state_delta, state_delta_conv, state_ffn_conv, norm_mix, w_in, sb_q_norm, sb_k_norm,
              sb_logit_bias, ssm_lambda_re, ssm_lambda_im, ssm_log_step, ssm_b_re, ssm_b_im, ssm_c_re, ssm_c_im,
              ssm_d, ssm_w_glu, ssm_b_glu, dn_conv_w, dn_a_log, dn_dt_bias, dn_norm,
              norm_out_sb, norm_out_ssm, w_out, norm_ffn, ffn_w_in, ffn_conv_w, ffn_w_out):
    bp = x_prompt.shape[0]
    bd = x_sample.shape[0]
    dt = x_prompt.dtype
    y_p, y_s = x_prompt, x_sample
    outs_p, outs_s = [], []
    for l in range(DEPTH):
        p = {
            "norm_mix": norm_mix[l], "w_in": w_in[l], "sb_q_norm": sb_q_norm[l], "sb_k_norm": sb_k_norm[l],
            "sb_logit_bias": sb_logit_bias[l],
            "ssm_lambda_re": ssm_lambda_re[l], "ssm_lambda_im": ssm_lambda_im[l], "ssm_log_step": ssm_log_step[l],
            "ssm_b_re": ssm_b_re[l], "ssm_b_im": ssm_b_im[l], "ssm_c_re": ssm_c_re[l], "ssm_c_im": ssm_c_im[l],
            "ssm_d": ssm_d[l], "ssm_w_glu": ssm_w_glu[l], "ssm_b_glu": ssm_b_glu[l],
            "dn_conv_w": dn_conv_w[l], "dn_a_log": dn_a_log[l], "dn_dt_bias": dn_dt_bias[l], "dn_norm": dn_norm[l],
            "norm_out_sb": norm_out_sb[l], "norm_out_ssm": norm_out_ssm[l], "w_out": w_out[l],
            "norm_ffn": norm_ffn[l], "ffn_w_in": ffn_w_in[l], "ffn_conv_w": ffn_conv_w[l], "ffn_w_out": ffn_w_out[l],
        }
        y_p, st_p = hybrid_layer(
            y_p, None, None,
            jnp.zeros((bp, SSM_GROUPS, SSM_STATE), dt), jnp.zeros((bp, SSM_GROUPS, SSM_STATE), dt),
            jnp.zeros((bp, DN_HEADS, DN_HEAD_DIM, DN_HEAD_DIM), dt),
            jnp.zeros((bp, DN_CONV - 1, 3 * DN_WIDTH), dt),
            jnp.zeros((bp, FFN_CONV - 1, D_FF), dt), p)
        k_past = cache_k[l][page_table].reshape(bd, -1, SB_HEADS, SB_HEAD_DIM)
        v_past = cache_v[l][page_table].reshape(bd, -1, SB_HEADS, SB_HEAD_DIM)
        y_s, st_s = hybrid_layer(
            y_s, k_past, v_past, state_ssm_re[l], state_ssm_im[l], state_delta[l],
            state_delta_conv[l], state_ffn_conv[l], p)
        outs_p.append(st_p)
        outs_s.append(st_s)
    k_p, v_p, sre_p, sim_p, dn_p, dnc_p, ffc_p = [jnp.stack(z) for z in zip(*outs_p)]
    k_s, v_s, sre_s, sim_s, dn_s, dnc_s, ffc_s = [jnp.stack(z) for z in zip(*outs_s)]
    return (y_p, y_s, k_p, v_p, k_s, v_s, sre_p, sim_p, sre_s, sim_s, dn_p, dn_s, dnc_p, dnc_s, ffc_p, ffc_s)
```

```python
import functools
import math

import jax
import jax.numpy as jnp
from jax import lax
from jax.experimental import pallas as pl
from jax.experimental.pallas import tpu as pltpu

F32 = jnp.float32
BF16 = jnp.bfloat16
HIGHEST = lax.Precision.HIGHEST
NORM_EPS = 1e-6
LANES = 128
SUBLANES = 8
BF16_ROWS = 16
VMEM_LIMIT_BYTES = 56 * 1024 * 1024
DN_CHUNK = 64
TRI_BASE = 16

_NT = (((1,), (1,)), ((), ()))
_TN = (((0,), (0,)), ((), ()))


def _tile(dim, pref, mult=LANES):
    if dim <= pref:
        return dim
    t = (pref // mult) * mult
    while t >= mult:
        if dim % t == 0:
            return t
        t -= mult
    raise ValueError(f"no tile for {dim} (pref {pref}, mult {mult})")


def _params(*sem):
    return pltpu.CompilerParams(dimension_semantics=sem, vmem_limit_bytes=VMEM_LIMIT_BYTES)


def _mm(a, b, dims=None):
    ok = all(d % BF16_ROWS == 0 for d in a.shape + b.shape)
    if ok:
        a, b = a.astype(BF16), b.astype(BF16)
    else:
        a, b = a.astype(F32), b.astype(F32)
    if dims is None:
        return jnp.dot(a, b, preferred_element_type=F32)
    return lax.dot_general(a, b, dims, preferred_element_type=F32)


def _mm_f32(a, b, dims=None):
    if dims is None:
        return jnp.dot(a, b, preferred_element_type=F32, precision=HIGHEST)
    return lax.dot_general(a, b, dims, preferred_element_type=F32, precision=HIGHEST)


def _softplus(z):
    return jnp.maximum(z, 0.0) + jnp.log1p(jnp.exp(-jnp.abs(z)))


def _sigmoid(z):
    return 1.0 / (1.0 + jnp.exp(-z))


def _rms(x, gain, eps=NORM_EPS):
    return x * lax.rsqrt(jnp.mean(x * x, axis=-1, keepdims=True) + eps) * gain


def _inproj_kernel(x_ref, g_ref, w_ref, gs_ref, e_ref, qg_ref, kg_ref, *refs, head_dim, q_scale, aliased):
    q_ref, k_ref, v_ref, r_ref, xn_ref = refs[2:] if aliased else refs
    j = pl.program_id(1)

    @pl.when(j == 0)
    def _():
        xn_ref[...] = _rms(x_ref[...], g_ref[...]).astype(BF16)

    y = jnp.dot(xn_ref[...], w_ref[...], preferred_element_type=F32)

    def head_norm(gain):
        ms = jnp.dot((y * y).astype(BF16), gs_ref[...], preferred_element_type=F32) * (1.0 / head_dim)
        inv = lax.rsqrt(ms + NORM_EPS)
        hi = inv.astype(BF16)
        lo = (inv - hi.astype(F32)).astype(BF16)
        sc = (jnp.dot(hi, e_ref[...], preferred_element_type=F32)
              + jnp.dot(lo, e_ref[...], preferred_element_type=F32))
        return y * sc * gain

    @pl.when(j == 0)
    def _():
        q_ref[...] = (head_norm(qg_ref[...]) * q_scale).astype(BF16)

    @pl.when(j == 1)
    def _():
        k_ref[...] = head_norm(kg_ref[...])

    @pl.when(j == 2)
    def _():
        v_ref[...] = y

    @pl.when(j >= 3)
    def _():
        r_ref[...] = y


def _inproj(x2d, gain, w_perm, gs, e, qg, kg, kbuf, vbuf, layer, depth, head_dim, q_scale):
    m, d = x2d.shape
    tn = gs.shape[0]
    n_rest = w_perm.shape[1] - 3 * tn
    tm = _tile(m, 512, SUBLANES)
    aliased = kbuf is not None
    const = lambda i, j: (0, 0)
    in_specs = [
        pl.BlockSpec((tm, d), lambda i, j: (i, 0)),
        pl.BlockSpec((1, d), const),
        pl.BlockSpec((d, tn), lambda i, j: (0, j)),
        pl.BlockSpec((tn, LANES), const),
        pl.BlockSpec((LANES, tn), const),
        pl.BlockSpec((1, tn), const),
        pl.BlockSpec((1, tn), const),
    ]
    args = [x2d, gain, w_perm, gs, e, qg, kg]
    aliases = {}
    if aliased:
        in_specs += [pl.BlockSpec(memory_space=pl.ANY), pl.BlockSpec(memory_space=pl.ANY)]
        args += [kbuf, vbuf]
        aliases = {7: 1, 8: 2}
    out_specs = [
        pl.BlockSpec((tm, tn), lambda i, j: (i, 0)),
        pl.BlockSpec((None, tm, tn), lambda i, j: (layer, i, 0)),
        pl.BlockSpec((None, tm, tn), lambda i, j: (layer, i, 0)),
        pl.BlockSpec((tm, tn), lambda i, j: (i, jnp.maximum(j - 3, 0))),
    ]
    out_shape = [
        jax.ShapeDtypeStruct((m, tn), BF16),
        jax.ShapeDtypeStruct((depth, m, tn), F32),
        jax.ShapeDtypeStruct((depth, m, tn), F32),
        jax.ShapeDtypeStruct((m, n_rest), F32),
    ]
    return pl.pallas_call(
        functools.partial(_inproj_kernel, head_dim=head_dim, q_scale=q_scale, aliased=aliased),
        grid=(m // tm, 3 + n_rest // tn),
        in_specs=in_specs, out_specs=out_specs, out_shape=out_shape,
        scratch_shapes=[pltpu.VMEM((tm, d), BF16)],
        input_output_aliases=aliases,
        compiler_params=_params("parallel", "arbitrary"),
        name="inproj",
    )(*args)


def _sb_prompt_kernel(bias_ref, q_ref, k_ref, v_ref, o_ref, *, blk, head_dim, heads_per_step):
    hp = pl.program_id(1)
    qi = pl.program_id(2)
    row = lax.broadcasted_iota(jnp.int32, (blk, blk), 0)
    col = lax.broadcasted_iota(jnp.int32, (blk, blk), 1)
    later = (row > col).astype(BF16)
    visible = col < row
    q = q_ref[...]
    outs = []
    for hh in range(heads_per_step):
        lo = hh * head_dim
        qh = q[:, lo:lo + head_dim]
        bias = bias_ref[hp * heads_per_step + hh]

        def block(ki, o_acc, carry, masked):
            r0 = pl.multiple_of(ki * blk, blk)
            kb = k_ref[pl.ds(r0, blk), lo:lo + head_dim].astype(BF16)
            vb = v_ref[pl.ds(r0, blk), lo:lo + head_dim].astype(BF16)
            z = lax.dot_general(qh, kb, _NT, preferred_element_type=F32) + bias
            sp = _softplus(z)
            if masked:
                sp = jnp.where(visible, sp, 0.0)
            after = jnp.dot(sp.astype(BF16), later, preferred_element_type=F32)
            w = jnp.exp(z - sp - after - carry)
            if masked:
                w = jnp.where(visible, w, 0.0)
            o_acc = o_acc + jnp.dot(w.astype(BF16), vb, preferred_element_type=F32)
            carry = carry + after[:, 0:1] + sp[:, 0:1]
            return o_acc, carry

        o0 = jnp.zeros((blk, head_dim), F32)
        c0 = jnp.zeros((blk, 1), F32)
        o_acc, carry = block(qi, o0, c0, True)
        o_acc, _ = lax.fori_loop(
            0, qi, lambda it, oc: block(qi - 1 - it, oc[0], oc[1], False), (o_acc, carry))
        outs.append(o_acc)
    o_ref[...] = jnp.concatenate(outs, axis=-1) if len(outs) > 1 else outs[0]


def _sb_prompt(q, kbuf, vbuf, bias, layer, bsz, seq, head_dim):
    width = q.shape[-1]
    heads_per_step = LANES // head_dim
    blk = _tile(seq, 256)
    q3 = q.reshape(bsz, seq, width)
    k4 = kbuf.reshape(kbuf.shape[0], bsz, seq, width)
    v4 = vbuf.reshape(vbuf.shape[0], bsz, seq, width)
    return pl.pallas_call(
        functools.partial(_sb_prompt_kernel, blk=blk, head_dim=head_dim, heads_per_step=heads_per_step),
        grid=(bsz, width // LANES, seq // blk),
        in_specs=[
            pl.BlockSpec(memory_space=pltpu.SMEM),
            pl.BlockSpec((None, blk, LANES), lambda b, h, i: (b, i, h)),
            pl.BlockSpec((None, None, seq, LANES), lambda b, h, i: (layer, b, 0, h)),
            pl.BlockSpec((None, None, seq, LANES), lambda b, h, i: (layer, b, 0, h)),
        ],
        out_specs=pl.BlockSpec((None, blk, LANES), lambda b, h, i: (b, i, h)),
        out_shape=jax.ShapeDtypeStruct((bsz, seq, width), F32),
        compiler_params=_params("parallel", "parallel", "arbitrary"),
        name="sb_prompt",
    )(bias, q3, k4, v4).reshape(bsz * seq, width)


def _sb_sample_kernel(pt_ref, q_ref, mask_ref, brow_ref, kn_ref, vn_ref, *refs, pages_per_step, tq):
    del pt_ref
    kp = refs[:pages_per_step]
    vp = refs[pages_per_step:2 * pages_per_step]
    o_ref, acc_ref, carry_ref = refs[2 * pages_per_step:]
    c = pl.program_id(1)
    mask = mask_ref[...]
    qbd = jnp.concatenate([q_ref[...].astype(F32)] * (LANES // tq), axis=0) * mask
    brow = brow_ref[...]

    def process(kblk, vblk, qmat, visible, exact):
        n = kblk.shape[0]
        z = lax.dot_general(kblk, qmat, _NT, preferred_element_type=F32) + brow
        sp = _softplus(z)
        if visible is not None:
            sp = jnp.where(visible, sp, 0.0)
        row = lax.broadcasted_iota(jnp.int32, (n, n), 0)
        col = lax.broadcasted_iota(jnp.int32, (n, n), 1)
        later = col > row
        if exact:
            after = _mm_f32(later.astype(F32), sp)
        else:
            after = jnp.dot(later.astype(BF16), sp.astype(BF16), preferred_element_type=F32)
        w = jnp.exp(z - sp - after - carry_ref[...])
        if visible is not None:
            w = jnp.where(visible, w, 0.0)
        acc_ref[...] += lax.dot_general(w.astype(kblk.dtype), vblk, _TN, preferred_element_type=F32)
        carry_ref[...] += jnp.sum(sp, axis=0, keepdims=True)

    @pl.when(c == 0)
    def _():
        acc_ref[...] = jnp.zeros_like(acc_ref)
        carry_ref[...] = jnp.zeros_like(carry_ref)
        key = lax.broadcasted_iota(jnp.int32, (tq, LANES), 0)
        qry = lax.broadcasted_iota(jnp.int32, (tq, LANES), 1) & (tq - 1)
        process(kn_ref[...], vn_ref[...], qbd, key < qry, True)

    qbd16 = qbd.astype(BF16)
    for p in range(pages_per_step):
        process(kp[p][...].astype(BF16), vp[p][...].astype(BF16), qbd16, None, False)

    @pl.when(c == pl.num_programs(1) - 1)
    def _():
        acc = acc_ref[...] * mask
        o_ref[...] = acc.reshape(LANES // tq, tq, acc.shape[-1]).sum(axis=0)


def _sb_sample(q, kbuf, vbuf, cache_k, cache_v, page_table, bias, layer, bsz, tq, head_dim):
    width = q.shape[-1]
    heads = width // head_dim
    assert heads * tq <= LANES and tq & (tq - 1) == 0
    n_pool, page = cache_k.shape[1], cache_k.shape[2]
    npages = page_table.shape[1]
    pps = _tile(npages, 8, 1)
    ck = cache_k.reshape(cache_k.shape[0], n_pool, page, width)
    cv = cache_v.reshape(cache_v.shape[0], n_pool, page, width)
    lane = jnp.arange(LANES)
    mask = ((lane[:, None] // tq) == (jnp.arange(width)[None, :] // head_dim)).astype(F32)
    brow = jnp.where(lane < heads * tq, bias[jnp.minimum(lane // tq, heads - 1)], 0.0).astype(F32)[None, :]

    def page_map(p):
        return lambda b, c, pt: (layer, pt[b * npages + npages - 1 - (c * pps + p)], 0, 0)

    page_specs = [pl.BlockSpec((None, None, page, width), page_map(p)) for p in range(pps)]
    grid_spec = pltpu.PrefetchScalarGridSpec(
        num_scalar_prefetch=1,
        grid=(bsz, npages // pps),
        in_specs=[
            pl.BlockSpec((None, tq, width), lambda b, c, pt: (b, 0, 0)),
            pl.BlockSpec((LANES, width), lambda b, c, pt: (0, 0)),
            pl.BlockSpec((1, LANES), lambda b, c, pt: (0, 0)),
            pl.BlockSpec((None, None, tq, width), lambda b, c, pt: (layer, b, 0, 0)),
            pl.BlockSpec((None, None, tq, width), lambda b, c, pt: (layer, b, 0, 0)),
        ] + page_specs + page_specs,
        out_specs=pl.BlockSpec((None, tq, width), lambda b, c, pt: (b, 0, 0)),
        scratch_shapes=[pltpu.VMEM((LANES, width), F32), pltpu.VMEM((1, LANES), F32)],
    )
    k4 = kbuf.reshape(kbuf.shape[0], bsz, tq, width)
    v4 = vbuf.reshape(vbuf.shape[0], bsz, tq, width)
    out = pl.pallas_call(
        functools.partial(_sb_sample_kernel, pages_per_step=pps, tq=tq),
        grid_spec=grid_spec,
        out_shape=jax.ShapeDtypeStruct((bsz, tq, width), F32),
        compiler_params=_params("parallel", "arbitrary"),
        name="sb_sample",
    )(page_table.reshape(-1), q.reshape(bsz, tq, width), mask, brow, k4, v4,
      *([ck] * pps), *([cv] * pps))
    return out.reshape(bsz * tq, width)


def _s5_discretize_kernel(lre_ref, lim_ref, ls_ref, bre_ref, bim_ref, are_ref, aim_ref, bbre_ref, bbim_ref):
    lam_re, lam_im = lre_ref[...], lim_ref[...]
    dt = jnp.exp(ls_ref[...])
    mag = jnp.exp(lam_re * dt)
    ab_re = mag * jnp.cos(lam_im * dt)
    ab_im = mag * jnp.sin(lam_im * dt)
    den = lam_re * lam_re + lam_im * lam_im
    xr = ab_re - 1.0
    f_re = (xr * lam_re + ab_im * lam_im) / den
    f_im = (ab_im * lam_re - xr * lam_im) / den
    are_ref[...] = ab_re
    aim_ref[...] = ab_im
    bbre_ref[...] = f_re * bre_ref[...] - f_im * bim_ref[...]
    bbim_ref[...] = f_re * bim_ref[...] + f_im * bre_ref[...]


def _s5_discretize(lam_re, lam_im, log_step, b_re, b_im):
    g, n = lam_re.shape
    ch = b_re.shape[-1]
    col = lambda a: a.reshape(g * n, 1)
    ls = jnp.broadcast_to(log_step[:, None], (g, n))
    return pl.pallas_call(
        _s5_discretize_kernel,
        out_shape=[jax.ShapeDtypeStruct((g * n, 1), F32)] * 2 + [jax.ShapeDtypeStruct((g * n, ch), F32)] * 2,
        name="s5_discretize",
    )(col(lam_re), col(lam_im), col(ls), b_re.reshape(g * n, ch), b_im.reshape(g * n, ch))


def _s5_kernel(u_ref, bmat_ref, arow_ref, h0_ref, cmat_ref, d_ref, wglu_ref, bglu_ref,
               y_ref, hfin_ref, hbuf_ref, hc_ref, *, tt, nb, ns, cb):
    ti = pl.program_id(1)
    width = u_ref.shape[-1]

    @pl.when(ti == 0)
    def _():
        hc_ref[...] = h0_ref[...]

    u = u_ref[...].reshape(tt * nb, width)
    hbuf_ref[...] = jnp.dot(u.astype(BF16), bmat_ref[...], preferred_element_type=F32)

    for c in range(ns // cb):
        re = slice(c * cb, (c + 1) * cb)
        im = slice(ns + c * cb, ns + (c + 1) * cb)
        a_re = jnp.broadcast_to(arow_ref[:, re], (SUBLANES, cb))
        a_im = jnp.broadcast_to(arow_ref[:, im], (SUBLANES, cb))

        def batch_rows(bb, _):
            r0 = pl.multiple_of(bb * SUBLANES, SUBLANES)

            def step(t, h):
                h_re, h_im = h
                r = pl.multiple_of(t * nb + r0, SUBLANES)
                n_re = a_re * h_re - a_im * h_im + hbuf_ref[pl.ds(r, SUBLANES), re]
                n_im = a_re * h_im + a_im * h_re + hbuf_ref[pl.ds(r, SUBLANES), im]
                hbuf_ref[pl.ds(r, SUBLANES), re] = n_re
                hbuf_ref[pl.ds(r, SUBLANES), im] = n_im
                return n_re, n_im

            h = (hc_ref[pl.ds(r0, SUBLANES), re], hc_ref[pl.ds(r0, SUBLANES), im])
            h = lax.fori_loop(0, tt, step, h, unroll=min(tt, 8))
            hc_ref[pl.ds(r0, SUBLANES), re] = h[0]
            hc_ref[pl.ds(r0, SUBLANES), im] = h[1]
            return 0

        lax.fori_loop(0, nb // SUBLANES, batch_rows, 0)

    y = jnp.dot(hbuf_ref[...].astype(BF16), cmat_ref[...], preferred_element_type=F32) + d_ref[...] * u
    y = 0.5 * y * (1.0 + jnp.tanh(math.sqrt(2.0 / math.pi) * (y + 0.044715 * (y * y * y))))
    z = jnp.dot(y.astype(BF16), wglu_ref[...], preferred_element_type=F32) + bglu_ref[...]
    y_ref[...] = (y * _sigmoid(z)).reshape(tt, nb, width)

    @pl.when(ti == pl.num_programs(1) - 1)
    def _():
        hfin_ref[...] = hc_ref[...]


def _s5(u_tm, bmat, arow, h0, cmat, d_row, wglu, bglu):
    t, b, width = u_tm.shape
    ns2 = bmat.shape[1]
    nb = _tile(b, 64, SUBLANES)
    tt = _tile(t, max(SUBLANES, 512 // nb), 1)
    const = lambda i, j: (0, 0)
    return pl.pallas_call(
        functools.partial(_s5_kernel, tt=tt, nb=nb, ns=ns2 // 2, cb=_tile(ns2 // 2, 512)),
        grid=(b // nb, t // tt),
        in_specs=[
            pl.BlockSpec((tt, nb, width), lambda i, j: (j, i, 0)),
            pl.BlockSpec((width, ns2), const),
            pl.BlockSpec((1, ns2), const),
            pl.BlockSpec((nb, ns2), lambda i, j: (i, 0)),
            pl.BlockSpec((ns2, width), const),
            pl.BlockSpec((1, width), const),
            pl.BlockSpec((width, width), const),
            pl.BlockSpec((1, width), const),
        ],
        out_specs=[
            pl.BlockSpec((tt, nb, width), lambda i, j: (j, i, 0)),
            pl.BlockSpec((nb, ns2), lambda i, j: (i, 0)),
        ],
        out_shape=[jax.ShapeDtypeStruct((t, b, width), F32), jax.ShapeDtypeStruct((b, ns2), F32)],
        scratch_shapes=[pltpu.VMEM((tt * nb, ns2), F32), pltpu.VMEM((nb, ns2), F32)],
        compiler_params=_params("parallel", "arbitrary"),
        name="s5",
    )(u_tm, bmat, arow, h0, cmat, d_row, wglu, bglu)


def _tri_inverse(a, n):
    row = lax.broadcasted_iota(jnp.int32, (n, n), 0)
    col = lax.broadcasted_iota(jnp.int32, (n, n), 1)
    eye = (row == col).astype(F32)

    def same_block(b):
        s = b.bit_length() - 1
        return lax.shift_right_logical(row, s) == lax.shift_right_logical(col, s)

    b0 = min(n, TRI_BASE)
    neg = -jnp.where(same_block(b0), a, 0.0)
    t = eye + neg
    p = neg
    k = 1
    while 2 * k < b0:
        p = _mm_f32(p, p)
        t = t + _mm_f32(t, p)
        k *= 2
    b = b0
    while b < n:
        off = jnp.where(jnp.logical_and(same_block(2 * b), jnp.logical_not(same_block(b))), a, 0.0)
        t = t - _mm_f32(_mm_f32(t, off), t)
        b *= 2
    return t


def _dn_kernel(q_ref, k_ref, v_ref, ba_ref, gate_ref, cw_ref, prm_ref, nrm_ref, cbuf_ref, s0_ref,
               o_ref, sout_ref, cout_ref,
               xpad_ref, qs_ref, ks_ref, vs_ref, gb_ref, s_ref, *, bs, tt, chunk, nh, hd, taps):
    ti = pl.program_id(1)
    width = nh * hd
    halo = SUBLANES
    first = halo - (taps - 1)

    @pl.when(ti == 0)
    def _():
        s_ref[...] = s0_ref[...]
        xpad_ref[:, first:halo, :] = cbuf_ref[...]

    xpad_ref[:, halo:halo + tt, 0:width] = q_ref[...]
    xpad_ref[:, halo:halo + tt, width:2 * width] = k_ref[...]
    xpad_ref[:, halo:halo + tt, 2 * width:3 * width] = v_ref[...]
    cw = cw_ref[...]
    y = cw[0:1, :] * xpad_ref[:, first:first + tt, :]
    for i in range(1, taps):
        y = y + cw[i:i + 1, :] * xpad_ref[:, first + i:first + i + tt, :]
    cout_ref[...] = xpad_ref[:, halo + tt - (taps - 1):halo + tt, :]
    xpad_ref[:, 0:halo, :] = xpad_ref[:, tt:tt + halo, :]
    act = y * _sigmoid(y)
    for h in range(nh):
        sl = slice(h * hd, (h + 1) * hd)
        qh = act[:, :, h * hd:(h + 1) * hd]
        kh = act[:, :, width + h * hd:width + (h + 1) * hd]
        qs_ref[:, :, sl] = qh * lax.rsqrt(jnp.sum(qh * qh, axis=-1, keepdims=True) + NORM_EPS) * (hd ** -0.5)
        ks_ref[:, :, sl] = kh * lax.rsqrt(jnp.sum(kh * kh, axis=-1, keepdims=True) + NORM_EPS)
    vs_ref[...] = act[:, :, 2 * width:3 * width]
    ba = ba_ref[...]
    lane = lax.broadcasted_iota(jnp.int32, ba.shape, 2)
    gval = -jnp.exp(prm_ref[0:1, :]) * _softplus(ba + prm_ref[1:2, :])
    gb_ref[...] = jnp.where(lane < nh, _sigmoid(ba), jnp.where(lane < 2 * nh, gval, 0.0))

    row = lax.broadcasted_iota(jnp.int32, (chunk, chunk), 0)
    col = lax.broadcasted_iota(jnp.int32, (chunk, chunk), 1)
    lower = row >= col
    strict = row > col
    tri = lower.astype(F32)
    sel = (lax.broadcasted_iota(jnp.int32, (SUBLANES, LANES), 1)
           == lax.broadcasted_iota(jnp.int32, (SUBLANES, LANES), 0) + nh).astype(F32)

    def one_chunk(s, c):
        r = pl.multiple_of(c * chunk, chunk)
        rows = pl.ds(r, chunk)
        gbc = gb_ref[s, rows, :]
        gcum = _mm_f32(tri, gbc)
        grow = _mm_f32(sel, gcum, _NT)
        for h in range(nh):
            sl = slice(h * hd, (h + 1) * hd)
            gc = gcum[:, nh + h:nh + h + 1]
            beta = gbc[:, h:h + 1]
            decay = jnp.where(lower, jnp.exp(jnp.minimum(gc - grow[h:h + 1, :], 0.0)), 0.0)
            qh, kh, vh = qs_ref[s, rows, sl], ks_ref[s, rows, sl], vs_ref[s, rows, sl]
            kb = kh * beta
            a = _mm(kb, kh, _NT) * jnp.where(strict, decay, 0.0)
            tinv = _tri_inverse(a, chunk)
            u = _mm(tinv, vh * beta)
            w = _mm(tinv, kb * jnp.exp(gc))
            attn = _mm(qh, kh, _NT) * decay
            state = s_ref[s, h]
            v_new = u - _mm(w, state)
            o = _mm(qh * jnp.exp(gc), state) + _mm(attn, v_new)
            g_last = gc[chunk - 1:chunk, :]
            s_ref[s, h] = state * jnp.exp(g_last) + _mm(kh * jnp.exp(g_last - gc), v_new, _TN)
            gt = gate_ref[s, rows, sl]
            o_ref[s, rows, sl] = _rms(o, nrm_ref[...]) * (gt * _sigmoid(gt))

    def one_seq(s):
        if tt == chunk:
            one_chunk(s, 0)
        else:
            lax.fori_loop(0, tt // chunk, lambda c, _: (one_chunk(s, c), 0)[1], 0)

    if bs == 1:
        one_seq(0)
    else:
        lax.fori_loop(0, bs, lambda s, _: (one_seq(s), 0)[1], 0)

    @pl.when(ti == pl.num_programs(1) - 1)
    def _():
        sout_ref[...] = s_ref[...]


def _deltanet(rest3, col0, conv_w, prm, nrm, cbuf, s0, nh, hd):
    nseq, t, _ = rest3.shape
    width = nh * hd
    taps = conv_w.shape[0]
    chunk = min(DN_CHUNK, t)
    assert t % chunk == 0 and chunk & (chunk - 1) == 0 and t >= taps - 1
    tt = _tile(t, 256, chunk)
    bs = _tile(nseq, 8, 1) if tt == t and t <= SUBLANES else 1
    ba_col = (col0 + 4) * (width // LANES)
    blk = lambda cidx: pl.BlockSpec((bs, tt, width), lambda i, j: (i, j, cidx))
    const = lambda i, j: (0, 0)
    return pl.pallas_call(
        functools.partial(_dn_kernel, bs=bs, tt=tt, chunk=chunk, nh=nh, hd=hd, taps=taps),
        grid=(nseq // bs, t // tt),
        in_specs=[
            blk(col0 + 1), blk(col0 + 2), blk(col0 + 3),
            pl.BlockSpec((bs, tt, LANES), lambda i, j: (i, j, ba_col)),
            blk(col0),
            pl.BlockSpec((taps, 3 * width), const),
            pl.BlockSpec((SUBLANES, LANES), const),
            pl.BlockSpec((1, hd), const),
            pl.BlockSpec((bs, taps - 1, 3 * width), lambda i, j: (i, 0, 0)),
            pl.BlockSpec((bs, nh, hd, hd), lambda i, j: (i, 0, 0, 0)),
        ],
        out_specs=[
            pl.BlockSpec((bs, tt, width), lambda i, j: (i, j, 0)),
            pl.BlockSpec((bs, nh, hd, hd), lambda i, j: (i, 0, 0, 0)),
            pl.BlockSpec((bs, taps - 1, 3 * width), lambda i, j: (i, 0, 0)),
        ],
        out_shape=[
            jax.ShapeDtypeStruct((nseq, t, width), F32),
            jax.ShapeDtypeStruct((nseq, nh, hd, hd), F32),
            jax.ShapeDtypeStruct((nseq, taps - 1, 3 * width), F32),
        ],
        scratch_shapes=[
            pltpu.VMEM((bs, tt + 2 * SUBLANES, 3 * width), F32),
            pltpu.VMEM((bs, tt, width), F32),
            pltpu.VMEM((bs, tt, width), F32),
            pltpu.VMEM((bs, tt, width), F32),
            pltpu.VMEM((bs, tt, LANES), F32),
            pltpu.VMEM((bs, nh, hd, hd), F32),
        ],
        compiler_params=_params("parallel", "arbitrary"),
        name="deltanet",
    )(rest3, rest3, rest3, rest3, rest3, conv_w, prm, nrm, cbuf, s0)


def _outproj_kernel(oa_ref, ob_ref, oc_ref, ga_ref, gb_ref, w_ref, x_ref, o_ref, mix_ref):
    wa, wb = oa_ref.shape[-1], ob_ref.shape[-1]

    @pl.when(pl.program_id(1) == 0)
    def _():
        mix_ref[:, 0:wa] = _rms(oa_ref[...], ga_ref[...]).astype(BF16)
        mix_ref[:, wa:wa + wb] = _rms(ob_ref[...], gb_ref[...]).astype(BF16)
        mix_ref[:, wa + wb:] = oc_ref[...].astype(BF16)

    o_ref[...] = x_ref[...] + jnp.dot(mix_ref[...], w_ref[...], preferred_element_type=F32)


def _outproj(oa, ob, oc, ga, gb, w, x2d):
    m, d = x2d.shape
    tm = _tile(m, 512, SUBLANES)
    tn = _tile(d, 1024)
    row = lambda a: pl.BlockSpec((tm, a.shape[-1]), lambda i, j: (i, 0))
    const = lambda a: pl.BlockSpec((1, a.shape[-1]), lambda i, j: (0, 0))
    return pl.pallas_call(
        _outproj_kernel,
        grid=(m // tm, d // tn),
        in_specs=[row(oa), row(ob), row(oc), const(ga), const(gb),
                  pl.BlockSpec((w.shape[0], tn), lambda i, j: (0, j)),
                  pl.BlockSpec((tm, tn), lambda i, j: (i, j))],
        out_specs=pl.BlockSpec((tm, tn), lambda i, j: (i, j)),
        out_shape=jax.ShapeDtypeStruct((m, d), F32),
        scratch_shapes=[pltpu.VMEM((tm, w.shape[0]), BF16)],
        compiler_params=_params("parallel", "arbitrary"),
        name="outproj",
    )(oa, ob, oc, ga, gb, w, x2d)


def _ffn_kernel(x_ref, g_ref, wg_ref, wv_ref, wd_ref, cw_ref, st_ref, o_ref, buf_ref, h_ref, carry_ref,
                *, tiles_per_seq, per_seq_rows):
    i = pl.program_id(0)
    f = pl.program_id(1)
    tm, tf = h_ref.shape[0], wg_ref.shape[1]

    @pl.when(f == 0)
    def _():
        h_ref[...] = _rms(x_ref[...], g_ref[...]).astype(BF16)

    h = h_ref[...]
    gate = jnp.dot(h, wg_ref[...], preferred_element_type=F32)
    val = jnp.dot(h, wv_ref[...], preferred_element_type=F32)
    cw = cw_ref[...]
    prev1 = pltpu.roll(gate, 1, axis=0)
    prev2 = pltpu.roll(gate, 2, axis=0)
    row = lax.broadcasted_iota(jnp.int32, (tm, tf), 0)
    if per_seq_rows:
        nseq = tm // per_seq_rows
        st = st_ref[...]
        expand = lambda a: jnp.broadcast_to(a, (nseq, per_seq_rows, tf)).reshape(tm, tf)
        s0, s1 = expand(st[:, 0:1, :]), expand(st[:, 1:2, :])
        t = row & (per_seq_rows - 1)
        buf_ref[...] = gate.reshape(nseq, per_seq_rows, tf)[:, per_seq_rows - 2:, :]
    else:
        @pl.when((i % tiles_per_seq) == 0)
        def _():
            carry_ref[f] = st_ref[...]

        prev = carry_ref[f]
        s0, s1 = prev[0:1, :], prev[1:2, :]
        t = row
        tail = gate[tm - 2:tm, :]
        carry_ref[f] = tail
        buf_ref[...] = tail
    x1 = jnp.where(t >= 1, prev1, s1)
    x2 = jnp.where(t >= 2, prev2, jnp.where(t == 1, s1, s0))
    conv = cw[2:3, :] * gate + cw[1:2, :] * x1 + cw[0:1, :] * x2
    act = (conv * _sigmoid(conv) * val).astype(BF16)
    down = jnp.dot(act, wd_ref[...], preferred_element_type=F32)

    @pl.when(f == 0)
    def _():
        o_ref[...] = x_ref[...] + down

    @pl.when(f > 0)
    def _():
        o_ref[...] += down


def _ffn(x2d, gain, wg, wv, wd, conv_w, state, seq):
    m, d = x2d.shape
    d_ff = wg.shape[1]
    nseq = m // seq
    assert conv_w.shape[0] == 3
    tf = _tile(d_ff, 512)
    nf = d_ff // tf
    if seq <= SUBLANES:
        assert seq == SUBLANES
        tm = _tile(m, 512, SUBLANES)
        per_seq_rows, tiles_per_seq = seq, 1
        st_spec = buf_spec = pl.BlockSpec((tm // seq, 2, tf), lambda i, f: (i, 0, f))
        n_buf = nseq
    else:
        tm = _tile(seq, 512, SUBLANES)
        per_seq_rows, tiles_per_seq = 0, seq // tm
        st_spec = pl.BlockSpec((None, 2, tf), lambda i, f: (i // tiles_per_seq, 0, f))
        buf_spec = pl.BlockSpec((None, 2, tf), lambda i, f: (i, 0, f))
        n_buf = m // tm
    y, buf = pl.pallas_call(
        functools.partial(_ffn_kernel, tiles_per_seq=tiles_per_seq, per_seq_rows=per_seq_rows),
        grid=(m // tm, nf),
        in_specs=[
            pl.BlockSpec((tm, d), lambda i, f: (i, 0)),
            pl.BlockSpec((1, d), lambda i, f: (0, 0)),
            pl.BlockSpec((d, tf), lambda i, f: (0, f)),
            pl.BlockSpec((d, tf), lambda i, f: (0, f)),
            pl.BlockSpec((tf, d), lambda i, f: (f, 0)),
            pl.BlockSpec((3, tf), lambda i, f: (0, f)),
            st_spec,
        ],
        out_specs=[pl.BlockSpec((tm, d), lambda i, f: (i, 0)), buf_spec],
        out_shape=[jax.ShapeDtypeStruct((m, d), F32), jax.ShapeDtypeStruct((n_buf, 2, d_ff), F32)],
        scratch_shapes=[pltpu.VMEM((tm, d), BF16), pltpu.VMEM((nf, 2, tf), F32)],
        compiler_params=_params("arbitrary", "arbitrary"),
        name="convffn",
    )(x2d, gain, wg, wv, wd, conv_w, state)
    if not per_seq_rows:
        buf = buf.reshape(nseq, tiles_per_seq, 2, d_ff)[:, tiles_per_seq - 1]
    return y, buf


def _layer_weights(l, w_in, sb_q_norm, sb_k_norm, ssm, dn_a_log, dn_dt_bias, ssm_w_glu, w_out, ffn_w_in, ffn_w_out,
                   dims):
    sbw, ssmw, dnw, nh = dims["sbw"], dims["ssmw"], dims["dnw"], dims["dn_heads"]
    heads, hd = dims["sb_heads"], dims["sb_hd"]
    d = w_in.shape[1]
    o_u = 3 * sbw
    o_qkv = o_u + ssmw
    o_ba = o_qkv + 3 * dnw
    o_gate = o_ba + 2 * nh
    n_rest = -(-(ssmw + 4 * dnw + LANES) // sbw) * sbw
    pad = n_rest - (ssmw + 4 * dnw + 2 * nh)
    w = w_in[l]
    w_perm = jnp.concatenate([
        w[:, :o_u], w[:, o_u:o_qkv], w[:, o_gate:o_gate + dnw], w[:, o_qkv:o_ba], w[:, o_ba:o_gate],
        jnp.zeros((d, pad), w.dtype)], axis=1).astype(BF16)
    head_of = jnp.arange(sbw) // hd
    gs = (head_of[:, None] == jnp.arange(LANES)[None, :]).astype(BF16)
    qg = jnp.tile(sb_q_norm[l], heads)[None, :]
    kg = jnp.tile(sb_k_norm[l], heads)[None, :]
    a_re, a_im, bb_re, bb_im = ssm["disc"]
    g, n, ch = dims["ssm_groups"], dims["ssm_state"], dims["ssm_ch"]
    ns = g * n
    grp_of_state = jnp.arange(ns) // n
    grp_of_chan = jnp.arange(ssmw) // ch
    blockmask = grp_of_chan[:, None] == grp_of_state[None, :]
    expand_b = lambda bb: jnp.where(blockmask, jnp.tile(bb.T, (g, 1)), 0.0)
    bmat = jnp.concatenate([expand_b(bb_re), expand_b(bb_im)], axis=1).astype(BF16)
    arow = jnp.concatenate([a_re, a_im], axis=0).reshape(1, 2 * ns)
    expand_c = lambda c: jnp.where(blockmask.T, jnp.tile(c.transpose(0, 2, 1).reshape(ns, ch), (1, g)), 0.0)
    cmat = jnp.concatenate([expand_c(ssm["c_re"][l]), -expand_c(ssm["c_im"][l])], axis=0).astype(BF16)
    prm = jnp.zeros((SUBLANES, LANES), F32)
    prm = prm.at[0, nh:2 * nh].set(dn_a_log[l]).at[1, nh:2 * nh].set(dn_dt_bias[l])
    d_ff = ffn_w_in.shape[2] // 2
    return dict(
        w_perm=w_perm, gs=gs, e=gs.T, qg=qg, kg=kg, bmat=bmat, arow=arow, cmat=cmat, prm=prm,
        wglu=ssm_w_glu[l].astype(BF16), w_out=w_out[l].astype(BF16),
        wg=ffn_w_in[l][:, :d_ff].astype(BF16), wv=ffn_w_in[l][:, d_ff:].astype(BF16),
        wd=ffn_w_out[l].astype(BF16), n_rest=n_rest)


def kernel(x_prompt, x_sample, cache_k, cache_v, page_table, state_ssm_re, state_ssm_im, state_delta, state_delta_conv, state_ffn_conv, norm_mix, w_in, sb_q_norm, sb_k_norm, sb_logit_bias, ssm_lambda_re, ssm_lambda_im, ssm_log_step, ssm_b_re, ssm_b_im, ssm_c_re, ssm_c_im, ssm_d, ssm_w_glu, ssm_b_glu, dn_conv_w, dn_a_log, dn_dt_bias, dn_norm, norm_out_sb, norm_out_ssm, w_out, norm_ffn, ffn_w_in, ffn_conv_w, ffn_w_out):
    depth = w_in.shape[0]
    bp, tp, d = x_prompt.shape
    bd, td, _ = x_sample.shape
    sb_heads, sb_hd = cache_k.shape[-2], cache_k.shape[-1]
    g, n = ssm_lambda_re.shape[1:]
    ch = ssm_b_re.shape[-1]
    dn_heads, dn_hd = dn_a_log.shape[1], dn_norm.shape[1]
    dims = dict(sbw=sb_heads * sb_hd, sb_heads=sb_heads, sb_hd=sb_hd, ssmw=g * ch, ssm_groups=g, ssm_state=n,
                ssm_ch=ch, dnw=dn_heads * dn_hd, dn_heads=dn_heads)
    sbw, ssmw, dnw, ns = dims["sbw"], dims["ssmw"], dims["dnw"], g * n
    assert ssmw % dnw == 0 and dnw % LANES == 0 and sbw % LANES == 0 and LANES % sb_hd == 0
    d_ff = ffn_conv_w.shape[2]
    col0 = ssmw // dnw
    q_scale = sb_hd ** -0.5

    groups = {
        "p": dict(x=x_prompt.reshape(bp * tp, d), b=bp, t=tp, kbuf=None, vbuf=None, outs=[]),
        "s": dict(x=x_sample.reshape(bd * td, d), b=bd, t=td, kbuf=None, vbuf=None, outs=[]),
    }
    for l in range(depth):
        disc = _s5_discretize(ssm_lambda_re[l], ssm_lambda_im[l], ssm_log_step[l], ssm_b_re[l], ssm_b_im[l])
        lw = _layer_weights(l, w_in, sb_q_norm, sb_k_norm, dict(disc=disc, c_re=ssm_c_re, c_im=ssm_c_im),
                            dn_a_log, dn_dt_bias, ssm_w_glu, w_out, ffn_w_in, ffn_w_out, dims)
        for name, grp in groups.items():
            b, t, x = grp["b"], grp["t"], grp["x"]
            q, kbuf, vbuf, rest = _inproj(x, norm_mix[l][None, :], lw["w_perm"], lw["gs"], lw["e"], lw["qg"],
                                          lw["kg"], grp["kbuf"], grp["vbuf"], l, depth, sb_hd, q_scale)
            grp["kbuf"], grp["vbuf"] = kbuf, vbuf
            rest3 = rest.reshape(b, t, lw["n_rest"])
            if name == "p":
                o_a = _sb_prompt(q, kbuf, vbuf, sb_logit_bias[l], l, b, t, sb_hd)
                h0 = jnp.zeros((b, 2 * ns), F32)
                dn_s0 = jnp.zeros((b, dn_heads, dn_hd, dn_hd), F32)
                dn_c0 = jnp.zeros((b, dn_conv_w.shape[1] - 1, 3 * dnw), F32)
                ffn_c0 = jnp.zeros((b, ffn_conv_w.shape[1] - 1, d_ff), F32)
            else:
                o_a = _sb_sample(q, kbuf, vbuf, cache_k, cache_v, page_table, sb_logit_bias[l], l, b, t, sb_hd)
                h0 = jnp.concatenate([state_ssm_re[l].reshape(b, ns), state_ssm_im[l].reshape(b, ns)], axis=1)
                dn_s0, dn_c0, ffn_c0 = state_delta[l], state_delta_conv[l], state_ffn_conv[l]
            u_tm = rest3[:, :, :ssmw].transpose(1, 0, 2)
            y_tm, h_fin = _s5(u_tm, lw["bmat"], lw["arow"], h0, lw["cmat"], ssm_d[l][None, :], lw["wglu"],
                              ssm_b_glu[l][None, :])
            o_b = y_tm.transpose(1, 0, 2).reshape(b * t, ssmw)
            o_c, dn_s, dn_c = _deltanet(rest3, col0, dn_conv_w[l], lw["prm"], dn_norm[l][None, :], dn_c0, dn_s0,
                                        dn_heads, dn_hd)
            x = _outproj(o_a, o_b, o_c.reshape(b * t, dnw), norm_out_sb[l][None, :], norm_out_ssm[l][None, :],
                         lw["w_out"], x)
            x, ffn_c = _ffn(x, norm_ffn[l][None, :], lw["wg"], lw["wv"], lw["wd"], ffn_conv_w[l], ffn_c0, t)
            grp["x"] = x
            grp["outs"].append((h_fin[:, :ns].reshape(b, g, n), h_fin[:, ns:].reshape(b, g, n), dn_s, dn_c, ffn_c))

    res = {}
    for name, grp in groups.items():
        b, t = grp["b"], grp["t"]
        sre, sim, dn_s, dn_c, ffn_c = [jnp.stack(z) for z in zip(*grp["outs"])]
        res[name] = dict(
            y=grp["x"].reshape(b, t, d),
            k=grp["kbuf"].reshape(depth, b, t, sb_heads, sb_hd), v=grp["vbuf"].reshape(depth, b, t, sb_heads, sb_hd),
            sre=sre, sim=sim, dn_s=dn_s, dn_c=dn_c, ffn_c=ffn_c)
    p, s = res["p"], res["s"]
    return (p["y"], s["y"], p["k"], p["v"], s["k"], s["v"], p["sre"], p["sim"], s["sre"], s["sim"],
            p["dn_s"], s["dn_s"], p["dn_c"], s["dn_c"], p["ffn_c"], s["ffn_c"])
```

```python
import functools
import math

import jax
import jax.numpy as jnp
from jax import lax
from jax.experimental import pallas as pl
from jax.experimental.pallas import tpu as pltpu

F32 = jnp.float32
BF16 = jnp.bfloat16
HIGHEST = lax.Precision.HIGHEST
NORM_EPS = 1e-6
LANES = 128
SUBLANES = 8
BF16_ROWS = 16
VMEM_LIMIT_BYTES = 56 * 1024 * 1024
DN_CHUNK = 64
TRI_BASE = 16

_NT = (((1,), (1,)), ((), ()))
_TN = (((0,), (0,)), ((), ()))


def _tile(dim, pref, mult=LANES):
    if dim <= pref:
        return dim
    t = (pref // mult) * mult
    while t >= mult:
        if dim % t == 0:
            return t
        t -= mult
    raise ValueError(f"no tile for {dim} (pref {pref}, mult {mult})")


def _params(*sem):
    return pltpu.CompilerParams(dimension_semantics=sem, vmem_limit_bytes=VMEM_LIMIT_BYTES)


def _mm(a, b, dims=None):
    ok = all(d % BF16_ROWS == 0 for d in a.shape + b.shape)
    if ok:
        a, b = a.astype(BF16), b.astype(BF16)
    else:
        a, b = a.astype(F32), b.astype(F32)
    if dims is None:
        return jnp.dot(a, b, preferred_element_type=F32)
    return lax.dot_general(a, b, dims, preferred_element_type=F32)


def _mm_f32(a, b, dims=None):
    if dims is None:
        return jnp.dot(a, b, preferred_element_type=F32, precision=HIGHEST)
    return lax.dot_general(a, b, dims, preferred_element_type=F32, precision=HIGHEST)


def _softplus(z):
    return jnp.maximum(z, 0.0) + jnp.log1p(jnp.exp(-jnp.abs(z)))


def _sigmoid(z):
    return 1.0 / (1.0 + jnp.exp(-z))


def _rms(x, gain, eps=NORM_EPS):
    return x * lax.rsqrt(jnp.mean(x * x, axis=-1, keepdims=True) + eps) * gain


def _inproj_kernel(x_ref, g_ref, w_ref, gs_ref, e_ref, qg_ref, kg_ref, *refs, head_dim, q_scale, aliased):
    q_ref, k_ref, v_ref, r_ref, xn_ref = refs[2:] if aliased else refs
    j = pl.program_id(1)

    @pl.when(j == 0)
    def _():
        xn_ref[...] = _rms(x_ref[...], g_ref[...]).astype(BF16)

    y = jnp.dot(xn_ref[...], w_ref[...], preferred_element_type=F32)

    def head_norm(gain):
        ms = jnp.dot((y * y).astype(BF16), gs_ref[...], preferred_element_type=F32) * (1.0 / head_dim)
        inv = lax.rsqrt(ms + NORM_EPS)
        hi = inv.astype(BF16)
        lo = (inv - hi.astype(F32)).astype(BF16)
        sc = (jnp.dot(hi, e_ref[...], preferred_element_type=F32)
              + jnp.dot(lo, e_ref[...], preferred_element_type=F32))
        return y * sc * gain

    @pl.when(j == 0)
    def _():
        q_ref[...] = (head_norm(qg_ref[...]) * q_scale).astype(BF16)

    @pl.when(j == 1)
    def _():
        k_ref[...] = head_norm(kg_ref[...])

    @pl.when(j == 2)
    def _():
        v_ref[...] = y

    @pl.when(j >= 3)
    def _():
        r_ref[...] = y


def _inproj(x2d, gain, w_perm, gs, e, qg, kg, kbuf, vbuf, layer, depth, head_dim, q_scale):
    m, d = x2d.shape
    tn = gs.shape[0]
    n_rest = w_perm.shape[1] - 3 * tn
    tm = _tile(m, 512, SUBLANES)
    aliased = kbuf is not None
    const = lambda i, j: (0, 0)
    in_specs = [
        pl.BlockSpec((tm, d), lambda i, j: (i, 0)),
        pl.BlockSpec((1, d), const),
        pl.BlockSpec((d, tn), lambda i, j: (0, j)),
        pl.BlockSpec((tn, LANES), const),
        pl.BlockSpec((LANES, tn), const),
        pl.BlockSpec((1, tn), const),
        pl.BlockSpec((1, tn), const),
    ]
    args = [x2d, gain, w_perm, gs, e, qg, kg]
    aliases = {}
    if aliased:
        in_specs += [pl.BlockSpec(memory_space=pl.ANY), pl.BlockSpec(memory_space=pl.ANY)]
        args += [kbuf, vbuf]
        aliases = {7: 1, 8: 2}
    out_specs = [
        pl.BlockSpec((tm, tn), lambda i, j: (i, 0)),
        pl.BlockSpec((None, tm, tn), lambda i, j: (layer, i, 0)),
        pl.BlockSpec((None, tm, tn), lambda i, j: (layer, i, 0)),
        pl.BlockSpec((tm, tn), lambda i, j: (i, jnp.maximum(j - 3, 0))),
    ]
    out_shape = [
        jax.ShapeDtypeStruct((m, tn), BF16),
        jax.ShapeDtypeStruct((depth, m, tn), F32),
        jax.ShapeDtypeStruct((depth, m, tn), F32),
        jax.ShapeDtypeStruct((m, n_rest), F32),
    ]
    return pl.pallas_call(
        functools.partial(_inproj_kernel, head_dim=head_dim, q_scale=q_scale, aliased=aliased),
        grid=(m // tm, 3 + n_rest // tn),
        in_specs=in_specs, out_specs=out_specs, out_shape=out_shape,
        scratch_shapes=[pltpu.VMEM((tm, d), BF16)],
        input_output_aliases=aliases,
        compiler_params=_params("parallel", "arbitrary"),
        name="inproj",
    )(*args)


def _softplus_unit(z):
    return jnp.maximum(z, 0.0) + jnp.log(1.0 + jnp.exp(-jnp.abs(z)))


def _sb_prompt_kernel(bias_ref, q_ref, k_ref, v_ref, o_ref, *, blk, nsub, head_dim, heads_per_step):
    hp = pl.program_id(1)
    qi = pl.program_id(2)
    row = lax.broadcasted_iota(jnp.int32, (blk, blk), 0)
    col = lax.broadcasted_iota(jnp.int32, (blk, blk), 1)
    later = (row > col).astype(BF16)
    visible = col < row
    q = q_ref[...]
    heads = range(heads_per_step)
    lanes = [slice(hh * head_dim, (hh + 1) * head_dim) for hh in heads]
    qs = [[q[j * blk:(j + 1) * blk, lanes[hh]] for j in range(nsub)] for hh in heads]
    biases = [bias_ref[hp * heads_per_step + hh] for hh in heads]

    def load(ki):
        r0 = pl.multiple_of(ki * blk, blk)
        kb = k_ref[pl.ds(r0, blk), :].astype(BF16)
        vb = v_ref[pl.ds(r0, blk), :].astype(BF16)
        return [(kb[:, lanes[hh]], vb[:, lanes[hh]]) for hh in heads]

    def block(qh, kv, bias, oc, masked):
        o_acc, carry = oc
        z = lax.dot_general(qh, kv[0], _NT, preferred_element_type=F32) + bias
        sp = _softplus_unit(z)
        if masked:
            sp = jnp.where(visible, sp, 0.0)
        after = jnp.dot(sp.astype(BF16), later, preferred_element_type=F32)
        w = jnp.exp(z - sp - after - carry)
        if masked:
            w = jnp.where(visible, w, 0.0)
        o_acc = o_acc + jnp.dot(w.astype(BF16), kv[1], preferred_element_type=F32)
        return o_acc, carry + after[:, 0:1] + sp[:, 0:1]

    zero = (jnp.zeros((blk, head_dim), F32), jnp.zeros((blk, 1), F32))
    state = [[zero for _ in range(nsub)] for _ in heads]
    for d in reversed(range(nsub)):
        kv = load(qi * nsub + d)
        for hh in heads:
            for j in range(d, nsub):
                state[hh][j] = block(qs[hh][j], kv[hh], biases[hh], state[hh][j], j == d)

    def older(it, flat):
        kv = load(qi * nsub - 1 - it)
        return tuple(block(qs[hh][j], kv[hh], biases[hh], flat[hh * nsub + j], False)
                     for hh in heads for j in range(nsub))

    flat = lax.fori_loop(0, qi * nsub, older, tuple(state[hh][j] for hh in heads for j in range(nsub)))
    o_ref[...] = jnp.concatenate(
        [jnp.concatenate([flat[hh * nsub + j][0] for hh in heads], axis=-1) for j in range(nsub)], axis=0)


def _sb_prompt(q, kbuf, vbuf, bias, layer, bsz, seq, head_dim):
    width = q.shape[-1]
    heads_per_step = LANES // head_dim
    blk = _tile(seq, 256)
    nsub = 2 if seq % (2 * blk) == 0 else 1
    rows = nsub * blk
    q3 = q.reshape(bsz, seq, width)
    k4 = kbuf.reshape(kbuf.shape[0], bsz, seq, width)
    v4 = vbuf.reshape(vbuf.shape[0], bsz, seq, width)
    return pl.pallas_call(
        functools.partial(_sb_prompt_kernel, blk=blk, nsub=nsub, head_dim=head_dim, heads_per_step=heads_per_step),
        grid=(bsz, width // LANES, seq // rows),
        in_specs=[
            pl.BlockSpec(memory_space=pltpu.SMEM),
            pl.BlockSpec((None, rows, LANES), lambda b, h, i: (b, i, h)),
            pl.BlockSpec((None, None, seq, LANES), lambda b, h, i: (layer, b, 0, h)),
            pl.BlockSpec((None, None, seq, LANES), lambda b, h, i: (layer, b, 0, h)),
        ],
        out_specs=pl.BlockSpec((None, rows, LANES), lambda b, h, i: (b, i, h)),
        out_shape=jax.ShapeDtypeStruct((bsz, seq, width), F32),
        compiler_params=_params("parallel", "parallel", "arbitrary"),
        name="sb_prompt",
    )(bias, q3, k4, v4).reshape(bsz * seq, width)


SB_LANE_BLOCK = 256


def _sb_sample_kernel(pt_ref, q_ref, bcol_ref, kn_ref, vn_ref, *refs, pages_per_step, tq, heads, head_dim):
    del pt_ref
    kp = refs[:pages_per_step]
    vp = refs[pages_per_step:2 * pages_per_step]
    o_ref, acc_ref, carry_ref = refs[2 * pages_per_step:]
    c = pl.program_id(1)
    nr = heads * tq
    n_ht = heads // SUBLANES
    rt = SUBLANES * tq
    ltq = tq.bit_length() - 1
    page = kp[0].shape[0]
    nl = page * SUBLANES
    qf = q_ref[...].astype(F32)
    width = qf.shape[-1]
    bcol = bcol_ref[...]
    head_cols = lambda h: slice(h * head_dim, (h + 1) * head_dim)

    @pl.when(c == 0)
    def _():
        r = lax.broadcasted_iota(jnp.int32, (nr, width), 0)
        col = lax.broadcasted_iota(jnp.int32, (nr, width), 1)
        own_head = lax.shift_right_logical(r, ltq) == lax.shift_right_logical(col, head_dim.bit_length() - 1)
        qbd = jnp.where(own_head, jnp.concatenate([qf] * heads, axis=0), 0.0)
        z = lax.dot_general(qbd, kn_ref[...], _NT, preferred_element_type=F32) + bcol
        key = lax.broadcasted_iota(jnp.int32, (nr, tq), 1)
        qry = lax.broadcasted_iota(jnp.int32, (nr, tq), 0) & (tq - 1)
        vis = key < qry
        sp = jnp.where(vis, _softplus_unit(z), 0.0)
        later = (lax.broadcasted_iota(jnp.int32, (tq, tq), 0)
                 > lax.broadcasted_iota(jnp.int32, (tq, tq), 1)).astype(F32)
        w = jnp.where(vis, jnp.exp(z - sp - _mm_f32(sp, later)), 0.0)
        full = jnp.dot(w, vn_ref[...], preferred_element_type=F32)
        acc_ref[...] = jnp.concatenate([full[h * tq:(h + 1) * tq, head_cols(h)] for h in range(heads)], axis=0)
        carry_ref[...] = jnp.sum(sp, axis=-1, keepdims=True)

    qst = jnp.concatenate([qf[:, head_cols(h)] for h in range(heads)], axis=0).astype(BF16)
    r2 = lax.broadcasted_iota(jnp.int32, (nr, nl), 0)
    l2 = lax.broadcasted_iota(jnp.int32, (nr, nl), 1)
    match = (lax.shift_right_logical(r2, ltq) & (SUBLANES - 1)) == (l2 & (SUBLANES - 1))
    nb = min(SB_LANE_BLOCK, nl)
    ls = SUBLANES.bit_length() - 1
    later_key = (lax.shift_right_logical(lax.broadcasted_iota(jnp.int32, (nb, nb), 0), ls)
                 > lax.shift_right_logical(lax.broadcasted_iota(jnp.int32, (nb, nb), 1), ls)).astype(BF16)
    tile_rows = lambda ref, t: ref[:, t * SUBLANES:(t + 1) * SUBLANES, :].reshape(nl, head_dim).astype(BF16)
    for p in range(pages_per_step):
        z = jnp.concatenate(
            [lax.dot_general(qst[t * rt:(t + 1) * rt], tile_rows(kp[p], t), _NT, preferred_element_type=F32)
             for t in range(n_ht)], axis=0) + bcol
        sp = jnp.where(match, _softplus_unit(z), 0.0)
        run = carry_ref[...]
        after = [None] * (nl // nb)
        for b in reversed(range(nl // nb)):
            blk = sp[:, b * nb:(b + 1) * nb]
            after[b] = jnp.dot(blk.astype(BF16), later_key, preferred_element_type=F32) + run
            run = run + jnp.sum(blk, axis=-1, keepdims=True)
        w = jnp.where(match, jnp.exp(z - sp - jnp.concatenate(after, axis=1)), 0.0).astype(BF16)
        for t in range(n_ht):
            acc_ref[t * rt:(t + 1) * rt, :] += jnp.dot(w[t * rt:(t + 1) * rt], tile_rows(vp[p], t),
                                                       preferred_element_type=F32)
        carry_ref[...] = run

    @pl.when(c == pl.num_programs(1) - 1)
    def _():
        acc = acc_ref[...]
        o_ref[...] = jnp.concatenate([acc[h * tq:(h + 1) * tq, :] for h in range(heads)], axis=1)


def _sb_sample(q, kbuf, vbuf, cache_k, cache_v, page_table, bias, layer, bsz, tq, head_dim):
    width = q.shape[-1]
    heads = width // head_dim
    assert heads % SUBLANES == 0 and tq & (tq - 1) == 0 and head_dim & (head_dim - 1) == 0
    page = cache_k.shape[2]
    npages = page_table.shape[1]
    pps = _tile(npages, 4, 1)
    bcol = jnp.repeat(bias.astype(F32), tq)[:, None]

    def page_map(p):
        return lambda b, c, pt: (layer, pt[b * npages + npages - 1 - (c * pps + p)], 0, 0, 0)

    page_specs = [pl.BlockSpec((None, None, page, heads, head_dim), page_map(p)) for p in range(pps)]
    grid_spec = pltpu.PrefetchScalarGridSpec(
        num_scalar_prefetch=1,
        grid=(bsz, npages // pps),
        in_specs=[
            pl.BlockSpec((None, tq, width), lambda b, c, pt: (b, 0, 0)),
            pl.BlockSpec((heads * tq, 1), lambda b, c, pt: (0, 0)),
            pl.BlockSpec((None, None, tq, width), lambda b, c, pt: (layer, b, 0, 0)),
            pl.BlockSpec((None, None, tq, width), lambda b, c, pt: (layer, b, 0, 0)),
        ] + page_specs + page_specs,
        out_specs=pl.BlockSpec((None, tq, width), lambda b, c, pt: (b, 0, 0)),
        scratch_shapes=[pltpu.VMEM((heads * tq, head_dim), F32), pltpu.VMEM((heads * tq, 1), F32)],
    )
    k4 = kbuf.reshape(kbuf.shape[0], bsz, tq, width)
    v4 = vbuf.reshape(vbuf.shape[0], bsz, tq, width)
    out = pl.pallas_call(
        functools.partial(_sb_sample_kernel, pages_per_step=pps, tq=tq, heads=heads, head_dim=head_dim),
        grid_spec=grid_spec,
        out_shape=jax.ShapeDtypeStruct((bsz, tq, width), F32),
        compiler_params=_params("parallel", "arbitrary"),
        name="sb_sample",
    )(page_table.reshape(-1), q.reshape(bsz, tq, width), bcol, k4, v4,
      *([cache_k] * pps), *([cache_v] * pps))
    return out.reshape(bsz * tq, width)


def _s5_discretize_kernel(lre_ref, lim_ref, ls_ref, bre_ref, bim_ref, are_ref, aim_ref, bbre_ref, bbim_ref):
    lam_re, lam_im = lre_ref[...], lim_ref[...]
    dt = jnp.exp(ls_ref[...])
    mag = jnp.exp(lam_re * dt)
    ab_re = mag * jnp.cos(lam_im * dt)
    ab_im = mag * jnp.sin(lam_im * dt)
    den = lam_re * lam_re + lam_im * lam_im
    xr = ab_re - 1.0
    f_re = (xr * lam_re + ab_im * lam_im) / den
    f_im = (ab_im * lam_re - xr * lam_im) / den
    are_ref[...] = ab_re
    aim_ref[...] = ab_im
    bbre_ref[...] = f_re * bre_ref[...] - f_im * bim_ref[...]
    bbim_ref[...] = f_re * bim_ref[...] + f_im * bre_ref[...]


def _s5_discretize(lam_re, lam_im, log_step, b_re, b_im):
    g, n = lam_re.shape
    ch = b_re.shape[-1]
    col = lambda a: a.reshape(g * n, 1)
    ls = jnp.broadcast_to(log_step[:, None], (g, n))
    return pl.pallas_call(
        _s5_discretize_kernel,
        out_shape=[jax.ShapeDtypeStruct((g * n, 1), F32)] * 2 + [jax.ShapeDtypeStruct((g * n, ch), F32)] * 2,
        name="s5_discretize",
    )(col(lam_re), col(lam_im), col(ls), b_re.reshape(g * n, ch), b_im.reshape(g * n, ch))


def _s5_kernel(u_ref, bmat_ref, arow_ref, h0_ref, cmat_ref, d_ref, wglu_ref, bglu_ref,
               y_ref, hfin_ref, hbuf_ref, hc_ref, *, tt, nb, ns, cb):
    ti = pl.program_id(1)
    width = u_ref.shape[-1]

    @pl.when(ti == 0)
    def _():
        hc_ref[...] = h0_ref[...]

    u = u_ref[...].reshape(tt * nb, width)
    hbuf_ref[...] = jnp.dot(u.astype(BF16), bmat_ref[...], preferred_element_type=F32)

    for c in range(ns // cb):
        re = slice(c * cb, (c + 1) * cb)
        im = slice(ns + c * cb, ns + (c + 1) * cb)
        a_re = jnp.broadcast_to(arow_ref[:, re], (SUBLANES, cb))
        a_im = jnp.broadcast_to(arow_ref[:, im], (SUBLANES, cb))

        def batch_rows(bb, _):
            r0 = pl.multiple_of(bb * SUBLANES, SUBLANES)

            def step(t, h):
                h_re, h_im = h
                r = pl.multiple_of(t * nb + r0, SUBLANES)
                n_re = a_re * h_re - a_im * h_im + hbuf_ref[pl.ds(r, SUBLANES), re]
                n_im = a_re * h_im + a_im * h_re + hbuf_ref[pl.ds(r, SUBLANES), im]
                hbuf_ref[pl.ds(r, SUBLANES), re] = n_re
                hbuf_ref[pl.ds(r, SUBLANES), im] = n_im
                return n_re, n_im

            h = (hc_ref[pl.ds(r0, SUBLANES), re], hc_ref[pl.ds(r0, SUBLANES), im])
            h = lax.fori_loop(0, tt, step, h, unroll=min(tt, 8))
            hc_ref[pl.ds(r0, SUBLANES), re] = h[0]
            hc_ref[pl.ds(r0, SUBLANES), im] = h[1]
            return 0

        lax.fori_loop(0, nb // SUBLANES, batch_rows, 0)

    y = jnp.dot(hbuf_ref[...].astype(BF16), cmat_ref[...], preferred_element_type=F32) + d_ref[...] * u
    y = 0.5 * y * (1.0 + jnp.tanh(math.sqrt(2.0 / math.pi) * (y + 0.044715 * (y * y * y))))
    z = jnp.dot(y.astype(BF16), wglu_ref[...], preferred_element_type=F32) + bglu_ref[...]
    y_ref[...] = (y * _sigmoid(z)).reshape(tt, nb, width)

    @pl.when(ti == pl.num_programs(1) - 1)
    def _():
        hfin_ref[...] = hc_ref[...]


def _s5(u_tm, bmat, arow, h0, cmat, d_row, wglu, bglu):
    t, b, width = u_tm.shape
    ns2 = bmat.shape[1]
    nb = _tile(b, 64, SUBLANES)
    tt = _tile(t, max(SUBLANES, 512 // nb), 1)
    const = lambda i, j: (0, 0)
    return pl.pallas_call(
        functools.partial(_s5_kernel, tt=tt, nb=nb, ns=ns2 // 2, cb=_tile(ns2 // 2, 512)),
        grid=(b // nb, t // tt),
        in_specs=[
            pl.BlockSpec((tt, nb, width), lambda i, j: (j, i, 0)),
            pl.BlockSpec((width, ns2), const),
            pl.BlockSpec((1, ns2), const),
            pl.BlockSpec((nb, ns2), lambda i, j: (i, 0)),
            pl.BlockSpec((ns2, width), const),
            pl.BlockSpec((1, width), const),
            pl.BlockSpec((width, width), const),
            pl.BlockSpec((1, width), const),
        ],
        out_specs=[
            pl.BlockSpec((tt, nb, width), lambda i, j: (j, i, 0)),
            pl.BlockSpec((nb, ns2), lambda i, j: (i, 0)),
        ],
        out_shape=[jax.ShapeDtypeStruct((t, b, width), F32), jax.ShapeDtypeStruct((b, ns2), F32)],
        scratch_shapes=[pltpu.VMEM((tt * nb, ns2), F32), pltpu.VMEM((nb, ns2), F32)],
        compiler_params=_params("parallel", "arbitrary"),
        name="s5",
    )(u_tm, bmat, arow, h0, cmat, d_row, wglu, bglu)


def _split(a):
    hi = a.astype(BF16)
    return hi, (a - hi.astype(F32)).astype(BF16)


def _mm3(a, b):
    d = lambda x, y: jnp.dot(x, y, preferred_element_type=F32)
    return d(a[0], b[0]) + d(a[0], b[1]) + d(a[1], b[0])


def _block_tri_inverse(a, n, c):
    row = lax.broadcasted_iota(jnp.int32, (n, n), 0)
    col = lax.broadcasted_iota(jnp.int32, (n, n), 1)
    eye = (row == col).astype(F32)

    def same_block(b):
        s = b.bit_length() - 1
        return lax.shift_right_logical(row, s) == lax.shift_right_logical(col, s)

    b0 = min(c, TRI_BASE)
    neg = -jnp.where(same_block(b0), a, 0.0)
    t = eye + neg
    p = _split(neg)
    k = 1
    while 2 * k < b0:
        p = _split(_mm3(p, p))
        t = t + _mm3(_split(t), p)
        k *= 2
    b = b0
    while b < c:
        off = jnp.where(jnp.logical_and(same_block(2 * b), jnp.logical_not(same_block(b))), a, 0.0)
        ts = _split(t)
        t = t - _mm3(_split(_mm3(ts, _split(off))), ts)
        b *= 2
    return t


def _dn_kernel(q_ref, k_ref, v_ref, ba_ref, gate_ref, cw_ref, prm_ref, nrm_ref, cbuf_ref, s0_ref,
               o_ref, sout_ref, cout_ref,
               xpad_ref, qs_ref, ks_ref, vs_ref, gb_ref, s_ref, *, bs, tt, chunk, nh, hd, taps):
    ti = pl.program_id(1)
    width = nh * hd
    halo = SUBLANES
    first = halo - (taps - 1)

    @pl.when(ti == 0)
    def _():
        s_ref[...] = s0_ref[...]
        xpad_ref[:, first:halo, :] = cbuf_ref[...]

    xpad_ref[:, halo:halo + tt, 0:width] = q_ref[...]
    xpad_ref[:, halo:halo + tt, width:2 * width] = k_ref[...]
    xpad_ref[:, halo:halo + tt, 2 * width:3 * width] = v_ref[...]
    cw = cw_ref[...]
    y = cw[0:1, :] * xpad_ref[:, first:first + tt, :]
    for i in range(1, taps):
        y = y + cw[i:i + 1, :] * xpad_ref[:, first + i:first + i + tt, :]
    cout_ref[...] = xpad_ref[:, halo + tt - (taps - 1):halo + tt, :]
    xpad_ref[:, 0:halo, :] = xpad_ref[:, tt:tt + halo, :]
    act = y * _sigmoid(y)
    for h in range(nh):
        sl = slice(h * hd, (h + 1) * hd)
        qh = act[:, :, h * hd:(h + 1) * hd]
        kh = act[:, :, width + h * hd:width + (h + 1) * hd]
        qs_ref[:, :, sl] = qh * lax.rsqrt(jnp.sum(qh * qh, axis=-1, keepdims=True) + NORM_EPS) * (hd ** -0.5)
        ks_ref[:, :, sl] = kh * lax.rsqrt(jnp.sum(kh * kh, axis=-1, keepdims=True) + NORM_EPS)
    vs_ref[...] = act[:, :, 2 * width:3 * width]
    ba = ba_ref[...]
    lane = lax.broadcasted_iota(jnp.int32, ba.shape, 2)
    gval = -jnp.exp(prm_ref[0:1, :]) * _softplus(ba + prm_ref[1:2, :])
    gb_ref[...] = jnp.where(lane < nh, _sigmoid(ba), jnp.where(lane < 2 * nh, gval, 0.0))

    pairs = [(s, h) for s in range(bs) for h in range(nh)]
    npairs = len(pairs)
    nrow = npairs * chunk
    lc = chunk.bit_length() - 1
    row = lax.broadcasted_iota(jnp.int32, (nrow, nrow), 0)
    col = lax.broadcasted_iota(jnp.int32, (nrow, nrow), 1)
    same = lax.shift_right_logical(row, lc) == lax.shift_right_logical(col, lc)
    lower = jnp.logical_and(same, row >= col)
    strict = jnp.logical_and(same, row > col)
    tri = lower.astype(BF16)
    last_sel = col == (row | (chunk - 1))
    prow = lax.shift_right_logical(lax.broadcasted_iota(jnp.int32, (nrow, LANES), 0), lc)
    plane = lax.broadcasted_iota(jnp.int32, (nrow, LANES), 1)
    head_of_row = prow & (nh - 1)
    is_beta = plane == head_of_row
    is_g = plane == head_of_row + nh
    ones8 = jnp.ones((BF16_ROWS, LANES), BF16)
    pair_of_row = lax.shift_right_logical(lax.broadcasted_iota(jnp.int32, (nrow, hd), 0), lc)

    def expand(x):
        return jnp.concatenate([jnp.where(pair_of_row == p, x, 0.0) for p in range(npairs)], axis=1)

    def one_chunk(c):
        rows = pl.ds(c * chunk, chunk)
        stack = lambda ref: jnp.concatenate([ref[s, rows, h * hd:(h + 1) * hd] for s, h in pairs], axis=0)
        qst, kst, vst = stack(qs_ref), stack(ks_ref), stack(vs_ref)
        gbst = jnp.concatenate([gb_ref[s, rows, :] for s, _ in pairs], axis=0)
        g_hi, g_lo = _split(gbst)
        gcum_all = (jnp.dot(tri, g_hi, preferred_element_type=F32)
                    + jnp.dot(tri, g_lo, preferred_element_type=F32))
        gsel = jnp.where(is_g, gcum_all, 0.0)
        gc = jnp.sum(gsel, axis=-1, keepdims=True)
        beta = jnp.sum(jnp.where(is_beta, gbst, 0.0), axis=-1, keepdims=True)
        s_hi, s_lo = _split(gsel)
        grow = (lax.dot_general(ones8, s_hi, _NT, preferred_element_type=F32)
                + lax.dot_general(ones8, s_lo, _NT, preferred_element_type=F32))[0:1, :]
        decay = jnp.where(lower, jnp.exp(jnp.minimum(gc - grow, 0.0)), 0.0)
        kb = kst * beta
        a = _mm(kb, kst, _NT) * jnp.where(strict, decay, 0.0)
        tinv = _block_tri_inverse(a, nrow, chunk)
        u = _mm(tinv, vst * beta)
        w = _mm(tinv, kb * jnp.exp(gc))
        attn = _mm(qst, kst, _NT) * decay
        s_stack = jnp.concatenate([s_ref[s, h] for s, h in pairs], axis=0)
        v_new = u - _mm(expand(w), s_stack)
        o = _mm(expand(qst * jnp.exp(gc)), s_stack) + _mm(attn, v_new)
        g_last = jnp.sum(jnp.where(last_sel, grow, 0.0), axis=-1, keepdims=True)
        upd = _mm(expand(kst * jnp.exp(g_last - gc)), v_new, _TN)
        gt = stack(gate_ref)
        on = _rms(o, nrm_ref[...]) * (gt * _sigmoid(gt))
        for idx, (s, h) in enumerate(pairs):
            gl = g_last[idx * chunk:idx * chunk + 1, :]
            s_ref[s, h] = s_ref[s, h] * jnp.exp(gl) + upd[idx * hd:(idx + 1) * hd, :]
            o_ref[s, rows, h * hd:(h + 1) * hd] = on[idx * chunk:(idx + 1) * chunk, :]

    for c in range(tt // chunk):
        one_chunk(c)

    @pl.when(ti == pl.num_programs(1) - 1)
    def _():
        sout_ref[...] = s_ref[...]


def _deltanet(rest3, col0, conv_w, prm, nrm, cbuf, s0, nh, hd):
    nseq, t, _ = rest3.shape
    width = nh * hd
    taps = conv_w.shape[0]
    chunk = min(DN_CHUNK, t)
    assert t % chunk == 0 and chunk & (chunk - 1) == 0 and t >= taps - 1
    tt = _tile(t, 256, chunk)
    bs = _tile(nseq, 8, 1) if tt == t and t <= SUBLANES else 1
    ba_col = (col0 + 4) * (width // LANES)
    blk = lambda cidx: pl.BlockSpec((bs, tt, width), lambda i, j: (i, j, cidx))
    const = lambda i, j: (0, 0)
    return pl.pallas_call(
        functools.partial(_dn_kernel, bs=bs, tt=tt, chunk=chunk, nh=nh, hd=hd, taps=taps),
        grid=(nseq // bs, t // tt),
        in_specs=[
            blk(col0 + 1), blk(col0 + 2), blk(col0 + 3),
            pl.BlockSpec((bs, tt, LANES), lambda i, j: (i, j, ba_col)),
            blk(col0),
            pl.BlockSpec((taps, 3 * width), const),
            pl.BlockSpec((SUBLANES, LANES), const),
            pl.BlockSpec((1, hd), const),
            pl.BlockSpec((bs, taps - 1, 3 * width), lambda i, j: (i, 0, 0)),
            pl.BlockSpec((bs, nh, hd, hd), lambda i, j: (i, 0, 0, 0)),
        ],
        out_specs=[
            pl.BlockSpec((bs, tt, width), lambda i, j: (i, j, 0)),
            pl.BlockSpec((bs, nh, hd, hd), lambda i, j: (i, 0, 0, 0)),
            pl.BlockSpec((bs, taps - 1, 3 * width), lambda i, j: (i, 0, 0)),
        ],
        out_shape=[
            jax.ShapeDtypeStruct((nseq, t, width), F32),
            jax.ShapeDtypeStruct((nseq, nh, hd, hd), F32),
            jax.ShapeDtypeStruct((nseq, taps - 1, 3 * width), F32),
        ],
        scratch_shapes=[
            pltpu.VMEM((bs, tt + 2 * SUBLANES, 3 * width), F32),
            pltpu.VMEM((bs, tt, width), F32),
            pltpu.VMEM((bs, tt, width), F32),
            pltpu.VMEM((bs, tt, width), F32),
            pltpu.VMEM((bs, tt, LANES), F32),
            pltpu.VMEM((bs, nh, hd, hd), F32),
        ],
        compiler_params=_params("parallel", "arbitrary"),
        name="deltanet",
    )(rest3, rest3, rest3, rest3, rest3, conv_w, prm, nrm, cbuf, s0)


def _outproj_kernel(oa_ref, ob_ref, oc_ref, ga_ref, gb_ref, w_ref, x_ref, o_ref, mix_ref):
    wa, wb = oa_ref.shape[-1], ob_ref.shape[-1]

    @pl.when(pl.program_id(1) == 0)
    def _():
        mix_ref[:, 0:wa] = _rms(oa_ref[...], ga_ref[...]).astype(BF16)
        mix_ref[:, wa:wa + wb] = _rms(ob_ref[...], gb_ref[...]).astype(BF16)
        mix_ref[:, wa + wb:] = oc_ref[...].astype(BF16)

    o_ref[...] = x_ref[...] + jnp.dot(mix_ref[...], w_ref[...], preferred_element_type=F32)


def _outproj(oa, ob, oc, ga, gb, w, x2d):
    m, d = x2d.shape
    tm = _tile(m, 512, SUBLANES)
    tn = _tile(d, 1024)
    row = lambda a: pl.BlockSpec((tm, a.shape[-1]), lambda i, j: (i, 0))
    const = lambda a: pl.BlockSpec((1, a.shape[-1]), lambda i, j: (0, 0))
    return pl.pallas_call(
        _outproj_kernel,
        grid=(m // tm, d // tn),
        in_specs=[row(oa), row(ob), row(oc), const(ga), const(gb),
                  pl.BlockSpec((w.shape[0], tn), lambda i, j: (0, j)),
                  pl.BlockSpec((tm, tn), lambda i, j: (i, j))],
        out_specs=pl.BlockSpec((tm, tn), lambda i, j: (i, j)),
        out_shape=jax.ShapeDtypeStruct((m, d), F32),
        scratch_shapes=[pltpu.VMEM((tm, w.shape[0]), BF16)],
        compiler_params=_params("parallel", "arbitrary"),
        name="outproj",
    )(oa, ob, oc, ga, gb, w, x2d)


FFN_SUBTILE = 256


def _ffn_kernel(x_ref, g_ref, wg_ref, wv_ref, wd_ref, cw_ref, st_ref, o_ref, buf_ref, h_ref, carry_ref,
                *, tiles_per_seq, per_seq_rows):
    i = pl.program_id(0)
    f = pl.program_id(1)
    tm, tf = h_ref.shape[0], wg_ref.shape[1]

    @pl.when(f == 0)
    def _():
        h_ref[...] = _rms(x_ref[...], g_ref[...]).astype(BF16)

    h = h_ref[...]
    if not per_seq_rows:
        @pl.when((i % tiles_per_seq) == 0)
        def _():
            carry_ref[f] = st_ref[...]

    tc = FFN_SUBTILE if tf % FFN_SUBTILE == 0 else tf
    row = lax.broadcasted_iota(jnp.int32, (tm, tc), 0)
    down = None
    for s in range(tf // tc):
        cols = slice(s * tc, (s + 1) * tc)
        gate = jnp.dot(h, wg_ref[:, cols], preferred_element_type=F32)
        val = jnp.dot(h, wv_ref[:, cols], preferred_element_type=F32)
        prev1 = pltpu.roll(gate, 1, axis=0)
        prev2 = pltpu.roll(gate, 2, axis=0)
        if per_seq_rows:
            nseq = tm // per_seq_rows
            expand = lambda a: jnp.broadcast_to(a, (nseq, per_seq_rows, tc)).reshape(tm, tc)
            s0, s1 = expand(st_ref[:, 0:1, cols]), expand(st_ref[:, 1:2, cols])
            t = row & (per_seq_rows - 1)
            buf_ref[:, :, cols] = gate.reshape(nseq, per_seq_rows, tc)[:, per_seq_rows - 2:, :]
        else:
            s0, s1 = carry_ref[f, 0:1, cols], carry_ref[f, 1:2, cols]
            t = row
            tail = gate[tm - 2:tm, :]
            carry_ref[f, :, cols] = tail
            buf_ref[:, cols] = tail
        x1 = jnp.where(t >= 1, prev1, s1)
        x2 = jnp.where(t >= 2, prev2, jnp.where(t == 1, s1, s0))
        conv = cw_ref[2:3, cols] * gate + cw_ref[1:2, cols] * x1 + cw_ref[0:1, cols] * x2
        act = (conv * _sigmoid(conv) * val).astype(BF16)
        part = jnp.dot(act, wd_ref[cols, :], preferred_element_type=F32)
        down = part if down is None else down + part

    @pl.when(f == 0)
    def _():
        o_ref[...] = x_ref[...] + down

    @pl.when(f > 0)
    def _():
        o_ref[...] += down


def _ffn(x2d, gain, wg, wv, wd, conv_w, state, seq):
    m, d = x2d.shape
    d_ff = wg.shape[1]
    nseq = m // seq
    assert conv_w.shape[0] == 3
    tf = _tile(d_ff, 512)
    nf = d_ff // tf
    if seq <= SUBLANES:
        assert seq == SUBLANES
        tm = _tile(m, 512, SUBLANES)
        per_seq_rows, tiles_per_seq = seq, 1
        st_spec = buf_spec = pl.BlockSpec((tm // seq, 2, tf), lambda i, f: (i, 0, f))
        n_buf = nseq
    else:
        tm = _tile(seq, 512, SUBLANES)
        per_seq_rows, tiles_per_seq = 0, seq // tm
        st_spec = pl.BlockSpec((None, 2, tf), lambda i, f: (i // tiles_per_seq, 0, f))
        buf_spec = pl.BlockSpec((None, 2, tf), lambda i, f: (i, 0, f))
        n_buf = m // tm
    y, buf = pl.pallas_call(
        functools.partial(_ffn_kernel, tiles_per_seq=tiles_per_seq, per_seq_rows=per_seq_rows),
        grid=(m // tm, nf),
        in_specs=[
            pl.BlockSpec((tm, d), lambda i, f: (i, 0)),
            pl.BlockSpec((1, d), lambda i, f: (0, 0)),
            pl.BlockSpec((d, tf), lambda i, f: (0, f)),
            pl.BlockSpec((d, tf), lambda i, f: (0, f)),
            pl.BlockSpec((tf, d), lambda i, f: (f, 0)),
            pl.BlockSpec((3, tf), lambda i, f: (0, f)),
            st_spec,
        ],
        out_specs=[pl.BlockSpec((tm, d), lambda i, f: (i, 0)), buf_spec],
        out_shape=[jax.ShapeDtypeStruct((m, d), F32), jax.ShapeDtypeStruct((n_buf, 2, d_ff), F32)],
        scratch_shapes=[pltpu.VMEM((tm, d), BF16), pltpu.VMEM((nf, 2, tf), F32)],
        compiler_params=_params("arbitrary", "arbitrary"),
        name="convffn",
    )(x2d, gain, wg, wv, wd, conv_w, state)
    if not per_seq_rows:
        buf = buf.reshape(nseq, tiles_per_seq, 2, d_ff)[:, tiles_per_seq - 1]
    return y, buf


def _layer_weights(l, w_in, sb_q_norm, sb_k_norm, ssm, dn_a_log, dn_dt_bias, ssm_w_glu, w_out, ffn_w_in, ffn_w_out,
                   dims):
    sbw, ssmw, dnw, nh = dims["sbw"], dims["ssmw"], dims["dnw"], dims["dn_heads"]
    heads, hd = dims["sb_heads"], dims["sb_hd"]
    d = w_in.shape[1]
    o_u = 3 * sbw
    o_qkv = o_u + ssmw
    o_ba = o_qkv + 3 * dnw
    o_gate = o_ba + 2 * nh
    n_rest = -(-(ssmw + 4 * dnw + LANES) // sbw) * sbw
    pad = n_rest - (ssmw + 4 * dnw + 2 * nh)
    w = w_in[l]
    w_perm = jnp.concatenate([
        w[:, :o_u], w[:, o_u:o_qkv], w[:, o_gate:o_gate + dnw], w[:, o_qkv:o_ba], w[:, o_ba:o_gate],
        jnp.zeros((d, pad), w.dtype)], axis=1).astype(BF16)
    head_of = jnp.arange(sbw) // hd
    gs = (head_of[:, None] == jnp.arange(LANES)[None, :]).astype(BF16)
    qg = jnp.tile(sb_q_norm[l], heads)[None, :]
    kg = jnp.tile(sb_k_norm[l], heads)[None, :]
    a_re, a_im, bb_re, bb_im = ssm["disc"]
    g, n, ch = dims["ssm_groups"], dims["ssm_state"], dims["ssm_ch"]
    ns = g * n
    grp_of_state = jnp.arange(ns) // n
    grp_of_chan = jnp.arange(ssmw) // ch
    blockmask = grp_of_chan[:, None] == grp_of_state[None, :]
    expand_b = lambda bb: jnp.where(blockmask, jnp.tile(bb.T, (g, 1)), 0.0)
    bmat = jnp.concatenate([expand_b(bb_re), expand_b(bb_im)], axis=1).astype(BF16)
    arow = jnp.concatenate([a_re, a_im], axis=0).reshape(1, 2 * ns)
    expand_c = lambda c: jnp.where(blockmask.T, jnp.tile(c.transpose(0, 2, 1).reshape(ns, ch), (1, g)), 0.0)
    cmat = jnp.concatenate([expand_c(ssm["c_re"][l]), -expand_c(ssm["c_im"][l])], axis=0).astype(BF16)
    prm = jnp.zeros((SUBLANES, LANES), F32)
    prm = prm.at[0, nh:2 * nh].set(dn_a_log[l]).at[1, nh:2 * nh].set(dn_dt_bias[l])
    d_ff = ffn_w_in.shape[2] // 2
    return dict(
        w_perm=w_perm, gs=gs, e=gs.T, qg=qg, kg=kg, bmat=bmat, arow=arow, cmat=cmat, prm=prm,
        wglu=ssm_w_glu[l].astype(BF16), w_out=w_out[l].astype(BF16),
        wg=ffn_w_in[l][:, :d_ff].astype(BF16), wv=ffn_w_in[l][:, d_ff:].astype(BF16),
        wd=ffn_w_out[l].astype(BF16), n_rest=n_rest)


def kernel(x_prompt, x_sample, cache_k, cache_v, page_table, state_ssm_re, state_ssm_im, state_delta, state_delta_conv, state_ffn_conv, norm_mix, w_in, sb_q_norm, sb_k_norm, sb_logit_bias, ssm_lambda_re, ssm_lambda_im, ssm_log_step, ssm_b_re, ssm_b_im, ssm_c_re, ssm_c_im, ssm_d, ssm_w_glu, ssm_b_glu, dn_conv_w, dn_a_log, dn_dt_bias, dn_norm, norm_out_sb, norm_out_ssm, w_out, norm_ffn, ffn_w_in, ffn_conv_w, ffn_w_out):
    depth = w_in.shape[0]
    bp, tp, d = x_prompt.shape
    bd, td, _ = x_sample.shape
    sb_heads, sb_hd = cache_k.shape[-2], cache_k.shape[-1]
    g, n = ssm_lambda_re.shape[1:]
    ch = ssm_b_re.shape[-1]
    dn_heads, dn_hd = dn_a_log.shape[1], dn_norm.shape[1]
    dims = dict(sbw=sb_heads * sb_hd, sb_heads=sb_heads, sb_hd=sb_hd, ssmw=g * ch, ssm_groups=g, ssm_state=n,
                ssm_ch=ch, dnw=dn_heads * dn_hd, dn_heads=dn_heads)
    sbw, ssmw, dnw, ns = dims["sbw"], dims["ssmw"], dims["dnw"], g * n
    assert ssmw % dnw == 0 and dnw % LANES == 0 and sbw % LANES == 0 and LANES % sb_hd == 0
    d_ff = ffn_conv_w.shape[2]
    col0 = ssmw // dnw
    q_scale = sb_hd ** -0.5

    groups = {
        "p": dict(x=x_prompt.reshape(bp * tp, d), b=bp, t=tp, kbuf=None, vbuf=None, outs=[]),
        "s": dict(x=x_sample.reshape(bd * td, d), b=bd, t=td, kbuf=None, vbuf=None, outs=[]),
    }
    for l in range(depth):
        disc = _s5_discretize(ssm_lambda_re[l], ssm_lambda_im[l], ssm_log_step[l], ssm_b_re[l], ssm_b_im[l])
        lw = _layer_weights(l, w_in, sb_q_norm, sb_k_norm, dict(disc=disc, c_re=ssm_c_re, c_im=ssm_c_im),
                            dn_a_log, dn_dt_bias, ssm_w_glu, w_out, ffn_w_in, ffn_w_out, dims)
        for name, grp in groups.items():
            b, t, x = grp["b"], grp["t"], grp["x"]
            q, kbuf, vbuf, rest = _inproj(x, norm_mix[l][None, :], lw["w_perm"], lw["gs"], lw["e"], lw["qg"],
                                          lw["kg"], grp["kbuf"], grp["vbuf"], l, depth, sb_hd, q_scale)
            grp["kbuf"], grp["vbuf"] = kbuf, vbuf
            rest3 = rest.reshape(b, t, lw["n_rest"])
            if name == "p":
                o_a = _sb_prompt(q, kbuf, vbuf, sb_logit_bias[l], l, b, t, sb_hd)
                h0 = jnp.zeros((b, 2 * ns), F32)
                dn_s0 = jnp.zeros((b, dn_heads, dn_hd, dn_hd), F32)
                dn_c0 = jnp.zeros((b, dn_conv_w.shape[1] - 1, 3 * dnw), F32)
                ffn_c0 = jnp.zeros((b, ffn_conv_w.shape[1] - 1, d_ff), F32)
            else:
                o_a = _sb_sample(q, kbuf, vbuf, cache_k, cache_v, page_table, sb_logit_bias[l], l, b, t, sb_hd)
                h0 = jnp.concatenate([state_ssm_re[l].reshape(b, ns), state_ssm_im[l].reshape(b, ns)], axis=1)
                dn_s0, dn_c0, ffn_c0 = state_delta[l], state_delta_conv[l], state_ffn_conv[l]
            u_tm = rest3[:, :, :ssmw].transpose(1, 0, 2)
            y_tm, h_fin = _s5(u_tm, lw["bmat"], lw["arow"], h0, lw["cmat"], ssm_d[l][None, :], lw["wglu"],
                              ssm_b_glu[l][None, :])
            o_b = y_tm.transpose(1, 0, 2).reshape(b * t, ssmw)
            o_c, dn_s, dn_c = _deltanet(rest3, col0, dn_conv_w[l], lw["prm"], dn_norm[l][None, :], dn_c0, dn_s0,
                                        dn_heads, dn_hd)
            x = _outproj(o_a, o_b, o_c.reshape(b * t, dnw), norm_out_sb[l][None, :], norm_out_ssm[l][None, :],
                         lw["w_out"], x)
            x, ffn_c = _ffn(x, norm_ffn[l][None, :], lw["wg"], lw["wv"], lw["wd"], ffn_conv_w[l], ffn_c0, t)
            grp["x"] = x
            grp["outs"].append((h_fin[:, :ns].reshape(b, g, n), h_fin[:, ns:].reshape(b, g, n), dn_s, dn_c, ffn_c))

    res = {}
    for name, grp in groups.items():
        b, t = grp["b"], grp["t"]
        sre, sim, dn_s, dn_c, ffn_c = [jnp.stack(z) for z in zip(*grp["outs"])]
        res[name] = dict(
            y=grp["x"].reshape(b, t, d),
            k=grp["kbuf"].reshape(depth, b, t, sb_heads, sb_hd), v=grp["vbuf"].reshape(depth, b, t, sb_heads, sb_hd),
            sre=sre, sim=sim, dn_s=dn_s, dn_c=dn_c, ffn_c=ffn_c)
    p, s = res["p"], res["s"]
    return (p["y"], s["y"], p["k"], p["v"], s["k"], s["v"], p["sre"], p["sim"], s["sre"], s["sim"],
            p["dn_s"], s["dn_s"], p["dn_c"], s["dn_c"], p["ffn_c"], s["ffn_c"])
```

```python
import functools
import math

import jax
import jax.numpy as jnp
from jax import lax
from jax.experimental import pallas as pl
from jax.experimental.pallas import tpu as pltpu

F32 = jnp.float32
BF16 = jnp.bfloat16
HIGHEST = lax.Precision.HIGHEST
NORM_EPS = 1e-6
LANES = 128
SUBLANES = 8
BF16_ROWS = 16
VMEM_LIMIT_BYTES = 56 * 1024 * 1024
DN_CHUNK = 64
TRI_BASE = 16

_NT = (((1,), (1,)), ((), ()))
_TN = (((0,), (0,)), ((), ()))


def _tile(dim, pref, mult=LANES):
    if dim <= pref:
        return dim
    t = (pref // mult) * mult
    while t >= mult:
        if dim % t == 0:
            return t
        t -= mult
    raise ValueError(f"no tile for {dim} (pref {pref}, mult {mult})")


def _params(*sem):
    return pltpu.CompilerParams(dimension_semantics=sem, vmem_limit_bytes=VMEM_LIMIT_BYTES)


def _mm(a, b, dims=None):
    ok = all(d % BF16_ROWS == 0 for d in a.shape + b.shape)
    if ok:
        a, b = a.astype(BF16), b.astype(BF16)
    else:
        a, b = a.astype(F32), b.astype(F32)
    if dims is None:
        return jnp.dot(a, b, preferred_element_type=F32)
    return lax.dot_general(a, b, dims, preferred_element_type=F32)


def _mm_f32(a, b, dims=None):
    if dims is None:
        return jnp.dot(a, b, preferred_element_type=F32, precision=HIGHEST)
    return lax.dot_general(a, b, dims, preferred_element_type=F32, precision=HIGHEST)


def _softplus(z):
    return jnp.maximum(z, 0.0) + jnp.log1p(jnp.exp(-jnp.abs(z)))


def _sigmoid(z):
    return 1.0 / (1.0 + jnp.exp(-z))


def _rms(x, gain, eps=NORM_EPS):
    return x * lax.rsqrt(jnp.mean(x * x, axis=-1, keepdims=True) + eps) * gain


def _inproj_kernel(x_ref, g_ref, w_ref, gs_ref, e_ref, qg_ref, kg_ref, *refs, head_dim, q_scale, aliased,
                   time_minor):
    q_ref, k_ref, v_ref, r_ref, xn_ref = refs[2:] if aliased else refs
    j = pl.program_id(1)
    kv_layout = (lambda a: a.T) if time_minor else (lambda a: a)

    @pl.when(j == 0)
    def _():
        xn_ref[...] = _rms(x_ref[...], g_ref[...]).astype(BF16)

    y = jnp.dot(xn_ref[...], w_ref[...], preferred_element_type=F32)

    def head_norm(gain):
        ms = jnp.dot((y * y).astype(BF16), gs_ref[...], preferred_element_type=F32) * (1.0 / head_dim)
        inv = lax.rsqrt(ms + NORM_EPS)
        hi = inv.astype(BF16)
        lo = (inv - hi.astype(F32)).astype(BF16)
        sc = (jnp.dot(hi, e_ref[...], preferred_element_type=F32)
              + jnp.dot(lo, e_ref[...], preferred_element_type=F32))
        return y * sc * gain

    @pl.when(j == 0)
    def _():
        q_ref[...] = (head_norm(qg_ref[...]) * q_scale).astype(BF16)

    @pl.when(j == 1)
    def _():
        k_ref[...] = kv_layout(head_norm(kg_ref[...]))

    @pl.when(j == 2)
    def _():
        v_ref[...] = kv_layout(y)

    @pl.when(j >= 3)
    def _():
        r_ref[...] = y


def _inproj(x2d, gain, w_perm, gs, e, qg, kg, kbuf, vbuf, layer, depth, head_dim, q_scale, seq, time_minor):
    m, d = x2d.shape
    tn = gs.shape[0]
    n_rest = w_perm.shape[1] - 3 * tn
    tm = _tile(seq, 512, LANES) if time_minor else _tile(m, 512, SUBLANES)
    tiles_per_seq = seq // tm
    aliased = kbuf is not None
    const = lambda i, j: (0, 0)
    in_specs = [
        pl.BlockSpec((tm, d), lambda i, j: (i, 0)),
        pl.BlockSpec((1, d), const),
        pl.BlockSpec((d, tn), lambda i, j: (0, j)),
        pl.BlockSpec((tn, LANES), const),
        pl.BlockSpec((LANES, tn), const),
        pl.BlockSpec((1, tn), const),
        pl.BlockSpec((1, tn), const),
    ]
    args = [x2d, gain, w_perm, gs, e, qg, kg]
    aliases = {}
    if aliased:
        in_specs += [pl.BlockSpec(memory_space=pl.ANY), pl.BlockSpec(memory_space=pl.ANY)]
        args += [kbuf, vbuf]
        aliases = {7: 1, 8: 2}
    if time_minor:
        kv_spec = pl.BlockSpec((None, None, tn, tm), lambda i, j: (layer, i // tiles_per_seq, 0, i % tiles_per_seq))
        kv_shape = jax.ShapeDtypeStruct((depth, m // seq, tn, seq), F32)
    else:
        kv_spec = pl.BlockSpec((None, tm, tn), lambda i, j: (layer, i, 0))
        kv_shape = jax.ShapeDtypeStruct((depth, m, tn), F32)
    out_specs = [
        pl.BlockSpec((tm, tn), lambda i, j: (i, 0)),
        kv_spec,
        kv_spec,
        pl.BlockSpec((tm, tn), lambda i, j: (i, jnp.maximum(j - 3, 0))),
    ]
    out_shape = [jax.ShapeDtypeStruct((m, tn), BF16), kv_shape, kv_shape, jax.ShapeDtypeStruct((m, n_rest), F32)]
    return pl.pallas_call(
        functools.partial(_inproj_kernel, head_dim=head_dim, q_scale=q_scale, aliased=aliased,
                          time_minor=time_minor),
        grid=(m // tm, 3 + n_rest // tn),
        in_specs=in_specs, out_specs=out_specs, out_shape=out_shape,
        scratch_shapes=[pltpu.VMEM((tm, d), BF16)],
        input_output_aliases=aliases,
        compiler_params=_params("parallel", "arbitrary"),
        name="inproj",
    )(*args)


def _softplus_unit(z):
    return jnp.maximum(z, 0.0) + jnp.log(1.0 + jnp.exp(-jnp.abs(z)))


def _sb_prompt_kernel(bias_ref, q_ref, k_ref, v_ref, o_ref, *, blk, nsub, head_dim, heads_per_step):
    hp = pl.program_id(1)
    qi = pl.program_id(2)
    row = lax.broadcasted_iota(jnp.int32, (blk, blk), 0)
    col = lax.broadcasted_iota(jnp.int32, (blk, blk), 1)
    later = (row > col).astype(BF16)
    visible = col < row
    q = q_ref[...]
    heads = range(heads_per_step)
    lanes = [slice(hh * head_dim, (hh + 1) * head_dim) for hh in heads]
    qs = [[q[j * blk:(j + 1) * blk, lanes[hh]] for j in range(nsub)] for hh in heads]
    biases = [bias_ref[hp * heads_per_step + hh] for hh in heads]

    def load(ki):
        r0 = pl.multiple_of(ki * blk, blk)
        kb = k_ref[:, pl.ds(r0, blk)].astype(BF16)
        vb = v_ref[:, pl.ds(r0, blk)].astype(BF16)
        return [(kb[lanes[hh], :], vb[lanes[hh], :]) for hh in heads]

    def block(qh, kv, bias, oc, masked):
        o_acc, carry = oc
        z = jnp.dot(qh, kv[0], preferred_element_type=F32) + bias
        sp = _softplus_unit(z)
        if masked:
            sp = jnp.where(visible, sp, 0.0)
        after = jnp.dot(sp.astype(BF16), later, preferred_element_type=F32)
        w = jnp.exp(z - sp - after - carry)
        if masked:
            w = jnp.where(visible, w, 0.0)
        o_acc = o_acc + lax.dot_general(w.astype(BF16), kv[1], _NT, preferred_element_type=F32)
        return o_acc, carry + after[:, 0:1] + sp[:, 0:1]

    zero = (jnp.zeros((blk, head_dim), F32), jnp.zeros((blk, 1), F32))
    state = [[zero for _ in range(nsub)] for _ in heads]
    for d in reversed(range(nsub)):
        kv = load(qi * nsub + d)
        for hh in heads:
            for j in range(d, nsub):
                state[hh][j] = block(qs[hh][j], kv[hh], biases[hh], state[hh][j], j == d)

    def older(it, flat):
        kv = load(qi * nsub - 1 - it)
        return tuple(block(qs[hh][j], kv[hh], biases[hh], flat[hh * nsub + j], False)
                     for hh in heads for j in range(nsub))

    flat = lax.fori_loop(0, qi * nsub, older, tuple(state[hh][j] for hh in heads for j in range(nsub)))
    o_ref[...] = jnp.concatenate(
        [jnp.concatenate([flat[hh * nsub + j][0] for hh in heads], axis=-1) for j in range(nsub)], axis=0)


def _sb_prompt(q, kbuf, vbuf, bias, layer, bsz, seq, head_dim):
    width = q.shape[-1]
    heads_per_step = LANES // head_dim
    blk = _tile(seq, 256)
    nsub = 2 if seq % (2 * blk) == 0 else 1
    rows = nsub * blk
    q3 = q.reshape(bsz, seq, width)
    return pl.pallas_call(
        functools.partial(_sb_prompt_kernel, blk=blk, nsub=nsub, head_dim=head_dim, heads_per_step=heads_per_step),
        grid=(bsz, width // LANES, seq // rows),
        in_specs=[
            pl.BlockSpec(memory_space=pltpu.SMEM),
            pl.BlockSpec((None, rows, LANES), lambda b, h, i: (b, i, h)),
            pl.BlockSpec((None, None, LANES, seq), lambda b, h, i: (layer, b, h, 0)),
            pl.BlockSpec((None, None, LANES, seq), lambda b, h, i: (layer, b, h, 0)),
        ],
        out_specs=pl.BlockSpec((None, rows, LANES), lambda b, h, i: (b, i, h)),
        out_shape=jax.ShapeDtypeStruct((bsz, seq, width), F32),
        compiler_params=_params("parallel", "parallel", "arbitrary"),
        name="sb_prompt",
    )(bias, q3, kbuf, vbuf).reshape(bsz * seq, width)


def _sb_sample_kernel(pt_ref, q_ref, bcol_ref, kn_ref, vn_ref, *refs, pages_per_step, tq, heads, head_dim):
    del pt_ref
    kp = refs[:pages_per_step]
    vp = refs[pages_per_step:2 * pages_per_step]
    o_ref, acc_ref, carry_ref = refs[2 * pages_per_step:]
    c = pl.program_id(1)
    nr = heads * tq
    ltq = tq.bit_length() - 1
    page = kp[0].shape[-1]
    qf = q_ref[...].astype(F32)
    width = qf.shape[-1]
    bcol = bcol_ref[...]
    r = lax.broadcasted_iota(jnp.int32, (nr, width), 0)
    col = lax.broadcasted_iota(jnp.int32, (nr, width), 1)
    own_head = lax.shift_right_logical(r, ltq) == lax.shift_right_logical(col, head_dim.bit_length() - 1)
    qbd = jnp.where(own_head, jnp.concatenate([qf] * heads, axis=0), 0.0)

    @pl.when(c == 0)
    def _():
        z = lax.dot_general(qbd, kn_ref[...], _NT, preferred_element_type=F32) + bcol
        key = lax.broadcasted_iota(jnp.int32, (nr, tq), 1)
        qry = lax.broadcasted_iota(jnp.int32, (nr, tq), 0) & (tq - 1)
        vis = key < qry
        sp = jnp.where(vis, _softplus_unit(z), 0.0)
        later = (lax.broadcasted_iota(jnp.int32, (tq, tq), 0)
                 > lax.broadcasted_iota(jnp.int32, (tq, tq), 1)).astype(F32)
        w = jnp.where(vis, jnp.exp(z - sp - _mm_f32(sp, later)), 0.0)
        acc_ref[...] = jnp.dot(w, vn_ref[...], preferred_element_type=F32)
        carry_ref[...] = jnp.sum(sp, axis=-1, keepdims=True)

    qbd16 = qbd.astype(BF16)
    later = (lax.broadcasted_iota(jnp.int32, (page, page), 0)
             > lax.broadcasted_iota(jnp.int32, (page, page), 1)).astype(BF16)
    for p in range(pages_per_step):
        kt = kp[p][...].reshape(width, page).astype(BF16)
        z = jnp.dot(qbd16, kt, preferred_element_type=F32) + bcol
        sp = _softplus_unit(z)
        after = jnp.dot(sp.astype(BF16), later, preferred_element_type=F32)
        carry = carry_ref[...]
        w = jnp.exp(z - sp - after - carry).astype(BF16)
        vt = vp[p][...].reshape(width, page).astype(BF16)
        acc_ref[...] += lax.dot_general(w, vt, _NT, preferred_element_type=F32)
        carry_ref[...] = carry + after[:, 0:1] + sp[:, 0:1]

    @pl.when(c == pl.num_programs(1) - 1)
    def _():
        acc = jnp.where(own_head, acc_ref[...], 0.0)
        o_ref[...] = acc.reshape(heads, tq, width).sum(axis=0)


def _sb_sample(q, kbuf, vbuf, cache_k, cache_v, page_table, bias, layer, bsz, tq, head_dim):
    width = q.shape[-1]
    heads = width // head_dim
    assert tq & (tq - 1) == 0 and head_dim & (head_dim - 1) == 0
    page = cache_k.shape[2]
    npages = page_table.shape[1]
    pps = _tile(npages, 8, 1)
    bcol = jnp.repeat(bias.astype(F32), tq)[:, None]
    cache_k = cache_k.transpose(0, 1, 3, 4, 2)
    cache_v = cache_v.transpose(0, 1, 3, 4, 2)

    def page_map(p):
        return lambda b, c, pt: (layer, pt[b * npages + npages - 1 - (c * pps + p)], 0, 0, 0)

    page_specs = [pl.BlockSpec((None, None, heads, head_dim, page), page_map(p)) for p in range(pps)]
    grid_spec = pltpu.PrefetchScalarGridSpec(
        num_scalar_prefetch=1,
        grid=(bsz, npages // pps),
        in_specs=[
            pl.BlockSpec((None, tq, width), lambda b, c, pt: (b, 0, 0)),
            pl.BlockSpec((heads * tq, 1), lambda b, c, pt: (0, 0)),
            pl.BlockSpec((None, None, tq, width), lambda b, c, pt: (layer, b, 0, 0)),
            pl.BlockSpec((None, None, tq, width), lambda b, c, pt: (layer, b, 0, 0)),
        ] + page_specs + page_specs,
        out_specs=pl.BlockSpec((None, tq, width), lambda b, c, pt: (b, 0, 0)),
        scratch_shapes=[pltpu.VMEM((heads * tq, width), F32), pltpu.VMEM((heads * tq, 1), F32)],
    )
    k4 = kbuf.reshape(kbuf.shape[0], bsz, tq, width)
    v4 = vbuf.reshape(vbuf.shape[0], bsz, tq, width)
    out = pl.pallas_call(
        functools.partial(_sb_sample_kernel, pages_per_step=pps, tq=tq, heads=heads, head_dim=head_dim),
        grid_spec=grid_spec,
        out_shape=jax.ShapeDtypeStruct((bsz, tq, width), F32),
        compiler_params=_params("parallel", "arbitrary"),
        name="sb_sample",
    )(page_table.reshape(-1), q.reshape(bsz, tq, width), bcol, k4, v4,
      *([cache_k] * pps), *([cache_v] * pps))
    return out.reshape(bsz * tq, width)


def _s5_discretize_kernel(lre_ref, lim_ref, ls_ref, bre_ref, bim_ref, are_ref, aim_ref, bbre_ref, bbim_ref):
    lam_re, lam_im = lre_ref[...], lim_ref[...]
    dt = jnp.exp(ls_ref[...])
    mag = jnp.exp(lam_re * dt)
    ab_re = mag * jnp.cos(lam_im * dt)
    ab_im = mag * jnp.sin(lam_im * dt)
    den = lam_re * lam_re + lam_im * lam_im
    xr = ab_re - 1.0
    f_re = (xr * lam_re + ab_im * lam_im) / den
    f_im = (ab_im * lam_re - xr * lam_im) / den
    are_ref[...] = ab_re
    aim_ref[...] = ab_im
    bbre_ref[...] = f_re * bre_ref[...] - f_im * bim_ref[...]
    bbim_ref[...] = f_re * bim_ref[...] + f_im * bre_ref[...]


def _s5_discretize(lam_re, lam_im, log_step, b_re, b_im):
    g, n = lam_re.shape
    ch = b_re.shape[-1]
    col = lambda a: a.reshape(g * n, 1)
    ls = jnp.broadcast_to(log_step[:, None], (g, n))
    return pl.pallas_call(
        _s5_discretize_kernel,
        out_shape=[jax.ShapeDtypeStruct((g * n, 1), F32)] * 2 + [jax.ShapeDtypeStruct((g * n, ch), F32)] * 2,
        name="s5_discretize",
    )(col(lam_re), col(lam_im), col(ls), b_re.reshape(g * n, ch), b_im.reshape(g * n, ch))


def _s5_kernel(u_ref, bmat_ref, arow_ref, h0_ref, cmat_ref, d_ref, wglu_ref, bglu_ref,
               y_ref, hfin_ref, hbuf_ref, hc_ref, *, tt, nb, ns, cb):
    ti = pl.program_id(1)
    width = u_ref.shape[-1]

    @pl.when(ti == 0)
    def _():
        hc_ref[...] = h0_ref[...]

    u = u_ref[...].reshape(tt * nb, width)
    hbuf_ref[...] = jnp.dot(u.astype(BF16), bmat_ref[...], preferred_element_type=F32)

    for c in range(ns // cb):
        re = slice(c * cb, (c + 1) * cb)
        im = slice(ns + c * cb, ns + (c + 1) * cb)
        a_re = jnp.broadcast_to(arow_ref[:, re], (SUBLANES, cb))
        a_im = jnp.broadcast_to(arow_ref[:, im], (SUBLANES, cb))

        def batch_rows(bb, _):
            r0 = pl.multiple_of(bb * SUBLANES, SUBLANES)

            def step(t, h):
                h_re, h_im = h
                r = pl.multiple_of(t * nb + r0, SUBLANES)
                n_re = a_re * h_re - a_im * h_im + hbuf_ref[pl.ds(r, SUBLANES), re]
                n_im = a_re * h_im + a_im * h_re + hbuf_ref[pl.ds(r, SUBLANES), im]
                hbuf_ref[pl.ds(r, SUBLANES), re] = n_re
                hbuf_ref[pl.ds(r, SUBLANES), im] = n_im
                return n_re, n_im

            h = (hc_ref[pl.ds(r0, SUBLANES), re], hc_ref[pl.ds(r0, SUBLANES), im])
            h = lax.fori_loop(0, tt, step, h, unroll=min(tt, 8))
            hc_ref[pl.ds(r0, SUBLANES), re] = h[0]
            hc_ref[pl.ds(r0, SUBLANES), im] = h[1]
            return 0

        lax.fori_loop(0, nb // SUBLANES, batch_rows, 0)

    y = jnp.dot(hbuf_ref[...].astype(BF16), cmat_ref[...], preferred_element_type=F32) + d_ref[...] * u
    y = 0.5 * y * (1.0 + jnp.tanh(math.sqrt(2.0 / math.pi) * (y + 0.044715 * (y * y * y))))
    z = jnp.dot(y.astype(BF16), wglu_ref[...], preferred_element_type=F32) + bglu_ref[...]
    y_ref[...] = (y * _sigmoid(z)).reshape(tt, nb, width)

    @pl.when(ti == pl.num_programs(1) - 1)
    def _():
        hfin_ref[...] = hc_ref[...]


def _s5(u_tm, bmat, arow, h0, cmat, d_row, wglu, bglu):
    t, b, width = u_tm.shape
    ns2 = bmat.shape[1]
    nb = _tile(b, 64, SUBLANES)
    tt = _tile(t, max(SUBLANES, 512 // nb), 1)
    const = lambda i, j: (0, 0)
    return pl.pallas_call(
        functools.partial(_s5_kernel, tt=tt, nb=nb, ns=ns2 // 2, cb=_tile(ns2 // 2, 512)),
        grid=(b // nb, t // tt),
        in_specs=[
            pl.BlockSpec((tt, nb, width), lambda i, j: (j, i, 0)),
            pl.BlockSpec((width, ns2), const),
            pl.BlockSpec((1, ns2), const),
            pl.BlockSpec((nb, ns2), lambda i, j: (i, 0)),
            pl.BlockSpec((ns2, width), const),
            pl.BlockSpec((1, width), const),
            pl.BlockSpec((width, width), const),
            pl.BlockSpec((1, width), const),
        ],
        out_specs=[
            pl.BlockSpec((tt, nb, width), lambda i, j: (j, i, 0)),
            pl.BlockSpec((nb, ns2), lambda i, j: (i, 0)),
        ],
        out_shape=[jax.ShapeDtypeStruct((t, b, width), F32), jax.ShapeDtypeStruct((b, ns2), F32)],
        scratch_shapes=[pltpu.VMEM((tt * nb, ns2), F32), pltpu.VMEM((nb, ns2), F32)],
        compiler_params=_params("parallel", "arbitrary"),
        name="s5",
    )(u_tm, bmat, arow, h0, cmat, d_row, wglu, bglu)


def _split(a):
    hi = a.astype(BF16)
    return hi, (a - hi.astype(F32)).astype(BF16)


def _mm3(a, b):
    d = lambda x, y: jnp.dot(x, y, preferred_element_type=F32)
    return d(a[0], b[0]) + d(a[0], b[1]) + d(a[1], b[0])


def _block_tri_inverse(a, n, c):
    row = lax.broadcasted_iota(jnp.int32, (n, n), 0)
    col = lax.broadcasted_iota(jnp.int32, (n, n), 1)
    eye = (row == col).astype(F32)

    def same_block(b):
        s = b.bit_length() - 1
        return lax.shift_right_logical(row, s) == lax.shift_right_logical(col, s)

    b0 = min(c, TRI_BASE)
    neg = -jnp.where(same_block(b0), a, 0.0)
    t = eye + neg
    p = _split(neg)
    k = 1
    while 2 * k < b0:
        p = _split(_mm3(p, p))
        t = t + _mm3(_split(t), p)
        k *= 2
    b = b0
    while b < c:
        off = jnp.where(jnp.logical_and(same_block(2 * b), jnp.logical_not(same_block(b))), a, 0.0)
        ts = _split(t)
        t = t - _mm3(_split(_mm3(ts, _split(off))), ts)
        b *= 2
    return t


def _dn_kernel(q_ref, k_ref, v_ref, ba_ref, gate_ref, cw_ref, prm_ref, nrm_ref, cbuf_ref, s0_ref,
               o_ref, sout_ref, cout_ref,
               xpad_ref, qs_ref, ks_ref, vs_ref, gb_ref, s_ref, *, bs, tt, chunk, nh, hd, taps):
    ti = pl.program_id(1)
    width = nh * hd
    halo = SUBLANES
    first = halo - (taps - 1)

    @pl.when(ti == 0)
    def _():
        s_ref[...] = s0_ref[...]
        xpad_ref[:, first:halo, :] = cbuf_ref[...]

    xpad_ref[:, halo:halo + tt, 0:width] = q_ref[...]
    xpad_ref[:, halo:halo + tt, width:2 * width] = k_ref[...]
    xpad_ref[:, halo:halo + tt, 2 * width:3 * width] = v_ref[...]
    cw = cw_ref[...]
    y = cw[0:1, :] * xpad_ref[:, first:first + tt, :]
    for i in range(1, taps):
        y = y + cw[i:i + 1, :] * xpad_ref[:, first + i:first + i + tt, :]
    cout_ref[...] = xpad_ref[:, halo + tt - (taps - 1):halo + tt, :]
    xpad_ref[:, 0:halo, :] = xpad_ref[:, tt:tt + halo, :]
    act = y * _sigmoid(y)
    for h in range(nh):
        sl = slice(h * hd, (h + 1) * hd)
        qh = act[:, :, h * hd:(h + 1) * hd]
        kh = act[:, :, width + h * hd:width + (h + 1) * hd]
        qs_ref[:, :, sl] = qh * lax.rsqrt(jnp.sum(qh * qh, axis=-1, keepdims=True) + NORM_EPS) * (hd ** -0.5)
        ks_ref[:, :, sl] = kh * lax.rsqrt(jnp.sum(kh * kh, axis=-1, keepdims=True) + NORM_EPS)
    vs_ref[...] = act[:, :, 2 * width:3 * width]
    ba = ba_ref[...]
    lane = lax.broadcasted_iota(jnp.int32, ba.shape, 2)
    gval = -jnp.exp(prm_ref[0:1, :]) * _softplus(ba + prm_ref[1:2, :])
    gb_ref[...] = jnp.where(lane < nh, _sigmoid(ba), jnp.where(lane < 2 * nh, gval, 0.0))

    pairs = [(s, h) for s in range(bs) for h in range(nh)]
    npairs = len(pairs)
    nrow = npairs * chunk
    lc = chunk.bit_length() - 1
    row = lax.broadcasted_iota(jnp.int32, (nrow, nrow), 0)
    col = lax.broadcasted_iota(jnp.int32, (nrow, nrow), 1)
    same = lax.shift_right_logical(row, lc) == lax.shift_right_logical(col, lc)
    lower = jnp.logical_and(same, row >= col)
    strict = jnp.logical_and(same, row > col)
    tri = lower.astype(BF16)
    last_sel = col == (row | (chunk - 1))
    prow = lax.shift_right_logical(lax.broadcasted_iota(jnp.int32, (nrow, LANES), 0), lc)
    plane = lax.broadcasted_iota(jnp.int32, (nrow, LANES), 1)
    head_of_row = prow & (nh - 1)
    is_beta = plane == head_of_row
    is_g = plane == head_of_row + nh
    ones8 = jnp.ones((BF16_ROWS, LANES), BF16)
    pair_of_row = lax.shift_right_logical(lax.broadcasted_iota(jnp.int32, (nrow, hd), 0), lc)

    def expand(x):
        return jnp.concatenate([jnp.where(pair_of_row == p, x, 0.0) for p in range(npairs)], axis=1)

    def one_chunk(c):
        rows = pl.ds(c * chunk, chunk)
        stack = lambda ref: jnp.concatenate([ref[s, rows, h * hd:(h + 1) * hd] for s, h in pairs], axis=0)
        qst, kst, vst = stack(qs_ref), stack(ks_ref), stack(vs_ref)
        gbst = jnp.concatenate([gb_ref[s, rows, :] for s, _ in pairs], axis=0)
        g_hi, g_lo = _split(gbst)
        gcum_all = (jnp.dot(tri, g_hi, preferred_element_type=F32)
                    + jnp.dot(tri, g_lo, preferred_element_type=F32))
        gsel = jnp.where(is_g, gcum_all, 0.0)
        gc = jnp.sum(gsel, axis=-1, keepdims=True)
        beta = jnp.sum(jnp.where(is_beta, gbst, 0.0), axis=-1, keepdims=True)
        s_hi, s_lo = _split(gsel)
        grow = (lax.dot_general(ones8, s_hi, _NT, preferred_element_type=F32)
                + lax.dot_general(ones8, s_lo, _NT, preferred_element_type=F32))[0:1, :]
        decay = jnp.where(lower, jnp.exp(jnp.minimum(gc - grow, 0.0)), 0.0)
        kb = kst * beta
        a = _mm(kb, kst, _NT) * jnp.where(strict, decay, 0.0)
        tinv = _block_tri_inverse(a, nrow, chunk)
        u = _mm(tinv, vst * beta)
        w = _mm(tinv, kb * jnp.exp(gc))
        attn = _mm(qst, kst, _NT) * decay
        s_stack = jnp.concatenate([s_ref[s, h] for s, h in pairs], axis=0)
        v_new = u - _mm(expand(w), s_stack)
        o = _mm(expand(qst * jnp.exp(gc)), s_stack) + _mm(attn, v_new)
        g_last = jnp.sum(jnp.where(last_sel, grow, 0.0), axis=-1, keepdims=True)
        upd = _mm(expand(kst * jnp.exp(g_last - gc)), v_new, _TN)
        gt = stack(gate_ref)
        on = _rms(o, nrm_ref[...]) * (gt * _sigmoid(gt))
        for idx, (s, h) in enumerate(pairs):
            gl = g_last[idx * chunk:idx * chunk + 1, :]
            s_ref[s, h] = s_ref[s, h] * jnp.exp(gl) + upd[idx * hd:(idx + 1) * hd, :]
            o_ref[s, rows, h * hd:(h + 1) * hd] = on[idx * chunk:(idx + 1) * chunk, :]

    for c in range(tt // chunk):
        one_chunk(c)

    @pl.when(ti == pl.num_programs(1) - 1)
    def _():
        sout_ref[...] = s_ref[...]


def _deltanet(rest3, col0, conv_w, prm, nrm, cbuf, s0, nh, hd):
    nseq, t, _ = rest3.shape
    width = nh * hd
    taps = conv_w.shape[0]
    chunk = min(DN_CHUNK, t)
    assert t % chunk == 0 and chunk & (chunk - 1) == 0 and t >= taps - 1
    tt = _tile(t, 256, chunk)
    bs = _tile(nseq, 8, 1) if tt == t and t <= SUBLANES else 1
    ba_col = (col0 + 4) * (width // LANES)
    blk = lambda cidx: pl.BlockSpec((bs, tt, width), lambda i, j: (i, j, cidx))
    const = lambda i, j: (0, 0)
    return pl.pallas_call(
        functools.partial(_dn_kernel, bs=bs, tt=tt, chunk=chunk, nh=nh, hd=hd, taps=taps),
        grid=(nseq // bs, t // tt),
        in_specs=[
            blk(col0 + 1), blk(col0 + 2), blk(col0 + 3),
            pl.BlockSpec((bs, tt, LANES), lambda i, j: (i, j, ba_col)),
            blk(col0),
            pl.BlockSpec((taps, 3 * width), const),
            pl.BlockSpec((SUBLANES, LANES), const),
            pl.BlockSpec((1, hd), const),
            pl.BlockSpec((bs, taps - 1, 3 * width), lambda i, j: (i, 0, 0)),
            pl.BlockSpec((bs, nh, hd, hd), lambda i, j: (i, 0, 0, 0)),
        ],
        out_specs=[
            pl.BlockSpec((bs, tt, width), lambda i, j: (i, j, 0)),
            pl.BlockSpec((bs, nh, hd, hd), lambda i, j: (i, 0, 0, 0)),
            pl.BlockSpec((bs, taps - 1, 3 * width), lambda i, j: (i, 0, 0)),
        ],
        out_shape=[
            jax.ShapeDtypeStruct((nseq, t, width), F32),
            jax.ShapeDtypeStruct((nseq, nh, hd, hd), F32),
            jax.ShapeDtypeStruct((nseq, taps - 1, 3 * width), F32),
        ],
        scratch_shapes=[
            pltpu.VMEM((bs, tt + 2 * SUBLANES, 3 * width), F32),
            pltpu.VMEM((bs, tt, width), F32),
            pltpu.VMEM((bs, tt, width), F32),
            pltpu.VMEM((bs, tt, width), F32),
            pltpu.VMEM((bs, tt, LANES), F32),
            pltpu.VMEM((bs, nh, hd, hd), F32),
        ],
        compiler_params=_params("parallel", "arbitrary"),
        name="deltanet",
    )(rest3, rest3, rest3, rest3, rest3, conv_w, prm, nrm, cbuf, s0)


def _outproj_kernel(oa_ref, ob_ref, oc_ref, ga_ref, gb_ref, w_ref, x_ref, o_ref, mix_ref):
    wa, wb = oa_ref.shape[-1], ob_ref.shape[-1]

    @pl.when(pl.program_id(1) == 0)
    def _():
        mix_ref[:, 0:wa] = _rms(oa_ref[...], ga_ref[...]).astype(BF16)
        mix_ref[:, wa:wa + wb] = _rms(ob_ref[...], gb_ref[...]).astype(BF16)
        mix_ref[:, wa + wb:] = oc_ref[...].astype(BF16)

    o_ref[...] = x_ref[...] + jnp.dot(mix_ref[...], w_ref[...], preferred_element_type=F32)


def _outproj(oa, ob, oc, ga, gb, w, x2d):
    m, d = x2d.shape
    tm = _tile(m, 512, SUBLANES)
    tn = _tile(d, 1024)
    row = lambda a: pl.BlockSpec((tm, a.shape[-1]), lambda i, j: (i, 0))
    const = lambda a: pl.BlockSpec((1, a.shape[-1]), lambda i, j: (0, 0))
    return pl.pallas_call(
        _outproj_kernel,
        grid=(m // tm, d // tn),
        in_specs=[row(oa), row(ob), row(oc), const(ga), const(gb),
                  pl.BlockSpec((w.shape[0], tn), lambda i, j: (0, j)),
                  pl.BlockSpec((tm, tn), lambda i, j: (i, j))],
        out_specs=pl.BlockSpec((tm, tn), lambda i, j: (i, j)),
        out_shape=jax.ShapeDtypeStruct((m, d), F32),
        scratch_shapes=[pltpu.VMEM((tm, w.shape[0]), BF16)],
        compiler_params=_params("parallel", "arbitrary"),
        name="outproj",
    )(oa, ob, oc, ga, gb, w, x2d)


FFN_SUBTILE = 256


def _ffn_kernel(x_ref, g_ref, wg_ref, wv_ref, wd_ref, cw_ref, st_ref, o_ref, buf_ref, h_ref, carry_ref,
                *, tiles_per_seq, per_seq_rows):
    i = pl.program_id(0)
    f = pl.program_id(1)
    tm, tf = h_ref.shape[0], wg_ref.shape[1]

    @pl.when(f == 0)
    def _():
        h_ref[...] = _rms(x_ref[...], g_ref[...]).astype(BF16)

    h = h_ref[...]
    if not per_seq_rows:
        @pl.when((i % tiles_per_seq) == 0)
        def _():
            carry_ref[f] = st_ref[...]

    tc = FFN_SUBTILE if tf % FFN_SUBTILE == 0 else tf
    row = lax.broadcasted_iota(jnp.int32, (tm, tc), 0)
    down = None
    for s in range(tf // tc):
        cols = slice(s * tc, (s + 1) * tc)
        gate = jnp.dot(h, wg_ref[:, cols], preferred_element_type=F32)
        val = jnp.dot(h, wv_ref[:, cols], preferred_element_type=F32)
        prev1 = pltpu.roll(gate, 1, axis=0)
        prev2 = pltpu.roll(gate, 2, axis=0)
        if per_seq_rows:
            nseq = tm // per_seq_rows
            expand = lambda a: jnp.broadcast_to(a, (nseq, per_seq_rows, tc)).reshape(tm, tc)
            s0, s1 = expand(st_ref[:, 0:1, cols]), expand(st_ref[:, 1:2, cols])
            t = row & (per_seq_rows - 1)
            buf_ref[:, :, cols] = gate.reshape(nseq, per_seq_rows, tc)[:, per_seq_rows - 2:, :]
        else:
            s0, s1 = carry_ref[f, 0:1, cols], carry_ref[f, 1:2, cols]
            t = row
            tail = gate[tm - 2:tm, :]
            carry_ref[f, :, cols] = tail
            buf_ref[:, cols] = tail
        x1 = jnp.where(t >= 1, prev1, s1)
        x2 = jnp.where(t >= 2, prev2, jnp.where(t == 1, s1, s0))
        conv = cw_ref[2:3, cols] * gate + cw_ref[1:2, cols] * x1 + cw_ref[0:1, cols] * x2
        act = (conv * _sigmoid(conv) * val).astype(BF16)
        part = jnp.dot(act, wd_ref[cols, :], preferred_element_type=F32)
        down = part if down is None else down + part

    @pl.when(f == 0)
    def _():
        o_ref[...] = x_ref[...] + down

    @pl.when(f > 0)
    def _():
        o_ref[...] += down


def _ffn(x2d, gain, wg, wv, wd, conv_w, state, seq):
    m, d = x2d.shape
    d_ff = wg.shape[1]
    nseq = m // seq
    assert conv_w.shape[0] == 3
    tf = _tile(d_ff, 512)
    nf = d_ff // tf
    if seq <= SUBLANES:
        assert seq == SUBLANES
        tm = _tile(m, 512, SUBLANES)
        per_seq_rows, tiles_per_seq = seq, 1
        st_spec = buf_spec = pl.BlockSpec((tm // seq, 2, tf), lambda i, f: (i, 0, f))
        n_buf = nseq
    else:
        tm = _tile(seq, 512, SUBLANES)
        per_seq_rows, tiles_per_seq = 0, seq // tm
        st_spec = pl.BlockSpec((None, 2, tf), lambda i, f: (i // tiles_per_seq, 0, f))
        buf_spec = pl.BlockSpec((None, 2, tf), lambda i, f: (i, 0, f))
        n_buf = m // tm
    y, buf = pl.pallas_call(
        functools.partial(_ffn_kernel, tiles_per_seq=tiles_per_seq, per_seq_rows=per_seq_rows),
        grid=(m // tm, nf),
        in_specs=[
            pl.BlockSpec((tm, d), lambda i, f: (i, 0)),
            pl.BlockSpec((1, d), lambda i, f: (0, 0)),
            pl.BlockSpec((d, tf), lambda i, f: (0, f)),
            pl.BlockSpec((d, tf), lambda i, f: (0, f)),
            pl.BlockSpec((tf, d), lambda i, f: (f, 0)),
            pl.BlockSpec((3, tf), lambda i, f: (0, f)),
            st_spec,
        ],
        out_specs=[pl.BlockSpec((tm, d), lambda i, f: (i, 0)), buf_spec],
        out_shape=[jax.ShapeDtypeStruct((m, d), F32), jax.ShapeDtypeStruct((n_buf, 2, d_ff), F32)],
        scratch_shapes=[pltpu.VMEM((tm, d), BF16), pltpu.VMEM((nf, 2, tf), F32)],
        compiler_params=_params("arbitrary", "arbitrary"),
        name="convffn",
    )(x2d, gain, wg, wv, wd, conv_w, state)
    if not per_seq_rows:
        buf = buf.reshape(nseq, tiles_per_seq, 2, d_ff)[:, tiles_per_seq - 1]
    return y, buf


def _layer_weights(l, w_in, sb_q_norm, sb_k_norm, ssm, dn_a_log, dn_dt_bias, ssm_w_glu, w_out, ffn_w_in, ffn_w_out,
                   dims):
    sbw, ssmw, dnw, nh = dims["sbw"], dims["ssmw"], dims["dnw"], dims["dn_heads"]
    heads, hd = dims["sb_heads"], dims["sb_hd"]
    d = w_in.shape[1]
    o_u = 3 * sbw
    o_qkv = o_u + ssmw
    o_ba = o_qkv + 3 * dnw
    o_gate = o_ba + 2 * nh
    n_rest = -(-(ssmw + 4 * dnw + LANES) // sbw) * sbw
    pad = n_rest - (ssmw + 4 * dnw + 2 * nh)
    w = w_in[l]
    w_perm = jnp.concatenate([
        w[:, :o_u], w[:, o_u:o_qkv], w[:, o_gate:o_gate + dnw], w[:, o_qkv:o_ba], w[:, o_ba:o_gate],
        jnp.zeros((d, pad), w.dtype)], axis=1).astype(BF16)
    head_of = jnp.arange(sbw) // hd
    gs = (head_of[:, None] == jnp.arange(LANES)[None, :]).astype(BF16)
    qg = jnp.tile(sb_q_norm[l], heads)[None, :]
    kg = jnp.tile(sb_k_norm[l], heads)[None, :]
    a_re, a_im, bb_re, bb_im = ssm["disc"]
    g, n, ch = dims["ssm_groups"], dims["ssm_state"], dims["ssm_ch"]
    ns = g * n
    grp_of_state = jnp.arange(ns) // n
    grp_of_chan = jnp.arange(ssmw) // ch
    blockmask = grp_of_chan[:, None] == grp_of_state[None, :]
    expand_b = lambda bb: jnp.where(blockmask, jnp.tile(bb.T, (g, 1)), 0.0)
    bmat = jnp.concatenate([expand_b(bb_re), expand_b(bb_im)], axis=1).astype(BF16)
    arow = jnp.concatenate([a_re, a_im], axis=0).reshape(1, 2 * ns)
    expand_c = lambda c: jnp.where(blockmask.T, jnp.tile(c.transpose(0, 2, 1).reshape(ns, ch), (1, g)), 0.0)
    cmat = jnp.concatenate([expand_c(ssm["c_re"][l]), -expand_c(ssm["c_im"][l])], axis=0).astype(BF16)
    prm = jnp.zeros((SUBLANES, LANES), F32)
    prm = prm.at[0, nh:2 * nh].set(dn_a_log[l]).at[1, nh:2 * nh].set(dn_dt_bias[l])
    d_ff = ffn_w_in.shape[2] // 2
    return dict(
        w_perm=w_perm, gs=gs, e=gs.T, qg=qg, kg=kg, bmat=bmat, arow=arow, cmat=cmat, prm=prm,
        wglu=ssm_w_glu[l].astype(BF16), w_out=w_out[l].astype(BF16),
        wg=ffn_w_in[l][:, :d_ff].astype(BF16), wv=ffn_w_in[l][:, d_ff:].astype(BF16),
        wd=ffn_w_out[l].astype(BF16), n_rest=n_rest)


def kernel(x_prompt, x_sample, cache_k, cache_v, page_table, state_ssm_re, state_ssm_im, state_delta, state_delta_conv, state_ffn_conv, norm_mix, w_in, sb_q_norm, sb_k_norm, sb_logit_bias, ssm_lambda_re, ssm_lambda_im, ssm_log_step, ssm_b_re, ssm_b_im, ssm_c_re, ssm_c_im, ssm_d, ssm_w_glu, ssm_b_glu, dn_conv_w, dn_a_log, dn_dt_bias, dn_norm, norm_out_sb, norm_out_ssm, w_out, norm_ffn, ffn_w_in, ffn_conv_w, ffn_w_out):
    depth = w_in.shape[0]
    bp, tp, d = x_prompt.shape
    bd, td, _ = x_sample.shape
    sb_heads, sb_hd = cache_k.shape[-2], cache_k.shape[-1]
    g, n = ssm_lambda_re.shape[1:]
    ch = ssm_b_re.shape[-1]
    dn_heads, dn_hd = dn_a_log.shape[1], dn_norm.shape[1]
    dims = dict(sbw=sb_heads * sb_hd, sb_heads=sb_heads, sb_hd=sb_hd, ssmw=g * ch, ssm_groups=g, ssm_state=n,
                ssm_ch=ch, dnw=dn_heads * dn_hd, dn_heads=dn_heads)
    sbw, ssmw, dnw, ns = dims["sbw"], dims["ssmw"], dims["dnw"], g * n
    assert ssmw % dnw == 0 and dnw % LANES == 0 and sbw % LANES == 0 and LANES % sb_hd == 0
    d_ff = ffn_conv_w.shape[2]
    col0 = ssmw // dnw
    q_scale = sb_hd ** -0.5

    groups = {
        "p": dict(x=x_prompt.reshape(bp * tp, d), b=bp, t=tp, kbuf=None, vbuf=None, outs=[]),
        "s": dict(x=x_sample.reshape(bd * td, d), b=bd, t=td, kbuf=None, vbuf=None, outs=[]),
    }
    for l in range(depth):
        disc = _s5_discretize(ssm_lambda_re[l], ssm_lambda_im[l], ssm_log_step[l], ssm_b_re[l], ssm_b_im[l])
        lw = _layer_weights(l, w_in, sb_q_norm, sb_k_norm, dict(disc=disc, c_re=ssm_c_re, c_im=ssm_c_im),
                            dn_a_log, dn_dt_bias, ssm_w_glu, w_out, ffn_w_in, ffn_w_out, dims)
        for name, grp in groups.items():
            b, t, x = grp["b"], grp["t"], grp["x"]
            q, kbuf, vbuf, rest = _inproj(x, norm_mix[l][None, :], lw["w_perm"], lw["gs"], lw["e"], lw["qg"],
                                          lw["kg"], grp["kbuf"], grp["vbuf"], l, depth, sb_hd, q_scale, t,
                                          time_minor=(name == "p"))
            grp["kbuf"], grp["vbuf"] = kbuf, vbuf
            rest3 = rest.reshape(b, t, lw["n_rest"])
            if name == "p":
                o_a = _sb_prompt(q, kbuf, vbuf, sb_logit_bias[l], l, b, t, sb_hd)
                h0 = jnp.zeros((b, 2 * ns), F32)
                dn_s0 = jnp.zeros((b, dn_heads, dn_hd, dn_hd), F32)
                dn_c0 = jnp.zeros((b, dn_conv_w.shape[1] - 1, 3 * dnw), F32)
                ffn_c0 = jnp.zeros((b, ffn_conv_w.shape[1] - 1, d_ff), F32)
            else:
                o_a = _sb_sample(q, kbuf, vbuf, cache_k, cache_v, page_table, sb_logit_bias[l], l, b, t, sb_hd)
                h0 = jnp.concatenate([state_ssm_re[l].reshape(b, ns), state_ssm_im[l].reshape(b, ns)], axis=1)
                dn_s0, dn_c0, ffn_c0 = state_delta[l], state_delta_conv[l], state_ffn_conv[l]
            u_tm = rest3[:, :, :ssmw].transpose(1, 0, 2)
            y_tm, h_fin = _s5(u_tm, lw["bmat"], lw["arow"], h0, lw["cmat"], ssm_d[l][None, :], lw["wglu"],
                              ssm_b_glu[l][None, :])
            o_b = y_tm.transpose(1, 0, 2).reshape(b * t, ssmw)
            o_c, dn_s, dn_c = _deltanet(rest3, col0, dn_conv_w[l], lw["prm"], dn_norm[l][None, :], dn_c0, dn_s0,
                                        dn_heads, dn_hd)
            x = _outproj(o_a, o_b, o_c.reshape(b * t, dnw), norm_out_sb[l][None, :], norm_out_ssm[l][None, :],
                         lw["w_out"], x)
            x, ffn_c = _ffn(x, norm_ffn[l][None, :], lw["wg"], lw["wv"], lw["wd"], ffn_conv_w[l], ffn_c0, t)
            grp["x"] = x
            grp["outs"].append((h_fin[:, :ns].reshape(b, g, n), h_fin[:, ns:].reshape(b, g, n), dn_s, dn_c, ffn_c))

    res = {}
    for name, grp in groups.items():
        b, t = grp["b"], grp["t"]
        sre, sim, dn_s, dn_c, ffn_c = [jnp.stack(z) for z in zip(*grp["outs"])]
        if name == "p":
            heads_last = lambda a: a.reshape(depth, b, sb_heads, sb_hd, t).transpose(0, 1, 4, 2, 3)
        else:
            heads_last = lambda a: a.reshape(depth, b, t, sb_heads, sb_hd)
        res[name] = dict(
            y=grp["x"].reshape(b, t, d), k=heads_last(grp["kbuf"]), v=heads_last(grp["vbuf"]),
            sre=sre, sim=sim, dn_s=dn_s, dn_c=dn_c, ffn_c=ffn_c)
    p, s = res["p"], res["s"]
    return (p["y"], s["y"], p["k"], p["v"], s["k"], s["v"], p["sre"], p["sim"], s["sre"], s["sim"],
            p["dn_s"], s["dn_s"], p["dn_c"], s["dn_c"], p["ffn_c"], s["ffn_c"])
```

```python
import functools
import math

import jax
import jax.numpy as jnp
from jax import lax
from jax.experimental import pallas as pl
from jax.experimental.pallas import tpu as pltpu

F32 = jnp.float32
BF16 = jnp.bfloat16
HIGHEST = lax.Precision.HIGHEST
NORM_EPS = 1e-6
LANES = 128
SUBLANES = 8
BF16_ROWS = 16
VMEM_LIMIT_BYTES = 56 * 1024 * 1024
DN_CHUNK = 64
TRI_BASE = 16

_NT = (((1,), (1,)), ((), ()))
_TN = (((0,), (0,)), ((), ()))


def _tile(dim, pref, mult=LANES):
    if dim <= pref:
        return dim
    t = (pref // mult) * mult
    while t >= mult:
        if dim % t == 0:
            return t
        t -= mult
    raise ValueError(f"no tile for {dim} (pref {pref}, mult {mult})")


def _params(*sem):
    return pltpu.CompilerParams(dimension_semantics=sem, vmem_limit_bytes=VMEM_LIMIT_BYTES)


def _mm(a, b, dims=None):
    ok = all(d % BF16_ROWS == 0 for d in a.shape + b.shape)
    if ok:
        a, b = a.astype(BF16), b.astype(BF16)
    else:
        a, b = a.astype(F32), b.astype(F32)
    if dims is None:
        return jnp.dot(a, b, preferred_element_type=F32)
    return lax.dot_general(a, b, dims, preferred_element_type=F32)


def _mm_f32(a, b, dims=None):
    if dims is None:
        return jnp.dot(a, b, preferred_element_type=F32, precision=HIGHEST)
    return lax.dot_general(a, b, dims, preferred_element_type=F32, precision=HIGHEST)


def _softplus(z):
    return jnp.maximum(z, 0.0) + jnp.log1p(jnp.exp(-jnp.abs(z)))


def _sigmoid(z):
    return 1.0 / (1.0 + jnp.exp(-z))


def _rms(x, gain, eps=NORM_EPS):
    return x * lax.rsqrt(jnp.mean(x * x, axis=-1, keepdims=True) + eps) * gain


def _inproj_kernel(x_ref, g_ref, w_ref, gs_ref, e_ref, qg_ref, kg_ref, *refs, head_dim, q_scale, aliased,
                   time_minor):
    q_ref, k_ref, v_ref, r_ref, xn_ref = refs[2:] if aliased else refs
    j = pl.program_id(1)
    kv_layout = (lambda a: a.T) if time_minor else (lambda a: a)

    @pl.when(j == 0)
    def _():
        xn_ref[...] = _rms(x_ref[...], g_ref[...]).astype(BF16)

    y = jnp.dot(xn_ref[...], w_ref[...], preferred_element_type=F32)

    def head_norm(gain):
        ms = jnp.dot((y * y).astype(BF16), gs_ref[...], preferred_element_type=F32) * (1.0 / head_dim)
        inv = lax.rsqrt(ms + NORM_EPS)
        hi = inv.astype(BF16)
        lo = (inv - hi.astype(F32)).astype(BF16)
        sc = (jnp.dot(hi, e_ref[...], preferred_element_type=F32)
              + jnp.dot(lo, e_ref[...], preferred_element_type=F32))
        return y * sc * gain

    @pl.when(j == 0)
    def _():
        q_ref[...] = (head_norm(qg_ref[...]) * q_scale).astype(BF16)

    @pl.when(j == 1)
    def _():
        k_ref[...] = kv_layout(head_norm(kg_ref[...]))

    @pl.when(j == 2)
    def _():
        v_ref[...] = kv_layout(y)

    @pl.when(j >= 3)
    def _():
        r_ref[...] = y


def _inproj(x2d, gain, w_perm, gs, e, qg, kg, kbuf, vbuf, layer, depth, head_dim, q_scale, seq, time_minor):
    m, d = x2d.shape
    tn = gs.shape[0]
    n_rest = w_perm.shape[1] - 3 * tn
    tm = _tile(seq, 512, LANES) if time_minor else _tile(m, 512, SUBLANES)
    tiles_per_seq = seq // tm
    aliased = kbuf is not None
    const = lambda i, j: (0, 0)
    in_specs = [
        pl.BlockSpec((tm, d), lambda i, j: (i, 0)),
        pl.BlockSpec((1, d), const),
        pl.BlockSpec((d, tn), lambda i, j: (0, j)),
        pl.BlockSpec((tn, LANES), const),
        pl.BlockSpec((LANES, tn), const),
        pl.BlockSpec((1, tn), const),
        pl.BlockSpec((1, tn), const),
    ]
    args = [x2d, gain, w_perm, gs, e, qg, kg]
    aliases = {}
    if aliased:
        in_specs += [pl.BlockSpec(memory_space=pl.ANY), pl.BlockSpec(memory_space=pl.ANY)]
        args += [kbuf, vbuf]
        aliases = {7: 1, 8: 2}
    if time_minor:
        kv_spec = pl.BlockSpec((None, None, tn, tm), lambda i, j: (layer, i // tiles_per_seq, 0, i % tiles_per_seq))
        kv_shape = jax.ShapeDtypeStruct((depth, m // seq, tn, seq), F32)
    else:
        kv_spec = pl.BlockSpec((None, tm, tn), lambda i, j: (layer, i, 0))
        kv_shape = jax.ShapeDtypeStruct((depth, m, tn), F32)
    out_specs = [
        pl.BlockSpec((tm, tn), lambda i, j: (i, 0)),
        kv_spec,
        kv_spec,
        pl.BlockSpec((tm, tn), lambda i, j: (i, jnp.maximum(j - 3, 0))),
    ]
    out_shape = [jax.ShapeDtypeStruct((m, tn), BF16), kv_shape, kv_shape, jax.ShapeDtypeStruct((m, n_rest), F32)]
    return pl.pallas_call(
        functools.partial(_inproj_kernel, head_dim=head_dim, q_scale=q_scale, aliased=aliased,
                          time_minor=time_minor),
        grid=(m // tm, 3 + n_rest // tn),
        in_specs=in_specs, out_specs=out_specs, out_shape=out_shape,
        scratch_shapes=[pltpu.VMEM((tm, d), BF16)],
        input_output_aliases=aliases,
        compiler_params=_params("parallel", "arbitrary"),
        name="inproj",
    )(*args)


def _softplus_unit(z):
    return jnp.maximum(z, 0.0) + jnp.log(1.0 + jnp.exp(-jnp.abs(z)))


SB_BLOCK = 256
SB_ROW_SUBBLOCKS = 4


def _sb_prompt_kernel(bias_ref, q_ref, k_ref, v_ref, o_ref, *, blk, nsub, head_dim, heads_per_step):
    hp = pl.program_id(1)
    qi = pl.program_id(2)
    row = lax.broadcasted_iota(jnp.int32, (blk, blk), 0)
    col = lax.broadcasted_iota(jnp.int32, (blk, blk), 1)
    later = (row > col).astype(BF16)
    visible = col < row
    q = q_ref[...]
    heads = range(heads_per_step)
    lanes = [slice(hh * head_dim, (hh + 1) * head_dim) for hh in heads]
    qs = [[q[j * blk:(j + 1) * blk, lanes[hh]] for j in range(nsub)] for hh in heads]
    biases = [bias_ref[hp * heads_per_step + hh] for hh in heads]

    def load(ki):
        r0 = pl.multiple_of(ki * blk, blk)
        kb = k_ref[:, pl.ds(r0, blk)].astype(BF16)
        vb = v_ref[:, pl.ds(r0, blk)].astype(BF16)
        return [(kb[lanes[hh], :], vb[lanes[hh], :]) for hh in heads]

    def block(qh, kv, bias, oc, masked):
        o_acc, carry = oc
        z = jnp.dot(qh, kv[0], preferred_element_type=F32) + bias
        sp = _softplus_unit(z)
        if masked:
            sp = jnp.where(visible, sp, 0.0)
        after = jnp.dot(sp.astype(BF16), later, preferred_element_type=F32)
        w = jnp.exp(z - sp - after - carry)
        if masked:
            w = jnp.where(visible, w, 0.0)
        o_acc = o_acc + lax.dot_general(w.astype(BF16), kv[1], _NT, preferred_element_type=F32)
        return o_acc, carry + after[:, 0:1] + sp[:, 0:1]

    zero = (jnp.zeros((blk, head_dim), F32), jnp.zeros((blk, 1), F32))
    state = [[zero for _ in range(nsub)] for _ in heads]
    for d in reversed(range(nsub)):
        kv = load(qi * nsub + d)
        for hh in heads:
            for j in range(d, nsub):
                state[hh][j] = block(qs[hh][j], kv[hh], biases[hh], state[hh][j], j == d)

    def older(it, flat):
        kv = load(qi * nsub - 1 - it)
        return tuple(block(qs[hh][j], kv[hh], biases[hh], flat[hh * nsub + j], False)
                     for hh in heads for j in range(nsub))

    flat = lax.fori_loop(0, qi * nsub, older, tuple(state[hh][j] for hh in heads for j in range(nsub)))
    o_ref[...] = jnp.concatenate(
        [jnp.concatenate([flat[hh * nsub + j][0] for hh in heads], axis=-1) for j in range(nsub)], axis=0)


def _sb_prompt(q, kbuf, vbuf, bias, layer, bsz, seq, head_dim):
    width = q.shape[-1]
    heads_per_step = LANES // head_dim
    blk = _tile(seq, SB_BLOCK)
    nsub = max(n for n in (1, 2, 4, 8) if n <= SB_ROW_SUBBLOCKS and seq % (n * blk) == 0)
    rows = nsub * blk
    q3 = q.reshape(bsz, seq, width)
    return pl.pallas_call(
        functools.partial(_sb_prompt_kernel, blk=blk, nsub=nsub, head_dim=head_dim, heads_per_step=heads_per_step),
        grid=(bsz, width // LANES, seq // rows),
        in_specs=[
            pl.BlockSpec(memory_space=pltpu.SMEM),
            pl.BlockSpec((None, rows, LANES), lambda b, h, i: (b, i, h)),
            pl.BlockSpec((None, None, LANES, seq), lambda b, h, i: (layer, b, h, 0)),
            pl.BlockSpec((None, None, LANES, seq), lambda b, h, i: (layer, b, h, 0)),
        ],
        out_specs=pl.BlockSpec((None, rows, LANES), lambda b, h, i: (b, i, h)),
        out_shape=jax.ShapeDtypeStruct((bsz, seq, width), F32),
        compiler_params=_params("parallel", "parallel", "arbitrary"),
        name="sb_prompt",
    )(bias, q3, kbuf, vbuf).reshape(bsz * seq, width)


SB_PAGE_GROUP = 4


def _sb_sample_kernel(pt_ref, q_ref, bcol_ref, kn_ref, vn_ref, *refs, pages_per_step, tq, heads, head_dim):
    del pt_ref
    kp = refs[:pages_per_step]
    vp = refs[pages_per_step:2 * pages_per_step]
    o_ref, acc_ref, carry_ref = refs[2 * pages_per_step:]
    c = pl.program_id(1)
    nr = heads * tq
    ltq = tq.bit_length() - 1
    page = kp[0].shape[-1]
    qf = q_ref[...].astype(F32)
    width = qf.shape[-1]
    bcol = bcol_ref[...]
    r = lax.broadcasted_iota(jnp.int32, (nr, width), 0)
    col = lax.broadcasted_iota(jnp.int32, (nr, width), 1)
    own_head = lax.shift_right_logical(r, ltq) == lax.shift_right_logical(col, head_dim.bit_length() - 1)
    qbd = jnp.where(own_head, jnp.concatenate([qf] * heads, axis=0), 0.0)

    @pl.when(c == 0)
    def _():
        z = lax.dot_general(qbd, kn_ref[...], _NT, preferred_element_type=F32) + bcol
        key = lax.broadcasted_iota(jnp.int32, (nr, tq), 1)
        qry = lax.broadcasted_iota(jnp.int32, (nr, tq), 0) & (tq - 1)
        vis = key < qry
        sp = jnp.where(vis, _softplus_unit(z), 0.0)
        later = (lax.broadcasted_iota(jnp.int32, (tq, tq), 0)
                 > lax.broadcasted_iota(jnp.int32, (tq, tq), 1)).astype(F32)
        w = jnp.where(vis, jnp.exp(z - sp - _mm_f32(sp, later)), 0.0)
        acc_ref[...] = jnp.dot(w, vn_ref[...], preferred_element_type=F32)
        carry_ref[...] = jnp.sum(sp, axis=-1, keepdims=True)

    qbd16 = qbd.astype(BF16)
    gsz = SB_PAGE_GROUP if pages_per_step % SB_PAGE_GROUP == 0 else 1
    nk = gsz * page
    later = (lax.broadcasted_iota(jnp.int32, (nk, nk), 0)
             > lax.broadcasted_iota(jnp.int32, (nk, nk), 1)).astype(BF16)
    side_by_side = lambda refs: jnp.concatenate(
        [r[...].reshape(width, page).astype(BF16) for r in reversed(refs)], axis=1)
    for g in range(pages_per_step // gsz):
        kt = side_by_side(kp[g * gsz:(g + 1) * gsz])
        z = jnp.dot(qbd16, kt, preferred_element_type=F32) + bcol
        sp = _softplus_unit(z)
        after = jnp.dot(sp.astype(BF16), later, preferred_element_type=F32)
        carry = carry_ref[...]
        w = jnp.exp(z - sp - after - carry).astype(BF16)
        vt = side_by_side(vp[g * gsz:(g + 1) * gsz])
        acc_ref[...] += lax.dot_general(w, vt, _NT, preferred_element_type=F32)
        carry_ref[...] = carry + after[:, 0:1] + sp[:, 0:1]

    @pl.when(c == pl.num_programs(1) - 1)
    def _():
        acc = jnp.where(own_head, acc_ref[...], 0.0)
        o_ref[...] = acc.reshape(heads, tq, width).sum(axis=0)


def _sb_sample(q, kbuf, vbuf, cache_k, cache_v, page_table, bias, layer, bsz, tq, head_dim):
    width = q.shape[-1]
    heads = width // head_dim
    assert tq & (tq - 1) == 0 and head_dim & (head_dim - 1) == 0
    page = cache_k.shape[2]
    npages = page_table.shape[1]
    pps = _tile(npages, 8, 1)
    bcol = jnp.repeat(bias.astype(F32), tq)[:, None]
    cache_k = cache_k.transpose(0, 1, 3, 4, 2)
    cache_v = cache_v.transpose(0, 1, 3, 4, 2)

    def page_map(p):
        return lambda b, c, pt: (layer, pt[b * npages + npages - 1 - (c * pps + p)], 0, 0, 0)

    page_specs = [pl.BlockSpec((None, None, heads, head_dim, page), page_map(p)) for p in range(pps)]
    grid_spec = pltpu.PrefetchScalarGridSpec(
        num_scalar_prefetch=1,
        grid=(bsz, npages // pps),
        in_specs=[
            pl.BlockSpec((None, tq, width), lambda b, c, pt: (b, 0, 0)),
            pl.BlockSpec((heads * tq, 1), lambda b, c, pt: (0, 0)),
            pl.BlockSpec((None, None, tq, width), lambda b, c, pt: (layer, b, 0, 0)),
            pl.BlockSpec((None, None, tq, width), lambda b, c, pt: (layer, b, 0, 0)),
        ] + page_specs + page_specs,
        out_specs=pl.BlockSpec((None, tq, width), lambda b, c, pt: (b, 0, 0)),
        scratch_shapes=[pltpu.VMEM((heads * tq, width), F32), pltpu.VMEM((heads * tq, 1), F32)],
    )
    k4 = kbuf.reshape(kbuf.shape[0], bsz, tq, width)
    v4 = vbuf.reshape(vbuf.shape[0], bsz, tq, width)
    out = pl.pallas_call(
        functools.partial(_sb_sample_kernel, pages_per_step=pps, tq=tq, heads=heads, head_dim=head_dim),
        grid_spec=grid_spec,
        out_shape=jax.ShapeDtypeStruct((bsz, tq, width), F32),
        compiler_params=_params("parallel", "arbitrary"),
        name="sb_sample",
    )(page_table.reshape(-1), q.reshape(bsz, tq, width), bcol, k4, v4,
      *([cache_k] * pps), *([cache_v] * pps))
    return out.reshape(bsz * tq, width)


def _s5_discretize_kernel(lre_ref, lim_ref, ls_ref, bre_ref, bim_ref, are_ref, aim_ref, bbre_ref, bbim_ref):
    lam_re, lam_im = lre_ref[...], lim_ref[...]
    dt = jnp.exp(ls_ref[...])
    mag = jnp.exp(lam_re * dt)
    ab_re = mag * jnp.cos(lam_im * dt)
    ab_im = mag * jnp.sin(lam_im * dt)
    den = lam_re * lam_re + lam_im * lam_im
    xr = ab_re - 1.0
    f_re = (xr * lam_re + ab_im * lam_im) / den
    f_im = (ab_im * lam_re - xr * lam_im) / den
    are_ref[...] = ab_re
    aim_ref[...] = ab_im
    bbre_ref[...] = f_re * bre_ref[...] - f_im * bim_ref[...]
    bbim_ref[...] = f_re * bim_ref[...] + f_im * bre_ref[...]


def _s5_discretize(lam_re, lam_im, log_step, b_re, b_im):
    g, n = lam_re.shape
    ch = b_re.shape[-1]
    col = lambda a: a.reshape(g * n, 1)
    ls = jnp.broadcast_to(log_step[:, None], (g, n))
    return pl.pallas_call(
        _s5_discretize_kernel,
        out_shape=[jax.ShapeDtypeStruct((g * n, 1), F32)] * 2 + [jax.ShapeDtypeStruct((g * n, ch), F32)] * 2,
        name="s5_discretize",
    )(col(lam_re), col(lam_im), col(ls), b_re.reshape(g * n, ch), b_im.reshape(g * n, ch))


def _s5_kernel(u_ref, bmat_ref, arow_ref, h0_ref, cmat_ref, d_ref, wglu_ref, bglu_ref,
               y_ref, hfin_ref, hbuf_ref, hc_ref, *, tt, nb, ns, cb):
    ti = pl.program_id(1)
    width = u_ref.shape[-1]

    @pl.when(ti == 0)
    def _():
        hc_ref[...] = h0_ref[...]

    u = u_ref[...].reshape(tt * nb, width)
    hbuf_ref[...] = jnp.dot(u.astype(BF16), bmat_ref[...], preferred_element_type=F32)

    for c in range(ns // cb):
        re = slice(c * cb, (c + 1) * cb)
        im = slice(ns + c * cb, ns + (c + 1) * cb)
        a_re = jnp.broadcast_to(arow_ref[:, re], (SUBLANES, cb))
        a_im = jnp.broadcast_to(arow_ref[:, im], (SUBLANES, cb))

        def batch_rows(bb, _):
            r0 = pl.multiple_of(bb * SUBLANES, SUBLANES)

            def step(t, h):
                h_re, h_im = h
                r = pl.multiple_of(t * nb + r0, SUBLANES)
                n_re = a_re * h_re - a_im * h_im + hbuf_ref[pl.ds(r, SUBLANES), re]
                n_im = a_re * h_im + a_im * h_re + hbuf_ref[pl.ds(r, SUBLANES), im]
                hbuf_ref[pl.ds(r, SUBLANES), re] = n_re
                hbuf_ref[pl.ds(r, SUBLANES), im] = n_im
                return n_re, n_im

            h = (hc_ref[pl.ds(r0, SUBLANES), re], hc_ref[pl.ds(r0, SUBLANES), im])
            h = lax.fori_loop(0, tt, step, h, unroll=min(tt, 8))
            hc_ref[pl.ds(r0, SUBLANES), re] = h[0]
            hc_ref[pl.ds(r0, SUBLANES), im] = h[1]
            return 0

        lax.fori_loop(0, nb // SUBLANES, batch_rows, 0)

    y = jnp.dot(hbuf_ref[...].astype(BF16), cmat_ref[...], preferred_element_type=F32) + d_ref[...] * u
    y = 0.5 * y * (1.0 + jnp.tanh(math.sqrt(2.0 / math.pi) * (y + 0.044715 * (y * y * y))))
    z = jnp.dot(y.astype(BF16), wglu_ref[...], preferred_element_type=F32) + bglu_ref[...]
    y_ref[...] = (y * _sigmoid(z)).reshape(tt, nb, width)

    @pl.when(ti == pl.num_programs(1) - 1)
    def _():
        hfin_ref[...] = hc_ref[...]


def _s5(u_tm, bmat, arow, h0, cmat, d_row, wglu, bglu):
    t, b, width = u_tm.shape
    ns2 = bmat.shape[1]
    nb = _tile(b, 64, SUBLANES)
    tt = _tile(t, max(SUBLANES, 512 // nb), 1)
    const = lambda i, j: (0, 0)
    return pl.pallas_call(
        functools.partial(_s5_kernel, tt=tt, nb=nb, ns=ns2 // 2, cb=_tile(ns2 // 2, 512)),
        grid=(b // nb, t // tt),
        in_specs=[
            pl.BlockSpec((tt, nb, width), lambda i, j: (j, i, 0)),
            pl.BlockSpec((width, ns2), const),
            pl.BlockSpec((1, ns2), const),
            pl.BlockSpec((nb, ns2), lambda i, j: (i, 0)),
            pl.BlockSpec((ns2, width), const),
            pl.BlockSpec((1, width), const),
            pl.BlockSpec((width, width), const),
            pl.BlockSpec((1, width), const),
        ],
        out_specs=[
            pl.BlockSpec((tt, nb, width), lambda i, j: (j, i, 0)),
            pl.BlockSpec((nb, ns2), lambda i, j: (i, 0)),
        ],
        out_shape=[jax.ShapeDtypeStruct((t, b, width), F32), jax.ShapeDtypeStruct((b, ns2), F32)],
        scratch_shapes=[pltpu.VMEM((tt * nb, ns2), F32), pltpu.VMEM((nb, ns2), F32)],
        compiler_params=_params("parallel", "arbitrary"),
        name="s5",
    )(u_tm, bmat, arow, h0, cmat, d_row, wglu, bglu)


def _split(a):
    hi = a.astype(BF16)
    return hi, (a - hi.astype(F32)).astype(BF16)


def _mm3(a, b):
    d = lambda x, y: jnp.dot(x, y, preferred_element_type=F32)
    return d(a[0], b[0]) + d(a[0], b[1]) + d(a[1], b[0])


def _block_tri_inverse(a, n, c):
    row = lax.broadcasted_iota(jnp.int32, (n, n), 0)
    col = lax.broadcasted_iota(jnp.int32, (n, n), 1)
    eye = (row == col).astype(F32)

    def same_block(b):
        s = b.bit_length() - 1
        return lax.shift_right_logical(row, s) == lax.shift_right_logical(col, s)

    b0 = min(c, TRI_BASE)
    neg = -jnp.where(same_block(b0), a, 0.0)
    t = eye + neg
    p = _split(neg)
    k = 1
    while 2 * k < b0:
        p = _split(_mm3(p, p))
        t = t + _mm3(_split(t), p)
        k *= 2
    b = b0
    while b < c:
        off = jnp.where(jnp.logical_and(same_block(2 * b), jnp.logical_not(same_block(b))), a, 0.0)
        ts = _split(t)
        t = t - _mm3(_split(_mm3(ts, _split(off))), ts)
        b *= 2
    return t


def _dn_kernel(q_ref, k_ref, v_ref, ba_ref, gate_ref, cw_ref, prm_ref, nrm_ref, cbuf_ref, s0_ref,
               o_ref, sout_ref, cout_ref,
               xpad_ref, qs_ref, ks_ref, vs_ref, gb_ref, s_ref, *, bs, tt, chunk, nh, hd, taps):
    ti = pl.program_id(1)
    width = nh * hd
    halo = SUBLANES
    first = halo - (taps - 1)

    @pl.when(ti == 0)
    def _():
        s_ref[...] = s0_ref[...]
        xpad_ref[:, first:halo, :] = cbuf_ref[...]

    xpad_ref[:, halo:halo + tt, 0:width] = q_ref[...]
    xpad_ref[:, halo:halo + tt, width:2 * width] = k_ref[...]
    xpad_ref[:, halo:halo + tt, 2 * width:3 * width] = v_ref[...]
    cw = cw_ref[...]
    y = cw[0:1, :] * xpad_ref[:, first:first + tt, :]
    for i in range(1, taps):
        y = y + cw[i:i + 1, :] * xpad_ref[:, first + i:first + i + tt, :]
    cout_ref[...] = xpad_ref[:, halo + tt - (taps - 1):halo + tt, :]
    xpad_ref[:, 0:halo, :] = xpad_ref[:, tt:tt + halo, :]
    act = y * _sigmoid(y)
    for h in range(nh):
        sl = slice(h * hd, (h + 1) * hd)
        qh = act[:, :, h * hd:(h + 1) * hd]
        kh = act[:, :, width + h * hd:width + (h + 1) * hd]
        qs_ref[:, :, sl] = qh * lax.rsqrt(jnp.sum(qh * qh, axis=-1, keepdims=True) + NORM_EPS) * (hd ** -0.5)
        ks_ref[:, :, sl] = kh * lax.rsqrt(jnp.sum(kh * kh, axis=-1, keepdims=True) + NORM_EPS)
    vs_ref[...] = act[:, :, 2 * width:3 * width]
    ba = ba_ref[...]
    lane = lax.broadcasted_iota(jnp.int32, ba.shape, 2)
    gval = -jnp.exp(prm_ref[0:1, :]) * _softplus(ba + prm_ref[1:2, :])
    gb_ref[...] = jnp.where(lane < nh, _sigmoid(ba), jnp.where(lane < 2 * nh, gval, 0.0))

    pairs = [(s, h) for s in range(bs) for h in range(nh)]
    npairs = len(pairs)
    nrow = npairs * chunk
    lc = chunk.bit_length() - 1
    row = lax.broadcasted_iota(jnp.int32, (nrow, nrow), 0)
    col = lax.broadcasted_iota(jnp.int32, (nrow, nrow), 1)
    same = lax.shift_right_logical(row, lc) == lax.shift_right_logical(col, lc)
    lower = jnp.logical_and(same, row >= col)
    strict = jnp.logical_and(same, row > col)
    tri = lower.astype(BF16)
    last_sel = col == (row | (chunk - 1))
    prow = lax.shift_right_logical(lax.broadcasted_iota(jnp.int32, (nrow, LANES), 0), lc)
    plane = lax.broadcasted_iota(jnp.int32, (nrow, LANES), 1)
    head_of_row = prow & (nh - 1)
    is_beta = plane == head_of_row
    is_g = plane == head_of_row + nh
    ones8 = jnp.ones((BF16_ROWS, LANES), BF16)
    pair_of_row = lax.shift_right_logical(lax.broadcasted_iota(jnp.int32, (nrow, hd), 0), lc)

    def expand(x):
        return jnp.concatenate([jnp.where(pair_of_row == p, x, 0.0) for p in range(npairs)], axis=1)

    def one_chunk(c):
        rows = pl.ds(c * chunk, chunk)
        stack = lambda ref: jnp.concatenate([ref[s, rows, h * hd:(h + 1) * hd] for s, h in pairs], axis=0)
        qst, kst, vst = stack(qs_ref), stack(ks_ref), stack(vs_ref)
        gbst = jnp.concatenate([gb_ref[s, rows, :] for s, _ in pairs], axis=0)
        g_hi, g_lo = _split(gbst)
        gcum_all = (jnp.dot(tri, g_hi, preferred_element_type=F32)
                    + jnp.dot(tri, g_lo, preferred_element_type=F32))
        gsel = jnp.where(is_g, gcum_all, 0.0)
        gc = jnp.sum(gsel, axis=-1, keepdims=True)
        beta = jnp.sum(jnp.where(is_beta, gbst, 0.0), axis=-1, keepdims=True)
        s_hi, s_lo = _split(gsel)
        grow = (lax.dot_general(ones8, s_hi, _NT, preferred_element_type=F32)
                + lax.dot_general(ones8, s_lo, _NT, preferred_element_type=F32))[0:1, :]
        decay = jnp.where(lower, jnp.exp(jnp.minimum(gc - grow, 0.0)), 0.0)
        kb = kst * beta
        a = _mm(kb, kst, _NT) * jnp.where(strict, decay, 0.0)
        tinv = _block_tri_inverse(a, nrow, chunk)
        u = _mm(tinv, vst * beta)
        w = _mm(tinv, kb * jnp.exp(gc))
        attn = _mm(qst, kst, _NT) * decay
        s_stack = jnp.concatenate([s_ref[s, h] for s, h in pairs], axis=0)
        v_new = u - _mm(expand(w), s_stack)
        o = _mm(expand(qst * jnp.exp(gc)), s_stack) + _mm(attn, v_new)
        g_last = jnp.sum(jnp.where(last_sel, grow, 0.0), axis=-1, keepdims=True)
        upd = _mm(expand(kst * jnp.exp(g_last - gc)), v_new, _TN)
        gt = stack(gate_ref)
        on = _rms(o, nrm_ref[...]) * (gt * _sigmoid(gt))
        for idx, (s, h) in enumerate(pairs):
            gl = g_last[idx * chunk:idx * chunk + 1, :]
            s_ref[s, h] = s_ref[s, h] * jnp.exp(gl) + upd[idx * hd:(idx + 1) * hd, :]
            o_ref[s, rows, h * hd:(h + 1) * hd] = on[idx * chunk:(idx + 1) * chunk, :]

    for c in range(tt // chunk):
        one_chunk(c)

    @pl.when(ti == pl.num_programs(1) - 1)
    def _():
        sout_ref[...] = s_ref[...]


def _deltanet(rest3, col0, conv_w, prm, nrm, cbuf, s0, nh, hd):
    nseq, t, _ = rest3.shape
    width = nh * hd
    taps = conv_w.shape[0]
    chunk = min(DN_CHUNK, t)
    assert t % chunk == 0 and chunk & (chunk - 1) == 0 and t >= taps - 1
    tt = _tile(t, 256, chunk)
    bs = _tile(nseq, 8, 1) if tt == t and t <= SUBLANES else 1
    ba_col = (col0 + 4) * (width // LANES)
    blk = lambda cidx: pl.BlockSpec((bs, tt, width), lambda i, j: (i, j, cidx))
    const = lambda i, j: (0, 0)
    return pl.pallas_call(
        functools.partial(_dn_kernel, bs=bs, tt=tt, chunk=chunk, nh=nh, hd=hd, taps=taps),
        grid=(nseq // bs, t // tt),
        in_specs=[
            blk(col0 + 1), blk(col0 + 2), blk(col0 + 3),
            pl.BlockSpec((bs, tt, LANES), lambda i, j: (i, j, ba_col)),
            blk(col0),
            pl.BlockSpec((taps, 3 * width), const),
            pl.BlockSpec((SUBLANES, LANES), const),
            pl.BlockSpec((1, hd), const),
            pl.BlockSpec((bs, taps - 1, 3 * width), lambda i, j: (i, 0, 0)),
            pl.BlockSpec((bs, nh, hd, hd), lambda i, j: (i, 0, 0, 0)),
        ],
        out_specs=[
            pl.BlockSpec((bs, tt, width), lambda i, j: (i, j, 0)),
            pl.BlockSpec((bs, nh, hd, hd), lambda i, j: (i, 0, 0, 0)),
            pl.BlockSpec((bs, taps - 1, 3 * width), lambda i, j: (i, 0, 0)),
        ],
        out_shape=[
            jax.ShapeDtypeStruct((nseq, t, width), F32),
            jax.ShapeDtypeStruct((nseq, nh, hd, hd), F32),
            jax.ShapeDtypeStruct((nseq, taps - 1, 3 * width), F32),
        ],
        scratch_shapes=[
            pltpu.VMEM((bs, tt + 2 * SUBLANES, 3 * width), F32),
            pltpu.VMEM((bs, tt, width), F32),
            pltpu.VMEM((bs, tt, width), F32),
            pltpu.VMEM((bs, tt, width), F32),
            pltpu.VMEM((bs, tt, LANES), F32),
            pltpu.VMEM((bs, nh, hd, hd), F32),
        ],
        compiler_params=_params("parallel", "arbitrary"),
        name="deltanet",
    )(rest3, rest3, rest3, rest3, rest3, conv_w, prm, nrm, cbuf, s0)


def _outproj_kernel(oa_ref, ob_ref, oc_ref, ga_ref, gb_ref, w_ref, x_ref, o_ref, mix_ref):
    wa, wb = oa_ref.shape[-1], ob_ref.shape[-1]

    @pl.when(pl.program_id(1) == 0)
    def _():
        mix_ref[:, 0:wa] = _rms(oa_ref[...], ga_ref[...]).astype(BF16)
        mix_ref[:, wa:wa + wb] = _rms(ob_ref[...], gb_ref[...]).astype(BF16)
        mix_ref[:, wa + wb:] = oc_ref[...].astype(BF16)

    o_ref[...] = x_ref[...] + jnp.dot(mix_ref[...], w_ref[...], preferred_element_type=F32)


def _outproj(oa, ob, oc, ga, gb, w, x2d):
    m, d = x2d.shape
    tm = _tile(m, 512, SUBLANES)
    tn = _tile(d, 1024)
    row = lambda a: pl.BlockSpec((tm, a.shape[-1]), lambda i, j: (i, 0))
    const = lambda a: pl.BlockSpec((1, a.shape[-1]), lambda i, j: (0, 0))
    return pl.pallas_call(
        _outproj_kernel,
        grid=(m // tm, d // tn),
        in_specs=[row(oa), row(ob), row(oc), const(ga), const(gb),
                  pl.BlockSpec((w.shape[0], tn), lambda i, j: (0, j)),
                  pl.BlockSpec((tm, tn), lambda i, j: (i, j))],
        out_specs=pl.BlockSpec((tm, tn), lambda i, j: (i, j)),
        out_shape=jax.ShapeDtypeStruct((m, d), F32),
        scratch_shapes=[pltpu.VMEM((tm, w.shape[0]), BF16)],
        compiler_params=_params("parallel", "arbitrary"),
        name="outproj",
    )(oa, ob, oc, ga, gb, w, x2d)


FFN_SUBTILE = 256


def _ffn_kernel(x_ref, g_ref, wg_ref, wv_ref, wd_ref, cw_ref, st_ref, o_ref, buf_ref, h_ref, act_ref, carry_ref,
                *, nf, tiles_per_seq, per_seq_rows):
    i = pl.program_id(0)
    f = pl.program_id(1)
    tm, tf = act_ref.shape

    def up():
        h = h_ref[...]
        if not per_seq_rows:
            @pl.when((i % tiles_per_seq) == 0)
            def _():
                carry_ref[f] = st_ref[...]

        tc = FFN_SUBTILE if tf % FFN_SUBTILE == 0 else tf
        row = lax.broadcasted_iota(jnp.int32, (tm, tc), 0)
        for s in range(tf // tc):
            cols = slice(s * tc, (s + 1) * tc)
            gate = jnp.dot(h, wg_ref[:, cols], preferred_element_type=F32)
            val = jnp.dot(h, wv_ref[:, cols], preferred_element_type=F32)
            prev1 = pltpu.roll(gate, 1, axis=0)
            prev2 = pltpu.roll(gate, 2, axis=0)
            if per_seq_rows:
                nseq = tm // per_seq_rows
                expand = lambda a: jnp.broadcast_to(a, (nseq, per_seq_rows, tc)).reshape(tm, tc)
                s0, s1 = expand(st_ref[:, 0:1, cols]), expand(st_ref[:, 1:2, cols])
                t = row & (per_seq_rows - 1)
                buf_ref[:, :, cols] = gate.reshape(nseq, per_seq_rows, tc)[:, per_seq_rows - 2:, :]
            else:
                s0, s1 = carry_ref[f, 0:1, cols], carry_ref[f, 1:2, cols]
                t = row
                tail = gate[tm - 2:tm, :]
                carry_ref[f, :, cols] = tail
                buf_ref[:, cols] = tail
            x1 = jnp.where(t >= 1, prev1, s1)
            x2 = jnp.where(t >= 2, prev2, jnp.where(t == 1, s1, s0))
            conv = cw_ref[2:3, cols] * gate + cw_ref[1:2, cols] * x1 + cw_ref[0:1, cols] * x2
            act_ref[:, cols] = (conv * _sigmoid(conv) * val).astype(BF16)

    def down():
        o_ref[...] += jnp.dot(act_ref[...], wd_ref[...], preferred_element_type=F32)

    @pl.when(f == 0)
    def _():
        h_ref[...] = _rms(x_ref[...], g_ref[...]).astype(BF16)
        o_ref[...] = x_ref[...]
        up()

    @pl.when(jnp.logical_and(f > 0, f < nf))
    def _():
        down()
        up()

    @pl.when(f == nf)
    def _():
        down()


def _ffn(x2d, gain, wg, wv, wd, conv_w, state, seq):
    m, d = x2d.shape
    d_ff = wg.shape[1]
    nseq = m // seq
    assert conv_w.shape[0] == 3
    tf = _tile(d_ff, 512)
    nf = d_ff // tf
    up_tile = lambda f: jnp.minimum(f, nf - 1)
    down_tile = lambda f: jnp.maximum(f - 1, 0)
    if seq <= SUBLANES:
        assert seq == SUBLANES
        tm = _tile(m, 512, SUBLANES)
        per_seq_rows, tiles_per_seq = seq, 1
        st_spec = buf_spec = pl.BlockSpec((tm // seq, 2, tf), lambda i, f: (i, 0, up_tile(f)))
        n_buf = nseq
    else:
        tm = _tile(seq, 512, SUBLANES)
        per_seq_rows, tiles_per_seq = 0, seq // tm
        st_spec = pl.BlockSpec((None, 2, tf), lambda i, f: (i // tiles_per_seq, 0, up_tile(f)))
        buf_spec = pl.BlockSpec((None, 2, tf), lambda i, f: (i, 0, up_tile(f)))
        n_buf = m // tm
    y, buf = pl.pallas_call(
        functools.partial(_ffn_kernel, nf=nf, tiles_per_seq=tiles_per_seq, per_seq_rows=per_seq_rows),
        grid=(m // tm, nf + 1),
        in_specs=[
            pl.BlockSpec((tm, d), lambda i, f: (i, 0)),
            pl.BlockSpec((1, d), lambda i, f: (0, 0)),
            pl.BlockSpec((d, tf), lambda i, f: (0, up_tile(f))),
            pl.BlockSpec((d, tf), lambda i, f: (0, up_tile(f))),
            pl.BlockSpec((tf, d), lambda i, f: (down_tile(f), 0)),
            pl.BlockSpec((3, tf), lambda i, f: (0, up_tile(f))),
            st_spec,
        ],
        out_specs=[pl.BlockSpec((tm, d), lambda i, f: (i, 0)), buf_spec],
        out_shape=[jax.ShapeDtypeStruct((m, d), F32), jax.ShapeDtypeStruct((n_buf, 2, d_ff), F32)],
        scratch_shapes=[pltpu.VMEM((tm, d), BF16), pltpu.VMEM((tm, tf), BF16), pltpu.VMEM((nf, 2, tf), F32)],
        compiler_params=_params("arbitrary", "arbitrary"),
        name="convffn",
    )(x2d, gain, wg, wv, wd, conv_w, state)
    if not per_seq_rows:
        buf = buf.reshape(nseq, tiles_per_seq, 2, d_ff)[:, tiles_per_seq - 1]
    return y, buf


def _layer_weights(l, w_in, sb_q_norm, sb_k_norm, ssm, dn_a_log, dn_dt_bias, ssm_w_glu, w_out, ffn_w_in, ffn_w_out,
                   dims):
    sbw, ssmw, dnw, nh = dims["sbw"], dims["ssmw"], dims["dnw"], dims["dn_heads"]
    heads, hd = dims["sb_heads"], dims["sb_hd"]
    d = w_in.shape[1]
    o_u = 3 * sbw
    o_qkv = o_u + ssmw
    o_ba = o_qkv + 3 * dnw
    o_gate = o_ba + 2 * nh
    n_rest = -(-(ssmw + 4 * dnw + LANES) // sbw) * sbw
    pad = n_rest - (ssmw + 4 * dnw + 2 * nh)
    w = w_in[l]
    w_perm = jnp.concatenate([
        w[:, :o_u], w[:, o_u:o_qkv], w[:, o_gate:o_gate + dnw], w[:, o_qkv:o_ba], w[:, o_ba:o_gate],
        jnp.zeros((d, pad), w.dtype)], axis=1).astype(BF16)
    head_of = jnp.arange(sbw) // hd
    gs = (head_of[:, None] == jnp.arange(LANES)[None, :]).astype(BF16)
    qg = jnp.tile(sb_q_norm[l], heads)[None, :]
    kg = jnp.tile(sb_k_norm[l], heads)[None, :]
    a_re, a_im, bb_re, bb_im = ssm["disc"]
    g, n, ch = dims["ssm_groups"], dims["ssm_state"], dims["ssm_ch"]
    ns = g * n
    grp_of_state = jnp.arange(ns) // n
    grp_of_chan = jnp.arange(ssmw) // ch
    blockmask = grp_of_chan[:, None] == grp_of_state[None, :]
    expand_b = lambda bb: jnp.where(blockmask, jnp.tile(bb.T, (g, 1)), 0.0)
    bmat = jnp.concatenate([expand_b(bb_re), expand_b(bb_im)], axis=1).astype(BF16)
    arow = jnp.concatenate([a_re, a_im], axis=0).reshape(1, 2 * ns)
    expand_c = lambda c: jnp.where(blockmask.T, jnp.tile(c.transpose(0, 2, 1).reshape(ns, ch), (1, g)), 0.0)
    cmat = jnp.concatenate([expand_c(ssm["c_re"][l]), -expand_c(ssm["c_im"][l])], axis=0).astype(BF16)
    prm = jnp.zeros((SUBLANES, LANES), F32)
    prm = prm.at[0, nh:2 * nh].set(dn_a_log[l]).at[1, nh:2 * nh].set(dn_dt_bias[l])
    d_ff = ffn_w_in.shape[2] // 2
    return dict(
        w_perm=w_perm, gs=gs, e=gs.T, qg=qg, kg=kg, bmat=bmat, arow=arow, cmat=cmat, prm=prm,
        wglu=ssm_w_glu[l].astype(BF16), w_out=w_out[l].astype(BF16),
        wg=ffn_w_in[l][:, :d_ff].astype(BF16), wv=ffn_w_in[l][:, d_ff:].astype(BF16),
        wd=ffn_w_out[l].astype(BF16), n_rest=n_rest)


def kernel(x_prompt, x_sample, cache_k, cache_v, page_table, state_ssm_re, state_ssm_im, state_delta, state_delta_conv, state_ffn_conv, norm_mix, w_in, sb_q_norm, sb_k_norm, sb_logit_bias, ssm_lambda_re, ssm_lambda_im, ssm_log_step, ssm_b_re, ssm_b_im, ssm_c_re, ssm_c_im, ssm_d, ssm_w_glu, ssm_b_glu, dn_conv_w, dn_a_log, dn_dt_bias, dn_norm, norm_out_sb, norm_out_ssm, w_out, norm_ffn, ffn_w_in, ffn_conv_w, ffn_w_out):
    depth = w_in.shape[0]
    bp, tp, d = x_prompt.shape
    bd, td, _ = x_sample.shape
    sb_heads, sb_hd = cache_k.shape[-2], cache_k.shape[-1]
    g, n = ssm_lambda_re.shape[1:]
    ch = ssm_b_re.shape[-1]
    dn_heads, dn_hd = dn_a_log.shape[1], dn_norm.shape[1]
    dims = dict(sbw=sb_heads * sb_hd, sb_heads=sb_heads, sb_hd=sb_hd, ssmw=g * ch, ssm_groups=g, ssm_state=n,
                ssm_ch=ch, dnw=dn_heads * dn_hd, dn_heads=dn_heads)
    sbw, ssmw, dnw, ns = dims["sbw"], dims["ssmw"], dims["dnw"], g * n
    assert ssmw % dnw == 0 and dnw % LANES == 0 and sbw % LANES == 0 and LANES % sb_hd == 0
    d_ff = ffn_conv_w.shape[2]
    col0 = ssmw // dnw
    q_scale = sb_hd ** -0.5

    groups = {
        "p": dict(x=x_prompt.reshape(bp * tp, d), b=bp, t=tp, kbuf=None, vbuf=None, outs=[]),
        "s": dict(x=x_sample.reshape(bd * td, d), b=bd, t=td, kbuf=None, vbuf=None, outs=[]),
    }
    for l in range(depth):
        disc = _s5_discretize(ssm_lambda_re[l], ssm_lambda_im[l], ssm_log_step[l], ssm_b_re[l], ssm_b_im[l])
        lw = _layer_weights(l, w_in, sb_q_norm, sb_k_norm, dict(disc=disc, c_re=ssm_c_re, c_im=ssm_c_im),
                            dn_a_log, dn_dt_bias, ssm_w_glu, w_out, ffn_w_in, ffn_w_out, dims)
        for name, grp in groups.items():
            b, t, x = grp["b"], grp["t"], grp["x"]
            q, kbuf, vbuf, rest = _inproj(x, norm_mix[l][None, :], lw["w_perm"], lw["gs"], lw["e"], lw["qg"],
                                          lw["kg"], grp["kbuf"], grp["vbuf"], l, depth, sb_hd, q_scale, t,
                                          time_minor=(name == "p"))
            grp["kbuf"], grp["vbuf"] = kbuf, vbuf
            rest3 = rest.reshape(b, t, lw["n_rest"])
            if name == "p":
                o_a = _sb_prompt(q, kbuf, vbuf, sb_logit_bias[l], l, b, t, sb_hd)
                h0 = jnp.zeros((b, 2 * ns), F32)
                dn_s0 = jnp.zeros((b, dn_heads, dn_hd, dn_hd), F32)
                dn_c0 = jnp.zeros((b, dn_conv_w.shape[1] - 1, 3 * dnw), F32)
                ffn_c0 = jnp.zeros((b, ffn_conv_w.shape[1] - 1, d_ff), F32)
            else:
                o_a = _sb_sample(q, kbuf, vbuf, cache_k, cache_v, page_table, sb_logit_bias[l], l, b, t, sb_hd)
                h0 = jnp.concatenate([state_ssm_re[l].reshape(b, ns), state_ssm_im[l].reshape(b, ns)], axis=1)
                dn_s0, dn_c0, ffn_c0 = state_delta[l], state_delta_conv[l], state_ffn_conv[l]
            u_tm = rest3[:, :, :ssmw].transpose(1, 0, 2)
            y_tm, h_fin = _s5(u_tm, lw["bmat"], lw["arow"], h0, lw["cmat"], ssm_d[l][None, :], lw["wglu"],
                              ssm_b_glu[l][None, :])
            o_b = y_tm.transpose(1, 0, 2).reshape(b * t, ssmw)
            o_c, dn_s, dn_c = _deltanet(rest3, col0, dn_conv_w[l], lw["prm"], dn_norm[l][None, :], dn_c0, dn_s0,
                                        dn_heads, dn_hd)
            x = _outproj(o_a, o_b, o_c.reshape(b * t, dnw), norm_out_sb[l][None, :], norm_out_ssm[l][None, :],
                         lw["w_out"], x)
            x, ffn_c = _ffn(x, norm_ffn[l][None, :], lw["wg"], lw["wv"], lw["wd"], ffn_conv_w[l], ffn_c0, t)
            grp["x"] = x
            grp["outs"].append((h_fin[:, :ns].reshape(b, g, n), h_fin[:, ns:].reshape(b, g, n), dn_s, dn_c, ffn_c))

    res = {}
    for name, grp in groups.items():
        b, t = grp["b"], grp["t"]
        sre, sim, dn_s, dn_c, ffn_c = [jnp.stack(z) for z in zip(*grp["outs"])]
        if name == "p":
            heads_last = lambda a: a.reshape(depth, b, sb_heads, sb_hd, t).transpose(0, 1, 4, 2, 3)
        else:
            heads_last = lambda a: a.reshape(depth, b, t, sb_heads, sb_hd)
        res[name] = dict(
            y=grp["x"].reshape(b, t, d), k=heads_last(grp["kbuf"]), v=heads_last(grp["vbuf"]),
            sre=sre, sim=sim, dn_s=dn_s, dn_c=dn_c, ffn_c=ffn_c)
    p, s = res["p"], res["s"]
    return (p["y"], s["y"], p["k"], p["v"], s["k"], s["v"], p["sre"], p["sim"], s["sre"], s["sim"],
            p["dn_s"], s["dn_s"], p["dn_c"], s["dn_c"], p["ffn_c"], s["ffn_c"])
```

```python
import functools
import math

import jax
import jax.numpy as jnp
from jax import lax
from jax.experimental import pallas as pl
from jax.experimental.pallas import tpu as pltpu

F32 = jnp.float32
BF16 = jnp.bfloat16
HIGHEST = lax.Precision.HIGHEST
NORM_EPS = 1e-6
LANES = 128
SUBLANES = 8
BF16_ROWS = 16
VMEM_LIMIT_BYTES = 56 * 1024 * 1024
DN_CHUNK = 64
TRI_BASE = 16

_NT = (((1,), (1,)), ((), ()))
_TN = (((0,), (0,)), ((), ()))


def _tile(dim, pref, mult=LANES):
    if dim <= pref:
        return dim
    t = (pref // mult) * mult
    while t >= mult:
        if dim % t == 0:
            return t
        t -= mult
    raise ValueError(f"no tile for {dim} (pref {pref}, mult {mult})")


def _params(*sem):
    return pltpu.CompilerParams(dimension_semantics=sem, vmem_limit_bytes=VMEM_LIMIT_BYTES)


def _mm(a, b, dims=None):
    ok = all(d % BF16_ROWS == 0 for d in a.shape + b.shape)
    if ok:
        a, b = a.astype(BF16), b.astype(BF16)
    else:
        a, b = a.astype(F32), b.astype(F32)
    if dims is None:
        return jnp.dot(a, b, preferred_element_type=F32)
    return lax.dot_general(a, b, dims, preferred_element_type=F32)


def _mm_f32(a, b, dims=None):
    if dims is None:
        return jnp.dot(a, b, preferred_element_type=F32, precision=HIGHEST)
    return lax.dot_general(a, b, dims, preferred_element_type=F32, precision=HIGHEST)


def _softplus(z):
    return jnp.maximum(z, 0.0) + jnp.log1p(jnp.exp(-jnp.abs(z)))


def _sigmoid(z):
    return 1.0 / (1.0 + jnp.exp(-z))


def _rms(x, gain, eps=NORM_EPS):
    return x * lax.rsqrt(jnp.mean(x * x, axis=-1, keepdims=True) + eps) * gain


def _inproj_kernel(x_ref, g_ref, w_ref, gs_ref, e_ref, qg_ref, kg_ref, *refs, head_dim, q_scale, aliased,
                   time_minor):
    q_ref, k_ref, v_ref, r_ref, xn_ref = refs[2:] if aliased else refs
    j = pl.program_id(1)
    kv_layout = (lambda a: a.T) if time_minor else (lambda a: a)

    @pl.when(j == 0)
    def _():
        xn_ref[...] = _rms(x_ref[...], g_ref[...]).astype(BF16)

    y = jnp.dot(xn_ref[...], w_ref[...], preferred_element_type=F32)

    def head_norm(gain):
        ms = jnp.dot((y * y).astype(BF16), gs_ref[...], preferred_element_type=F32) * (1.0 / head_dim)
        inv = lax.rsqrt(ms + NORM_EPS)
        hi = inv.astype(BF16)
        lo = (inv - hi.astype(F32)).astype(BF16)
        sc = (jnp.dot(hi, e_ref[...], preferred_element_type=F32)
              + jnp.dot(lo, e_ref[...], preferred_element_type=F32))
        return y * sc * gain

    @pl.when(j == 0)
    def _():
        q_ref[...] = (head_norm(qg_ref[...]) * q_scale).astype(BF16)

    @pl.when(j == 1)
    def _():
        k_ref[...] = kv_layout(head_norm(kg_ref[...]))

    @pl.when(j == 2)
    def _():
        v_ref[...] = kv_layout(y)

    @pl.when(j >= 3)
    def _():
        r_ref[...] = y


def _inproj(x2d, gain, w_perm, gs, e, qg, kg, kbuf, vbuf, layer, depth, head_dim, q_scale, seq, time_minor):
    m, d = x2d.shape
    tn = gs.shape[0]
    assert w_perm.shape[1:] == (d, tn)
    n_rest = (w_perm.shape[0] - 3) * tn
    tm = _tile(seq, 512, LANES) if time_minor else _tile(m, 512, SUBLANES)
    tiles_per_seq = seq // tm
    aliased = kbuf is not None
    const = lambda i, j: (0, 0)
    in_specs = [
        pl.BlockSpec((tm, d), lambda i, j: (i, 0)),
        pl.BlockSpec((1, d), const),
        pl.BlockSpec((None, d, tn), lambda i, j: (j, 0, 0)),
        pl.BlockSpec((tn, LANES), const),
        pl.BlockSpec((LANES, tn), const),
        pl.BlockSpec((1, tn), const),
        pl.BlockSpec((1, tn), const),
    ]
    args = [x2d, gain, w_perm, gs, e, qg, kg]
    aliases = {}
    if aliased:
        in_specs += [pl.BlockSpec(memory_space=pl.ANY), pl.BlockSpec(memory_space=pl.ANY)]
        args += [kbuf, vbuf]
        aliases = {7: 1, 8: 2}
    if time_minor:
        kv_spec = pl.BlockSpec((None, None, tn, tm), lambda i, j: (layer, i // tiles_per_seq, 0, i % tiles_per_seq))
        kv_shape = jax.ShapeDtypeStruct((depth, m // seq, tn, seq), F32)
    else:
        kv_spec = pl.BlockSpec((None, tm, tn), lambda i, j: (layer, i, 0))
        kv_shape = jax.ShapeDtypeStruct((depth, m, tn), F32)
    out_specs = [
        pl.BlockSpec((tm, tn), lambda i, j: (i, 0)),
        kv_spec,
        kv_spec,
        pl.BlockSpec((tm, tn), lambda i, j: (i, jnp.maximum(j - 3, 0))),
    ]
    out_shape = [jax.ShapeDtypeStruct((m, tn), BF16), kv_shape, kv_shape, jax.ShapeDtypeStruct((m, n_rest), F32)]
    return pl.pallas_call(
        functools.partial(_inproj_kernel, head_dim=head_dim, q_scale=q_scale, aliased=aliased,
                          time_minor=time_minor),
        grid=(m // tm, 3 + n_rest // tn),
        in_specs=in_specs, out_specs=out_specs, out_shape=out_shape,
        scratch_shapes=[pltpu.VMEM((tm, d), BF16)],
        input_output_aliases=aliases,
        compiler_params=_params("parallel", "arbitrary"),
        name="inproj",
    )(*args)


def _softplus_unit(z):
    return jnp.maximum(z, 0.0) + jnp.log(1.0 + jnp.exp(-jnp.abs(z)))


SB_BLOCK = 256
SB_ROW_SUBBLOCKS = 4


def _sb_prompt_kernel(bias_ref, q_ref, k_ref, v_ref, o_ref, *, blk, nsub, head_dim, heads_per_step):
    hp = pl.program_id(1)
    qi = pl.program_id(2)
    row = lax.broadcasted_iota(jnp.int32, (blk, blk), 0)
    col = lax.broadcasted_iota(jnp.int32, (blk, blk), 1)
    later = (row > col).astype(BF16)
    visible = col < row
    q = q_ref[...]
    heads = range(heads_per_step)
    lanes = [slice(hh * head_dim, (hh + 1) * head_dim) for hh in heads]
    qs = [[q[j * blk:(j + 1) * blk, lanes[hh]] for j in range(nsub)] for hh in heads]
    biases = [bias_ref[hp * heads_per_step + hh] for hh in heads]

    def load(ki):
        r0 = pl.multiple_of(ki * blk, blk)
        kb = k_ref[:, pl.ds(r0, blk)].astype(BF16)
        vb = v_ref[:, pl.ds(r0, blk)].astype(BF16)
        return [(kb[lanes[hh], :], vb[lanes[hh], :]) for hh in heads]

    def block(qh, kv, bias, oc, masked):
        o_acc, carry = oc
        z = jnp.dot(qh, kv[0], preferred_element_type=F32) + bias
        sp = _softplus_unit(z)
        if masked:
            sp = jnp.where(visible, sp, 0.0)
        after = jnp.dot(sp.astype(BF16), later, preferred_element_type=F32)
        w = jnp.exp(z - sp - after - carry)
        if masked:
            w = jnp.where(visible, w, 0.0)
        o_acc = o_acc + lax.dot_general(w.astype(BF16), kv[1], _NT, preferred_element_type=F32)
        return o_acc, carry + after[:, 0:1] + sp[:, 0:1]

    zero = (jnp.zeros((blk, head_dim), F32), jnp.zeros((blk, 1), F32))
    state = [[zero for _ in range(nsub)] for _ in heads]
    for d in reversed(range(nsub)):
        kv = load(qi * nsub + d)
        for hh in heads:
            for j in range(d, nsub):
                state[hh][j] = block(qs[hh][j], kv[hh], biases[hh], state[hh][j], j == d)

    def older(it, flat):
        kv = load(qi * nsub - 1 - it)
        return tuple(block(qs[hh][j], kv[hh], biases[hh], flat[hh * nsub + j], False)
                     for hh in heads for j in range(nsub))

    flat = lax.fori_loop(0, qi * nsub, older, tuple(state[hh][j] for hh in heads for j in range(nsub)))
    o_ref[...] = jnp.concatenate(
        [jnp.concatenate([flat[hh * nsub + j][0] for hh in heads], axis=-1) for j in range(nsub)], axis=0)


def _sb_prompt(q, kbuf, vbuf, bias, layer, bsz, seq, head_dim):
    width = q.shape[-1]
    heads_per_step = LANES // head_dim
    blk = _tile(seq, SB_BLOCK)
    nsub = max(n for n in (1, 2, 4, 8) if n <= SB_ROW_SUBBLOCKS and seq % (n * blk) == 0)
    rows = nsub * blk
    q3 = q.reshape(bsz, seq, width)
    return pl.pallas_call(
        functools.partial(_sb_prompt_kernel, blk=blk, nsub=nsub, head_dim=head_dim, heads_per_step=heads_per_step),
        grid=(bsz, width // LANES, seq // rows),
        in_specs=[
            pl.BlockSpec(memory_space=pltpu.SMEM),
            pl.BlockSpec((None, rows, LANES), lambda b, h, i: (b, i, h)),
            pl.BlockSpec((None, None, LANES, seq), lambda b, h, i: (layer, b, h, 0)),
            pl.BlockSpec((None, None, LANES, seq), lambda b, h, i: (layer, b, h, 0)),
        ],
        out_specs=pl.BlockSpec((None, rows, LANES), lambda b, h, i: (b, i, h)),
        out_shape=jax.ShapeDtypeStruct((bsz, seq, width), F32),
        compiler_params=_params("parallel", "parallel", "arbitrary"),
        name="sb_prompt",
    )(bias, q3, kbuf, vbuf).reshape(bsz * seq, width)


SB_PAGE_GROUP = 4


def _sb_sample_kernel(pt_ref, q_ref, bcol_ref, kn_ref, vn_ref, *refs, pages_per_step, tq, heads, head_dim):
    del pt_ref
    kp = refs[:pages_per_step]
    vp = refs[pages_per_step:2 * pages_per_step]
    o_ref, acc_ref, carry_ref = refs[2 * pages_per_step:]
    c = pl.program_id(1)
    nr = heads * tq
    ltq = tq.bit_length() - 1
    page = kp[0].shape[-1]
    qf = q_ref[...].astype(F32)
    width = qf.shape[-1]
    bcol = bcol_ref[...]
    r = lax.broadcasted_iota(jnp.int32, (nr, width), 0)
    col = lax.broadcasted_iota(jnp.int32, (nr, width), 1)
    own_head = lax.shift_right_logical(r, ltq) == lax.shift_right_logical(col, head_dim.bit_length() - 1)
    qbd = jnp.where(own_head, jnp.concatenate([qf] * heads, axis=0), 0.0)

    @pl.when(c == 0)
    def _():
        z = lax.dot_general(qbd, kn_ref[...], _NT, preferred_element_type=F32) + bcol
        key = lax.broadcasted_iota(jnp.int32, (nr, tq), 1)
        qry = lax.broadcasted_iota(jnp.int32, (nr, tq), 0) & (tq - 1)
        vis = key < qry
        sp = jnp.where(vis, _softplus_unit(z), 0.0)
        later = (lax.broadcasted_iota(jnp.int32, (tq, tq), 0)
                 > lax.broadcasted_iota(jnp.int32, (tq, tq), 1)).astype(F32)
        w = jnp.where(vis, jnp.exp(z - sp - _mm_f32(sp, later)), 0.0)
        acc_ref[...] = jnp.dot(w, vn_ref[...], preferred_element_type=F32)
        carry_ref[...] = jnp.sum(sp, axis=-1, keepdims=True)

    qbd16 = qbd.astype(BF16)
    gsz = SB_PAGE_GROUP if pages_per_step % SB_PAGE_GROUP == 0 else 1
    nk = gsz * page
    later = (lax.broadcasted_iota(jnp.int32, (nk, nk), 0)
             > lax.broadcasted_iota(jnp.int32, (nk, nk), 1)).astype(BF16)
    side_by_side = lambda refs: jnp.concatenate(
        [r[...].reshape(width, page).astype(BF16) for r in reversed(refs)], axis=1)
    for g in range(pages_per_step // gsz):
        kt = side_by_side(kp[g * gsz:(g + 1) * gsz])
        z = jnp.dot(qbd16, kt, preferred_element_type=F32) + bcol
        sp = _softplus_unit(z)
        after = jnp.dot(sp.astype(BF16), later, preferred_element_type=F32)
        carry = carry_ref[...]
        w = jnp.exp(z - sp - after - carry).astype(BF16)
        vt = side_by_side(vp[g * gsz:(g + 1) * gsz])
        acc_ref[...] += lax.dot_general(w, vt, _NT, preferred_element_type=F32)
        carry_ref[...] = carry + after[:, 0:1] + sp[:, 0:1]

    @pl.when(c == pl.num_programs(1) - 1)
    def _():
        acc = jnp.where(own_head, acc_ref[...], 0.0)
        o_ref[...] = acc.reshape(heads, tq, width).sum(axis=0)


def _sb_sample(q, kbuf, vbuf, cache_k, cache_v, page_table, bias, layer, bsz, tq, head_dim):
    width = q.shape[-1]
    heads = width // head_dim
    assert tq & (tq - 1) == 0 and head_dim & (head_dim - 1) == 0
    page = cache_k.shape[2]
    npages = page_table.shape[1]
    pps = _tile(npages, 8, 1)
    bcol = jnp.repeat(bias.astype(F32), tq)[:, None]
    cache_k = cache_k.transpose(0, 1, 3, 4, 2)
    cache_v = cache_v.transpose(0, 1, 3, 4, 2)

    def page_map(p):
        return lambda b, c, pt: (layer, pt[b * npages + npages - 1 - (c * pps + p)], 0, 0, 0)

    page_specs = [pl.BlockSpec((None, None, heads, head_dim, page), page_map(p)) for p in range(pps)]
    grid_spec = pltpu.PrefetchScalarGridSpec(
        num_scalar_prefetch=1,
        grid=(bsz, npages // pps),
        in_specs=[
            pl.BlockSpec((None, tq, width), lambda b, c, pt: (b, 0, 0)),
            pl.BlockSpec((heads * tq, 1), lambda b, c, pt: (0, 0)),
            pl.BlockSpec((None, None, tq, width), lambda b, c, pt: (layer, b, 0, 0)),
            pl.BlockSpec((None, None, tq, width), lambda b, c, pt: (layer, b, 0, 0)),
        ] + page_specs + page_specs,
        out_specs=pl.BlockSpec((None, tq, width), lambda b, c, pt: (b, 0, 0)),
        scratch_shapes=[pltpu.VMEM((heads * tq, width), F32), pltpu.VMEM((heads * tq, 1), F32)],
    )
    k4 = kbuf.reshape(kbuf.shape[0], bsz, tq, width)
    v4 = vbuf.reshape(vbuf.shape[0], bsz, tq, width)
    out = pl.pallas_call(
        functools.partial(_sb_sample_kernel, pages_per_step=pps, tq=tq, heads=heads, head_dim=head_dim),
        grid_spec=grid_spec,
        out_shape=jax.ShapeDtypeStruct((bsz, tq, width), F32),
        compiler_params=_params("parallel", "arbitrary"),
        name="sb_sample",
    )(page_table.reshape(-1), q.reshape(bsz, tq, width), bcol, k4, v4,
      *([cache_k] * pps), *([cache_v] * pps))
    return out.reshape(bsz * tq, width)


def _s5_discretize_kernel(lre_ref, lim_ref, ls_ref, bre_ref, bim_ref, are_ref, aim_ref, bbre_ref, bbim_ref):
    lam_re, lam_im = lre_ref[...], lim_ref[...]
    dt = jnp.exp(ls_ref[...])
    mag = jnp.exp(lam_re * dt)
    ab_re = mag * jnp.cos(lam_im * dt)
    ab_im = mag * jnp.sin(lam_im * dt)
    den = lam_re * lam_re + lam_im * lam_im
    xr = ab_re - 1.0
    f_re = (xr * lam_re + ab_im * lam_im) / den
    f_im = (ab_im * lam_re - xr * lam_im) / den
    are_ref[...] = ab_re
    aim_ref[...] = ab_im
    bbre_ref[...] = f_re * bre_ref[...] - f_im * bim_ref[...]
    bbim_ref[...] = f_re * bim_ref[...] + f_im * bre_ref[...]


def _s5_discretize(lam_re, lam_im, log_step, b_re, b_im):
    g, n = lam_re.shape
    ch = b_re.shape[-1]
    col = lambda a: a.reshape(g * n, 1)
    ls = jnp.broadcast_to(log_step[:, None], (g, n))
    return pl.pallas_call(
        _s5_discretize_kernel,
        out_shape=[jax.ShapeDtypeStruct((g * n, 1), F32)] * 2 + [jax.ShapeDtypeStruct((g * n, ch), F32)] * 2,
        name="s5_discretize",
    )(col(lam_re), col(lam_im), col(ls), b_re.reshape(g * n, ch), b_im.reshape(g * n, ch))


def _s5_kernel(u_ref, bmat_ref, arow_ref, h0_ref, cmat_ref, d_ref, wglu_ref, bglu_ref,
               y_ref, hfin_ref, hbuf_ref, hc_ref, *, tt, nb, ns, cb):
    ti = pl.program_id(1)
    width = u_ref.shape[-1]

    @pl.when(ti == 0)
    def _():
        hc_ref[...] = h0_ref[...]

    u = u_ref[...].reshape(tt * nb, width)
    hbuf_ref[...] = jnp.dot(u.astype(BF16), bmat_ref[...], preferred_element_type=F32)

    for c in range(ns // cb):
        re = slice(c * cb, (c + 1) * cb)
        im = slice(ns + c * cb, ns + (c + 1) * cb)
        a_re = jnp.broadcast_to(arow_ref[:, re], (SUBLANES, cb))
        a_im = jnp.broadcast_to(arow_ref[:, im], (SUBLANES, cb))

        def batch_rows(bb, _):
            r0 = pl.multiple_of(bb * SUBLANES, SUBLANES)

            def step(t, h):
                h_re, h_im = h
                r = pl.multiple_of(t * nb + r0, SUBLANES)
                n_re = a_re * h_re - a_im * h_im + hbuf_ref[pl.ds(r, SUBLANES), re]
                n_im = a_re * h_im + a_im * h_re + hbuf_ref[pl.ds(r, SUBLANES), im]
                hbuf_ref[pl.ds(r, SUBLANES), re] = n_re
                hbuf_ref[pl.ds(r, SUBLANES), im] = n_im
                return n_re, n_im

            h = (hc_ref[pl.ds(r0, SUBLANES), re], hc_ref[pl.ds(r0, SUBLANES), im])
            h = lax.fori_loop(0, tt, step, h, unroll=min(tt, 8))
            hc_ref[pl.ds(r0, SUBLANES), re] = h[0]
            hc_ref[pl.ds(r0, SUBLANES), im] = h[1]
            return 0

        lax.fori_loop(0, nb // SUBLANES, batch_rows, 0)

    y = jnp.dot(hbuf_ref[...].astype(BF16), cmat_ref[...], preferred_element_type=F32) + d_ref[...] * u
    y = 0.5 * y * (1.0 + jnp.tanh(math.sqrt(2.0 / math.pi) * (y + 0.044715 * (y * y * y))))
    z = jnp.dot(y.astype(BF16), wglu_ref[...], preferred_element_type=F32) + bglu_ref[...]
    y_ref[...] = (y * _sigmoid(z)).reshape(tt, nb, width)

    @pl.when(ti == pl.num_programs(1) - 1)
    def _():
        hfin_ref[...] = hc_ref[...]


def _s5(u_tm, bmat, arow, h0, cmat, d_row, wglu, bglu):
    t, b, width = u_tm.shape
    ns2 = bmat.shape[1]
    nb = _tile(b, 64, SUBLANES)
    tt = _tile(t, max(SUBLANES, 512 // nb), 1)
    const = lambda i, j: (0, 0)
    return pl.pallas_call(
        functools.partial(_s5_kernel, tt=tt, nb=nb, ns=ns2 // 2, cb=_tile(ns2 // 2, 512)),
        grid=(b // nb, t // tt),
        in_specs=[
            pl.BlockSpec((tt, nb, width), lambda i, j: (j, i, 0)),
            pl.BlockSpec((width, ns2), const),
            pl.BlockSpec((1, ns2), const),
            pl.BlockSpec((nb, ns2), lambda i, j: (i, 0)),
            pl.BlockSpec((ns2, width), const),
            pl.BlockSpec((1, width), const),
            pl.BlockSpec((width, width), const),
            pl.BlockSpec((1, width), const),
        ],
        out_specs=[
            pl.BlockSpec((tt, nb, width), lambda i, j: (j, i, 0)),
            pl.BlockSpec((nb, ns2), lambda i, j: (i, 0)),
        ],
        out_shape=[jax.ShapeDtypeStruct((t, b, width), F32), jax.ShapeDtypeStruct((b, ns2), F32)],
        scratch_shapes=[pltpu.VMEM((tt * nb, ns2), F32), pltpu.VMEM((nb, ns2), F32)],
        compiler_params=_params("parallel", "arbitrary"),
        name="s5",
    )(u_tm, bmat, arow, h0, cmat, d_row, wglu, bglu)


def _split(a):
    hi = a.astype(BF16)
    return hi, (a - hi.astype(F32)).astype(BF16)


def _mm3(a, b):
    d = lambda x, y: jnp.dot(x, y, preferred_element_type=F32)
    return d(a[0], b[0]) + d(a[0], b[1]) + d(a[1], b[0])


def _block_tri_inverse(a, n, c):
    row = lax.broadcasted_iota(jnp.int32, (n, n), 0)
    col = lax.broadcasted_iota(jnp.int32, (n, n), 1)
    eye = (row == col).astype(F32)

    def same_block(b):
        s = b.bit_length() - 1
        return lax.shift_right_logical(row, s) == lax.shift_right_logical(col, s)

    b0 = min(c, TRI_BASE)
    neg = -jnp.where(same_block(b0), a, 0.0)
    t = eye + neg
    p = _split(neg)
    k = 1
    while 2 * k < b0:
        p = _split(_mm3(p, p))
        t = t + _mm3(_split(t), p)
        k *= 2
    b = b0
    while b < c:
        off = jnp.where(jnp.logical_and(same_block(2 * b), jnp.logical_not(same_block(b))), a, 0.0)
        ts = _split(t)
        t = t - _mm3(_split(_mm3(ts, _split(off))), ts)
        b *= 2
    return t


def _dn_kernel(q_ref, k_ref, v_ref, ba_ref, gate_ref, cw_ref, prm_ref, nrm_ref, cbuf_ref, s0_ref,
               o_ref, sout_ref, cout_ref,
               xpad_ref, qs_ref, ks_ref, vs_ref, gb_ref, s_ref, *, bs, tt, chunk, nh, hd, taps):
    ti = pl.program_id(1)
    width = nh * hd
    halo = SUBLANES
    first = halo - (taps - 1)

    @pl.when(ti == 0)
    def _():
        s_ref[...] = s0_ref[...]
        xpad_ref[:, first:halo, :] = cbuf_ref[...]

    xpad_ref[:, halo:halo + tt, 0:width] = q_ref[...]
    xpad_ref[:, halo:halo + tt, width:2 * width] = k_ref[...]
    xpad_ref[:, halo:halo + tt, 2 * width:3 * width] = v_ref[...]
    cw = cw_ref[...]
    y = cw[0:1, :] * xpad_ref[:, first:first + tt, :]
    for i in range(1, taps):
        y = y + cw[i:i + 1, :] * xpad_ref[:, first + i:first + i + tt, :]
    cout_ref[...] = xpad_ref[:, halo + tt - (taps - 1):halo + tt, :]
    xpad_ref[:, 0:halo, :] = xpad_ref[:, tt:tt + halo, :]
    act = y * _sigmoid(y)
    for h in range(nh):
        sl = slice(h * hd, (h + 1) * hd)
        qh = act[:, :, h * hd:(h + 1) * hd]
        kh = act[:, :, width + h * hd:width + (h + 1) * hd]
        qs_ref[:, :, sl] = qh * lax.rsqrt(jnp.sum(qh * qh, axis=-1, keepdims=True) + NORM_EPS) * (hd ** -0.5)
        ks_ref[:, :, sl] = kh * lax.rsqrt(jnp.sum(kh * kh, axis=-1, keepdims=True) + NORM_EPS)
    vs_ref[...] = act[:, :, 2 * width:3 * width]
    ba = ba_ref[...]
    lane = lax.broadcasted_iota(jnp.int32, ba.shape, 2)
    gval = -jnp.exp(prm_ref[0:1, :]) * _softplus(ba + prm_ref[1:2, :])
    gb_ref[...] = jnp.where(lane < nh, _sigmoid(ba), jnp.where(lane < 2 * nh, gval, 0.0))

    pairs = [(s, h) for s in range(bs) for h in range(nh)]
    npairs = len(pairs)
    nrow = npairs * chunk
    lc = chunk.bit_length() - 1
    row = lax.broadcasted_iota(jnp.int32, (nrow, nrow), 0)
    col = lax.broadcasted_iota(jnp.int32, (nrow, nrow), 1)
    same = lax.shift_right_logical(row, lc) == lax.shift_right_logical(col, lc)
    lower = jnp.logical_and(same, row >= col)
    strict = jnp.logical_and(same, row > col)
    tri = lower.astype(BF16)
    last_sel = col == (row | (chunk - 1))
    prow = lax.shift_right_logical(lax.broadcasted_iota(jnp.int32, (nrow, LANES), 0), lc)
    plane = lax.broadcasted_iota(jnp.int32, (nrow, LANES), 1)
    head_of_row = prow & (nh - 1)
    is_beta = plane == head_of_row
    is_g = plane == head_of_row + nh
    ones8 = jnp.ones((BF16_ROWS, LANES), BF16)
    pair_of_row = lax.shift_right_logical(lax.broadcasted_iota(jnp.int32, (nrow, hd), 0), lc)

    def expand(x):
        return jnp.concatenate([jnp.where(pair_of_row == p, x, 0.0) for p in range(npairs)], axis=1)

    def one_chunk(c):
        rows = pl.ds(c * chunk, chunk)
        stack = lambda ref: jnp.concatenate([ref[s, rows, h * hd:(h + 1) * hd] for s, h in pairs], axis=0)
        qst, kst, vst = stack(qs_ref), stack(ks_ref), stack(vs_ref)
        gbst = jnp.concatenate([gb_ref[s, rows, :] for s, _ in pairs], axis=0)
        g_hi, g_lo = _split(gbst)
        gcum_all = (jnp.dot(tri, g_hi, preferred_element_type=F32)
                    + jnp.dot(tri, g_lo, preferred_element_type=F32))
        gsel = jnp.where(is_g, gcum_all, 0.0)
        gc = jnp.sum(gsel, axis=-1, keepdims=True)
        beta = jnp.sum(jnp.where(is_beta, gbst, 0.0), axis=-1, keepdims=True)
        s_hi, s_lo = _split(gsel)
        grow = (lax.dot_general(ones8, s_hi, _NT, preferred_element_type=F32)
                + lax.dot_general(ones8, s_lo, _NT, preferred_element_type=F32))[0:1, :]
        decay = jnp.where(lower, jnp.exp(jnp.minimum(gc - grow, 0.0)), 0.0)
        kb = kst * beta
        a = _mm(kb, kst, _NT) * jnp.where(strict, decay, 0.0)
        tinv = _block_tri_inverse(a, nrow, chunk)
        u = _mm(tinv, vst * beta)
        w = _mm(tinv, kb * jnp.exp(gc))
        attn = _mm(qst, kst, _NT) * decay
        s_stack = jnp.concatenate([s_ref[s, h] for s, h in pairs], axis=0)
        v_new = u - _mm(expand(w), s_stack)
        o = _mm(expand(qst * jnp.exp(gc)), s_stack) + _mm(attn, v_new)
        g_last = jnp.sum(jnp.where(last_sel, grow, 0.0), axis=-1, keepdims=True)
        upd = _mm(expand(kst * jnp.exp(g_last - gc)), v_new, _TN)
        gt = stack(gate_ref)
        on = _rms(o, nrm_ref[...]) * (gt * _sigmoid(gt))
        for idx, (s, h) in enumerate(pairs):
            gl = g_last[idx * chunk:idx * chunk + 1, :]
            s_ref[s, h] = s_ref[s, h] * jnp.exp(gl) + upd[idx * hd:(idx + 1) * hd, :]
            o_ref[s, rows, h * hd:(h + 1) * hd] = on[idx * chunk:(idx + 1) * chunk, :]

    for c in range(tt // chunk):
        one_chunk(c)

    @pl.when(ti == pl.num_programs(1) - 1)
    def _():
        sout_ref[...] = s_ref[...]


def _deltanet(rest3, col0, conv_w, prm, nrm, cbuf, s0, nh, hd):
    nseq, t, _ = rest3.shape
    width = nh * hd
    taps = conv_w.shape[0]
    chunk = min(DN_CHUNK, t)
    assert t % chunk == 0 and chunk & (chunk - 1) == 0 and t >= taps - 1
    tt = _tile(t, 256, chunk)
    bs = _tile(nseq, 8, 1) if tt == t and t <= SUBLANES else 1
    ba_col = (col0 + 4) * (width // LANES)
    blk = lambda cidx: pl.BlockSpec((bs, tt, width), lambda i, j: (i, j, cidx))
    const = lambda i, j: (0, 0)
    return pl.pallas_call(
        functools.partial(_dn_kernel, bs=bs, tt=tt, chunk=chunk, nh=nh, hd=hd, taps=taps),
        grid=(nseq // bs, t // tt),
        in_specs=[
            blk(col0 + 1), blk(col0 + 2), blk(col0 + 3),
            pl.BlockSpec((bs, tt, LANES), lambda i, j: (i, j, ba_col)),
            blk(col0),
            pl.BlockSpec((taps, 3 * width), const),
            pl.BlockSpec((SUBLANES, LANES), const),
            pl.BlockSpec((1, hd), const),
            pl.BlockSpec((bs, taps - 1, 3 * width), lambda i, j: (i, 0, 0)),
            pl.BlockSpec((bs, nh, hd, hd), lambda i, j: (i, 0, 0, 0)),
        ],
        out_specs=[
            pl.BlockSpec((bs, tt, width), lambda i, j: (i, j, 0)),
            pl.BlockSpec((bs, nh, hd, hd), lambda i, j: (i, 0, 0, 0)),
            pl.BlockSpec((bs, taps - 1, 3 * width), lambda i, j: (i, 0, 0)),
        ],
        out_shape=[
            jax.ShapeDtypeStruct((nseq, t, width), F32),
            jax.ShapeDtypeStruct((nseq, nh, hd, hd), F32),
            jax.ShapeDtypeStruct((nseq, taps - 1, 3 * width), F32),
        ],
        scratch_shapes=[
            pltpu.VMEM((bs, tt + 2 * SUBLANES, 3 * width), F32),
            pltpu.VMEM((bs, tt, width), F32),
            pltpu.VMEM((bs, tt, width), F32),
            pltpu.VMEM((bs, tt, width), F32),
            pltpu.VMEM((bs, tt, LANES), F32),
            pltpu.VMEM((bs, nh, hd, hd), F32),
        ],
        compiler_params=_params("parallel", "arbitrary"),
        name="deltanet",
    )(rest3, rest3, rest3, rest3, rest3, conv_w, prm, nrm, cbuf, s0)


def _outproj_kernel(oa_ref, ob_ref, oc_ref, ga_ref, gb_ref, w_ref, x_ref, o_ref, mix_ref):
    wa, wb = oa_ref.shape[-1], ob_ref.shape[-1]

    @pl.when(pl.program_id(1) == 0)
    def _():
        mix_ref[:, 0:wa] = _rms(oa_ref[...], ga_ref[...]).astype(BF16)
        mix_ref[:, wa:wa + wb] = _rms(ob_ref[...], gb_ref[...]).astype(BF16)
        mix_ref[:, wa + wb:] = oc_ref[...].astype(BF16)

    o_ref[...] = x_ref[...] + jnp.dot(mix_ref[...], w_ref[...], preferred_element_type=F32)


def _outproj(oa, ob, oc, ga, gb, w, x2d):
    m, d = x2d.shape
    tm = _tile(m, 512, SUBLANES)
    _, d_in, tn = w.shape
    row = lambda a: pl.BlockSpec((tm, a.shape[-1]), lambda i, j: (i, 0))
    const = lambda a: pl.BlockSpec((1, a.shape[-1]), lambda i, j: (0, 0))
    return pl.pallas_call(
        _outproj_kernel,
        grid=(m // tm, d // tn),
        in_specs=[row(oa), row(ob), row(oc), const(ga), const(gb),
                  pl.BlockSpec((None, d_in, tn), lambda i, j: (j, 0, 0)),
                  pl.BlockSpec((tm, tn), lambda i, j: (i, j))],
        out_specs=pl.BlockSpec((tm, tn), lambda i, j: (i, j)),
        out_shape=jax.ShapeDtypeStruct((m, d), F32),
        scratch_shapes=[pltpu.VMEM((tm, d_in), BF16)],
        compiler_params=_params("parallel", "arbitrary"),
        name="outproj",
    )(oa, ob, oc, ga, gb, w, x2d)


FFN_SUBTILE = 256


def _ffn_kernel(x_ref, g_ref, wg_ref, wv_ref, wd_ref, cw_ref, st_ref, o_ref, buf_ref, h_ref, act_ref, carry_ref,
                *, nf, tiles_per_seq, per_seq_rows):
    i = pl.program_id(0)
    f = pl.program_id(1)
    tm, tf = act_ref.shape

    def up():
        h = h_ref[...]
        if not per_seq_rows:
            @pl.when((i % tiles_per_seq) == 0)
            def _():
                carry_ref[f] = st_ref[...]

        tc = FFN_SUBTILE if tf % FFN_SUBTILE == 0 else tf
        row = lax.broadcasted_iota(jnp.int32, (tm, tc), 0)
        for s in range(tf // tc):
            cols = slice(s * tc, (s + 1) * tc)
            gate = jnp.dot(h, wg_ref[:, cols], preferred_element_type=F32)
            val = jnp.dot(h, wv_ref[:, cols], preferred_element_type=F32)
            prev1 = pltpu.roll(gate, 1, axis=0)
            prev2 = pltpu.roll(gate, 2, axis=0)
            if per_seq_rows:
                nseq = tm // per_seq_rows
                expand = lambda a: jnp.broadcast_to(a, (nseq, per_seq_rows, tc)).reshape(tm, tc)
                s0, s1 = expand(st_ref[:, 0:1, cols]), expand(st_ref[:, 1:2, cols])
                t = row & (per_seq_rows - 1)
                buf_ref[:, :, cols] = gate.reshape(nseq, per_seq_rows, tc)[:, per_seq_rows - 2:, :]
            else:
                s0, s1 = carry_ref[f, 0:1, cols], carry_ref[f, 1:2, cols]
                t = row
                tail = gate[tm - 2:tm, :]
                carry_ref[f, :, cols] = tail
                buf_ref[:, cols] = tail
            x1 = jnp.where(t >= 1, prev1, s1)
            x2 = jnp.where(t >= 2, prev2, jnp.where(t == 1, s1, s0))
            conv = cw_ref[2:3, cols] * gate + cw_ref[1:2, cols] * x1 + cw_ref[0:1, cols] * x2
            act_ref[:, cols] = (conv * _sigmoid(conv) * val).astype(BF16)

    def down():
        o_ref[...] += jnp.dot(act_ref[...], wd_ref[...], preferred_element_type=F32)

    @pl.when(f == 0)
    def _():
        h_ref[...] = _rms(x_ref[...], g_ref[...]).astype(BF16)
        o_ref[...] = x_ref[...]
        up()

    @pl.when(jnp.logical_and(f > 0, f < nf))
    def _():
        down()
        up()

    @pl.when(f == nf)
    def _():
        down()


def _ffn(x2d, gain, wg, wv, wd, conv_w, state, seq):
    m, d = x2d.shape
    nf, _, tf = wg.shape
    d_ff = nf * tf
    nseq = m // seq
    assert conv_w.shape[0] == 3
    up_tile = lambda f: jnp.minimum(f, nf - 1)
    down_tile = lambda f: jnp.maximum(f - 1, 0)
    if seq <= SUBLANES:
        assert seq == SUBLANES
        tm = _tile(m, 512, SUBLANES)
        per_seq_rows, tiles_per_seq = seq, 1
        st_spec = buf_spec = pl.BlockSpec((tm // seq, 2, tf), lambda i, f: (i, 0, up_tile(f)))
        n_buf = nseq
    else:
        tm = _tile(seq, 512, SUBLANES)
        per_seq_rows, tiles_per_seq = 0, seq // tm
        st_spec = pl.BlockSpec((None, 2, tf), lambda i, f: (i // tiles_per_seq, 0, up_tile(f)))
        buf_spec = pl.BlockSpec((None, 2, tf), lambda i, f: (i, 0, up_tile(f)))
        n_buf = m // tm
    y, buf = pl.pallas_call(
        functools.partial(_ffn_kernel, nf=nf, tiles_per_seq=tiles_per_seq, per_seq_rows=per_seq_rows),
        grid=(m // tm, nf + 1),
        in_specs=[
            pl.BlockSpec((tm, d), lambda i, f: (i, 0)),
            pl.BlockSpec((1, d), lambda i, f: (0, 0)),
            pl.BlockSpec((None, d, tf), lambda i, f: (up_tile(f), 0, 0)),
            pl.BlockSpec((None, d, tf), lambda i, f: (up_tile(f), 0, 0)),
            pl.BlockSpec((tf, d), lambda i, f: (down_tile(f), 0)),
            pl.BlockSpec((3, tf), lambda i, f: (0, up_tile(f))),
            st_spec,
        ],
        out_specs=[pl.BlockSpec((tm, d), lambda i, f: (i, 0)), buf_spec],
        out_shape=[jax.ShapeDtypeStruct((m, d), F32), jax.ShapeDtypeStruct((n_buf, 2, d_ff), F32)],
        scratch_shapes=[pltpu.VMEM((tm, d), BF16), pltpu.VMEM((tm, tf), BF16), pltpu.VMEM((nf, 2, tf), F32)],
        compiler_params=_params("arbitrary", "arbitrary"),
        name="convffn",
    )(x2d, gain, wg, wv, wd, conv_w, state)
    if not per_seq_rows:
        buf = buf.reshape(nseq, tiles_per_seq, 2, d_ff)[:, tiles_per_seq - 1]
    return y, buf


FFN_TILE = 512
OUT_TILE = 1024


def _col_tiles(w, tn):
    d, n = w.shape
    return w.reshape(d, n // tn, tn).transpose(1, 0, 2)


def _layer_weights(l, w_in, sb_q_norm, sb_k_norm, ssm, dn_a_log, dn_dt_bias, ssm_w_glu, w_out, ffn_w_in, ffn_w_out,
                   dims):
    sbw, ssmw, dnw, nh = dims["sbw"], dims["ssmw"], dims["dnw"], dims["dn_heads"]
    heads, hd = dims["sb_heads"], dims["sb_hd"]
    d = w_in.shape[1]
    o_u = 3 * sbw
    o_qkv = o_u + ssmw
    o_ba = o_qkv + 3 * dnw
    o_gate = o_ba + 2 * nh
    n_rest = -(-(ssmw + 4 * dnw + LANES) // sbw) * sbw
    pad = n_rest - (ssmw + 4 * dnw + 2 * nh)
    w = w_in[l]
    w_perm = jnp.concatenate([
        w[:, :o_u], w[:, o_u:o_qkv], w[:, o_gate:o_gate + dnw], w[:, o_qkv:o_ba], w[:, o_ba:o_gate],
        jnp.zeros((d, pad), w.dtype)], axis=1).astype(BF16)
    head_of = jnp.arange(sbw) // hd
    gs = (head_of[:, None] == jnp.arange(LANES)[None, :]).astype(BF16)
    qg = jnp.tile(sb_q_norm[l], heads)[None, :]
    kg = jnp.tile(sb_k_norm[l], heads)[None, :]
    a_re, a_im, bb_re, bb_im = ssm["disc"]
    g, n, ch = dims["ssm_groups"], dims["ssm_state"], dims["ssm_ch"]
    ns = g * n
    grp_of_state = jnp.arange(ns) // n
    grp_of_chan = jnp.arange(ssmw) // ch
    blockmask = grp_of_chan[:, None] == grp_of_state[None, :]
    expand_b = lambda bb: jnp.where(blockmask, jnp.tile(bb.T, (g, 1)), 0.0)
    bmat = jnp.concatenate([expand_b(bb_re), expand_b(bb_im)], axis=1).astype(BF16)
    arow = jnp.concatenate([a_re, a_im], axis=0).reshape(1, 2 * ns)
    expand_c = lambda c: jnp.where(blockmask.T, jnp.tile(c.transpose(0, 2, 1).reshape(ns, ch), (1, g)), 0.0)
    cmat = jnp.concatenate([expand_c(ssm["c_re"][l]), -expand_c(ssm["c_im"][l])], axis=0).astype(BF16)
    prm = jnp.zeros((SUBLANES, LANES), F32)
    prm = prm.at[0, nh:2 * nh].set(dn_a_log[l]).at[1, nh:2 * nh].set(dn_dt_bias[l])
    d_ff = ffn_w_in.shape[2] // 2
    tf = _tile(d_ff, FFN_TILE)
    return dict(
        w_perm=_col_tiles(w_perm, sbw), gs=gs, e=gs.T, qg=qg, kg=kg, bmat=bmat, arow=arow, cmat=cmat, prm=prm,
        wglu=ssm_w_glu[l].astype(BF16), w_out=_col_tiles(w_out[l].astype(BF16), _tile(w_out.shape[2], OUT_TILE)),
        wg=_col_tiles(ffn_w_in[l][:, :d_ff].astype(BF16), tf), wv=_col_tiles(ffn_w_in[l][:, d_ff:].astype(BF16), tf),
        wd=ffn_w_out[l].astype(BF16), n_rest=n_rest)


def kernel(x_prompt, x_sample, cache_k, cache_v, page_table, state_ssm_re, state_ssm_im, state_delta, state_delta_conv, state_ffn_conv, norm_mix, w_in, sb_q_norm, sb_k_norm, sb_logit_bias, ssm_lambda_re, ssm_lambda_im, ssm_log_step, ssm_b_re, ssm_b_im, ssm_c_re, ssm_c_im, ssm_d, ssm_w_glu, ssm_b_glu, dn_conv_w, dn_a_log, dn_dt_bias, dn_norm, norm_out_sb, norm_out_ssm, w_out, norm_ffn, ffn_w_in, ffn_conv_w, ffn_w_out):
    depth = w_in.shape[0]
    bp, tp, d = x_prompt.shape
    bd, td, _ = x_sample.shape
    sb_heads, sb_hd = cache_k.shape[-2], cache_k.shape[-1]
    g, n = ssm_lambda_re.shape[1:]
    ch = ssm_b_re.shape[-1]
    dn_heads, dn_hd = dn_a_log.shape[1], dn_norm.shape[1]
    dims = dict(sbw=sb_heads * sb_hd, sb_heads=sb_heads, sb_hd=sb_hd, ssmw=g * ch, ssm_groups=g, ssm_state=n,
                ssm_ch=ch, dnw=dn_heads * dn_hd, dn_heads=dn_heads)
    sbw, ssmw, dnw, ns = dims["sbw"], dims["ssmw"], dims["dnw"], g * n
    assert ssmw % dnw == 0 and dnw % LANES == 0 and sbw % LANES == 0 and LANES % sb_hd == 0
    d_ff = ffn_conv_w.shape[2]
    col0 = ssmw // dnw
    q_scale = sb_hd ** -0.5

    groups = {
        "p": dict(x=x_prompt.reshape(bp * tp, d), b=bp, t=tp, kbuf=None, vbuf=None, outs=[]),
        "s": dict(x=x_sample.reshape(bd * td, d), b=bd, t=td, kbuf=None, vbuf=None, outs=[]),
    }
    for l in range(depth):
        disc = _s5_discretize(ssm_lambda_re[l], ssm_lambda_im[l], ssm_log_step[l], ssm_b_re[l], ssm_b_im[l])
        lw = _layer_weights(l, w_in, sb_q_norm, sb_k_norm, dict(disc=disc, c_re=ssm_c_re, c_im=ssm_c_im),
                            dn_a_log, dn_dt_bias, ssm_w_glu, w_out, ffn_w_in, ffn_w_out, dims)
        for name, grp in groups.items():
            b, t, x = grp["b"], grp["t"], grp["x"]
            q, kbuf, vbuf, rest = _inproj(x, norm_mix[l][None, :], lw["w_perm"], lw["gs"], lw["e"], lw["qg"],
                                          lw["kg"], grp["kbuf"], grp["vbuf"], l, depth, sb_hd, q_scale, t,
                                          time_minor=(name == "p"))
            grp["kbuf"], grp["vbuf"] = kbuf, vbuf
            rest3 = rest.reshape(b, t, lw["n_rest"])
            if name == "p":
                o_a = _sb_prompt(q, kbuf, vbuf, sb_logit_bias[l], l, b, t, sb_hd)
                h0 = jnp.zeros((b, 2 * ns), F32)
                dn_s0 = jnp.zeros((b, dn_heads, dn_hd, dn_hd), F32)
                dn_c0 = jnp.zeros((b, dn_conv_w.shape[1] - 1, 3 * dnw), F32)
                ffn_c0 = jnp.zeros((b, ffn_conv_w.shape[1] - 1, d_ff), F32)
            else:
                o_a = _sb_sample(q, kbuf, vbuf, cache_k, cache_v, page_table, sb_logit_bias[l], l, b, t, sb_hd)
                h0 = jnp.concatenate([state_ssm_re[l].reshape(b, ns), state_ssm_im[l].reshape(b, ns)], axis=1)
                dn_s0, dn_c0, ffn_c0 = state_delta[l], state_delta_conv[l], state_ffn_conv[l]
            u_tm = rest3[:, :, :ssmw].transpose(1, 0, 2)
            y_tm, h_fin = _s5(u_tm, lw["bmat"], lw["arow"], h0, lw["cmat"], ssm_d[l][None, :], lw["wglu"],
                              ssm_b_glu[l][None, :])
            o_b = y_tm.transpose(1, 0, 2).reshape(b * t, ssmw)
            o_c, dn_s, dn_c = _deltanet(rest3, col0, dn_conv_w[l], lw["prm"], dn_norm[l][None, :], dn_c0, dn_s0,
                                        dn_heads, dn_hd)
            x = _outproj(o_a, o_b, o_c.reshape(b * t, dnw), norm_out_sb[l][None, :], norm_out_ssm[l][None, :],
                         lw["w_out"], x)
            x, ffn_c = _ffn(x, norm_ffn[l][None, :], lw["wg"], lw["wv"], lw["wd"], ffn_conv_w[l], ffn_c0, t)
            grp["x"] = x
            grp["outs"].append((h_fin[:, :ns].reshape(b, g, n), h_fin[:, ns:].reshape(b, g, n), dn_s, dn_c, ffn_c))

    res = {}
    for name, grp in groups.items():
        b, t = grp["b"], grp["t"]
        sre, sim, dn_s, dn_c, ffn_c = [jnp.stack(z) for z in zip(*grp["outs"])]
        if name == "p":
            heads_last = lambda a: a.reshape(depth, b, sb_heads, sb_hd, t).transpose(0, 1, 4, 2, 3)
        else:
            heads_last = lambda a: a.reshape(depth, b, t, sb_heads, sb_hd)
        res[name] = dict(
            y=grp["x"].reshape(b, t, d), k=heads_last(grp["kbuf"]), v=heads_last(grp["vbuf"]),
            sre=sre, sim=sim, dn_s=dn_s, dn_c=dn_c, ffn_c=ffn_c)
    p, s = res["p"], res["s"]
    return (p["y"], s["y"], p["k"], p["v"], s["k"], s["v"], p["sre"], p["sim"], s["sre"], s["sim"],
            p["dn_s"], s["dn_s"], p["dn_c"], s["dn_c"], p["ffn_c"], s["ffn_c"])
```

```python
import functools
import math

import jax
import jax.numpy as jnp
from jax import lax
from jax.experimental import pallas as pl
from jax.experimental.pallas import tpu as pltpu

F32 = jnp.float32
BF16 = jnp.bfloat16
HIGHEST = lax.Precision.HIGHEST
NORM_EPS = 1e-6
LANES = 128
SUBLANES = 8
BF16_ROWS = 16
VMEM_LIMIT_BYTES = 56 * 1024 * 1024
DN_CHUNK = 64
TRI_BASE = 16

_NT = (((1,), (1,)), ((), ()))
_TN = (((0,), (0,)), ((), ()))


def _tile(dim, pref, mult=LANES):
    if dim <= pref:
        return dim
    t = (pref // mult) * mult
    while t >= mult:
        if dim % t == 0:
            return t
        t -= mult
    raise ValueError(f"no tile for {dim} (pref {pref}, mult {mult})")


def _params(*sem):
    return pltpu.CompilerParams(dimension_semantics=sem, vmem_limit_bytes=VMEM_LIMIT_BYTES)


def _mm(a, b, dims=None):
    ok = all(d % BF16_ROWS == 0 for d in a.shape + b.shape)
    if ok:
        a, b = a.astype(BF16), b.astype(BF16)
    else:
        a, b = a.astype(F32), b.astype(F32)
    if dims is None:
        return jnp.dot(a, b, preferred_element_type=F32)
    return lax.dot_general(a, b, dims, preferred_element_type=F32)


def _mm_f32(a, b, dims=None):
    if dims is None:
        return jnp.dot(a, b, preferred_element_type=F32, precision=HIGHEST)
    return lax.dot_general(a, b, dims, preferred_element_type=F32, precision=HIGHEST)


def _softplus(z):
    return jnp.maximum(z, 0.0) + jnp.log1p(jnp.exp(-jnp.abs(z)))


def _sigmoid(z):
    return 1.0 / (1.0 + jnp.exp(-z))


def _rms(x, gain, eps=NORM_EPS):
    return x * lax.rsqrt(jnp.mean(x * x, axis=-1, keepdims=True) + eps) * gain


def _inproj_kernel(x_ref, g_ref, w_ref, gs_ref, e_ref, qg_ref, kg_ref, *refs, head_dim, q_scale, aliased,
                   time_minor):
    q_ref, k_ref, v_ref, r_ref, xn_ref = refs[2:] if aliased else refs
    j = pl.program_id(1)
    kv_layout = (lambda a: a.T) if time_minor else (lambda a: a)

    @pl.when(j == 0)
    def _():
        xn_ref[...] = _rms(x_ref[...], g_ref[...]).astype(BF16)

    y = jnp.dot(xn_ref[...], w_ref[...], preferred_element_type=F32)

    def head_norm(gain):
        ms = jnp.dot((y * y).astype(BF16), gs_ref[...], preferred_element_type=F32) * (1.0 / head_dim)
        inv = lax.rsqrt(ms + NORM_EPS)
        hi = inv.astype(BF16)
        lo = (inv - hi.astype(F32)).astype(BF16)
        sc = (jnp.dot(hi, e_ref[...], preferred_element_type=F32)
              + jnp.dot(lo, e_ref[...], preferred_element_type=F32))
        return y * sc * gain

    @pl.when(j == 0)
    def _():
        q_ref[...] = (head_norm(qg_ref[...]) * q_scale).astype(BF16)

    @pl.when(j == 1)
    def _():
        k_ref[...] = kv_layout(head_norm(kg_ref[...]))

    @pl.when(j == 2)
    def _():
        v_ref[...] = kv_layout(y)

    @pl.when(j >= 3)
    def _():
        r_ref[...] = y


def _inproj(x2d, gain, w_perm, gs, e, qg, kg, kbuf, vbuf, layer, depth, head_dim, q_scale, seq, time_minor):
    m, d = x2d.shape
    tn = gs.shape[0]
    assert w_perm.shape[1:] == (d, tn)
    n_rest = (w_perm.shape[0] - 3) * tn
    tm = _tile(seq, 512, LANES) if time_minor else _tile(m, 512, SUBLANES)
    tiles_per_seq = seq // tm
    aliased = kbuf is not None
    const = lambda i, j: (0, 0)
    in_specs = [
        pl.BlockSpec((tm, d), lambda i, j: (i, 0)),
        pl.BlockSpec((1, d), const),
        pl.BlockSpec((None, d, tn), lambda i, j: (j, 0, 0)),
        pl.BlockSpec((tn, LANES), const),
        pl.BlockSpec((LANES, tn), const),
        pl.BlockSpec((1, tn), const),
        pl.BlockSpec((1, tn), const),
    ]
    args = [x2d, gain, w_perm, gs, e, qg, kg]
    aliases = {}
    if aliased:
        in_specs += [pl.BlockSpec(memory_space=pl.ANY), pl.BlockSpec(memory_space=pl.ANY)]
        args += [kbuf, vbuf]
        aliases = {7: 1, 8: 2}
    if time_minor:
        kv_spec = pl.BlockSpec((None, None, tn, tm), lambda i, j: (layer, i // tiles_per_seq, 0, i % tiles_per_seq))
        kv_shape = jax.ShapeDtypeStruct((depth, m // seq, tn, seq), F32)
    else:
        kv_spec = pl.BlockSpec((None, tm, tn), lambda i, j: (layer, i, 0))
        kv_shape = jax.ShapeDtypeStruct((depth, m, tn), F32)
    out_specs = [
        pl.BlockSpec((tm, tn), lambda i, j: (i, 0)),
        kv_spec,
        kv_spec,
        pl.BlockSpec((tm, tn), lambda i, j: (i, jnp.maximum(j - 3, 0))),
    ]
    out_shape = [jax.ShapeDtypeStruct((m, tn), BF16), kv_shape, kv_shape, jax.ShapeDtypeStruct((m, n_rest), F32)]
    return pl.pallas_call(
        functools.partial(_inproj_kernel, head_dim=head_dim, q_scale=q_scale, aliased=aliased,
                          time_minor=time_minor),
        grid=(m // tm, 3 + n_rest // tn),
        in_specs=in_specs, out_specs=out_specs, out_shape=out_shape,
        scratch_shapes=[pltpu.VMEM((tm, d), BF16)],
        input_output_aliases=aliases,
        compiler_params=_params("parallel", "arbitrary"),
        name="inproj",
    )(*args)


LOG2E = math.log2(math.e)


def _softplus2(zs):
    neg_abs = lax.bitcast_convert_type(lax.bitcast_convert_type(zs, jnp.uint32) | jnp.uint32(0x80000000), F32)
    return jnp.maximum(zs, 0.0) + jnp.log2(1.0 + jnp.exp2(neg_abs))


SB_BLOCK = 256
SB_ROW_SUBBLOCKS = 4


def _sb_prompt_kernel(bias_ref, q_ref, k_ref, v_ref, o_ref, *, blk, nsub, head_dim, heads_per_step):
    hp = pl.program_id(1)
    qi = pl.program_id(2)
    row = lax.broadcasted_iota(jnp.int32, (blk, blk), 0)
    col = lax.broadcasted_iota(jnp.int32, (blk, blk), 1)
    later = (row > col).astype(BF16)
    visible = col < row
    q = q_ref[...]
    heads = range(heads_per_step)
    lanes = [slice(hh * head_dim, (hh + 1) * head_dim) for hh in heads]
    qs = [[q[j * blk:(j + 1) * blk, lanes[hh]] for j in range(nsub)] for hh in heads]
    biases = [bias_ref[hp * heads_per_step + hh] * LOG2E for hh in heads]

    def load(ki):
        r0 = pl.multiple_of(ki * blk, blk)
        kb = k_ref[:, pl.ds(r0, blk)].astype(BF16)
        vb = v_ref[:, pl.ds(r0, blk)].astype(BF16)
        return [(kb[lanes[hh], :], vb[lanes[hh], :]) for hh in heads]

    def block(qh, kv, bias, oc, masked):
        o_acc, carry = oc
        z = jnp.dot(qh, kv[0], preferred_element_type=F32) + bias
        sp = _softplus2(z)
        if masked:
            sp = jnp.where(visible, sp, 0.0)
        after = jnp.dot(sp.astype(BF16), later, preferred_element_type=F32)
        w = jnp.exp2(z - sp - after - carry)
        if masked:
            w = jnp.where(visible, w, 0.0)
        o_acc = o_acc + lax.dot_general(w.astype(BF16), kv[1], _NT, preferred_element_type=F32)
        return o_acc, carry + after[:, 0:1] + sp[:, 0:1]

    zero = (jnp.zeros((blk, head_dim), F32), jnp.zeros((blk, 1), F32))
    state = [[zero for _ in range(nsub)] for _ in heads]
    for d in reversed(range(nsub)):
        kv = load(qi * nsub + d)
        for hh in heads:
            for j in range(d, nsub):
                state[hh][j] = block(qs[hh][j], kv[hh], biases[hh], state[hh][j], j == d)

    def older(it, flat):
        kv = load(qi * nsub - 1 - it)
        return tuple(block(qs[hh][j], kv[hh], biases[hh], flat[hh * nsub + j], False)
                     for hh in heads for j in range(nsub))

    flat = lax.fori_loop(0, qi * nsub, older, tuple(state[hh][j] for hh in heads for j in range(nsub)))
    o_ref[...] = jnp.concatenate(
        [jnp.concatenate([flat[hh * nsub + j][0] for hh in heads], axis=-1) for j in range(nsub)], axis=0)


def _sb_prompt(q, kbuf, vbuf, bias, layer, bsz, seq, head_dim):
    width = q.shape[-1]
    heads_per_step = LANES // head_dim
    blk = _tile(seq, SB_BLOCK)
    nsub = max(n for n in (1, 2, 4, 8) if n <= SB_ROW_SUBBLOCKS and seq % (n * blk) == 0)
    rows = nsub * blk
    q3 = q.reshape(bsz, seq, width)
    return pl.pallas_call(
        functools.partial(_sb_prompt_kernel, blk=blk, nsub=nsub, head_dim=head_dim, heads_per_step=heads_per_step),
        grid=(bsz, width // LANES, seq // rows),
        in_specs=[
            pl.BlockSpec(memory_space=pltpu.SMEM),
            pl.BlockSpec((None, rows, LANES), lambda b, h, i: (b, i, h)),
            pl.BlockSpec((None, None, LANES, seq), lambda b, h, i: (layer, b, h, 0)),
            pl.BlockSpec((None, None, LANES, seq), lambda b, h, i: (layer, b, h, 0)),
        ],
        out_specs=pl.BlockSpec((None, rows, LANES), lambda b, h, i: (b, i, h)),
        out_shape=jax.ShapeDtypeStruct((bsz, seq, width), F32),
        compiler_params=_params("parallel", "parallel", "arbitrary"),
        name="sb_prompt",
    )(bias, q3, kbuf, vbuf).reshape(bsz * seq, width)


SB_PAGE_GROUP = 4


def _sb_sample_kernel(pt_ref, q_ref, bcol_ref, kn_ref, vn_ref, *refs, pages_per_step, tq, heads, head_dim):
    del pt_ref
    kp = refs[:pages_per_step]
    vp = refs[pages_per_step:2 * pages_per_step]
    o_ref, acc_ref, carry_ref = refs[2 * pages_per_step:]
    c = pl.program_id(1)
    nr = heads * tq
    ltq = tq.bit_length() - 1
    page = kp[0].shape[-1]
    qf = q_ref[...].astype(F32)
    width = qf.shape[-1]
    bcol = bcol_ref[...] * LOG2E
    r = lax.broadcasted_iota(jnp.int32, (nr, width), 0)
    col = lax.broadcasted_iota(jnp.int32, (nr, width), 1)
    own_head = lax.shift_right_logical(r, ltq) == lax.shift_right_logical(col, head_dim.bit_length() - 1)
    qbd = jnp.where(own_head, jnp.concatenate([qf] * heads, axis=0), 0.0)

    @pl.when(c == 0)
    def _():
        z = lax.dot_general(qbd, kn_ref[...], _NT, preferred_element_type=F32) + bcol
        key = lax.broadcasted_iota(jnp.int32, (nr, tq), 1)
        qry = lax.broadcasted_iota(jnp.int32, (nr, tq), 0) & (tq - 1)
        vis = key < qry
        sp = jnp.where(vis, _softplus2(z), 0.0)
        later = (lax.broadcasted_iota(jnp.int32, (tq, tq), 0)
                 > lax.broadcasted_iota(jnp.int32, (tq, tq), 1)).astype(F32)
        w = jnp.where(vis, jnp.exp2(z - sp - _mm_f32(sp, later)), 0.0)
        acc_ref[...] = jnp.dot(w, vn_ref[...], preferred_element_type=F32)
        carry_ref[...] = jnp.sum(sp, axis=-1, keepdims=True)

    qbd16 = qbd.astype(BF16)
    gsz = SB_PAGE_GROUP if pages_per_step % SB_PAGE_GROUP == 0 else 1
    nk = gsz * page
    later = (lax.broadcasted_iota(jnp.int32, (nk, nk), 0)
             > lax.broadcasted_iota(jnp.int32, (nk, nk), 1)).astype(BF16)
    side_by_side = lambda refs: jnp.concatenate(
        [r[...].reshape(width, page).astype(BF16) for r in reversed(refs)], axis=1)
    for g in range(pages_per_step // gsz):
        kt = side_by_side(kp[g * gsz:(g + 1) * gsz])
        z = jnp.dot(qbd16, kt, preferred_element_type=F32) + bcol
        sp = _softplus2(z)
        after = jnp.dot(sp.astype(BF16), later, preferred_element_type=F32)
        carry = carry_ref[...]
        w = jnp.exp2(z - sp - after - carry).astype(BF16)
        vt = side_by_side(vp[g * gsz:(g + 1) * gsz])
        acc_ref[...] += lax.dot_general(w, vt, _NT, preferred_element_type=F32)
        carry_ref[...] = carry + after[:, 0:1] + sp[:, 0:1]

    @pl.when(c == pl.num_programs(1) - 1)
    def _():
        acc = jnp.where(own_head, acc_ref[...], 0.0)
        o_ref[...] = acc.reshape(heads, tq, width).sum(axis=0)


def _sb_sample(q, kbuf, vbuf, cache_k, cache_v, page_table, bias, layer, bsz, tq, head_dim):
    width = q.shape[-1]
    heads = width // head_dim
    assert tq & (tq - 1) == 0 and head_dim & (head_dim - 1) == 0
    page = cache_k.shape[2]
    npages = page_table.shape[1]
    pps = _tile(npages, 8, 1)
    bcol = jnp.repeat(bias.astype(F32), tq)[:, None]
    cache_k = cache_k.transpose(0, 1, 3, 4, 2)
    cache_v = cache_v.transpose(0, 1, 3, 4, 2)

    def page_map(p):
        return lambda b, c, pt: (layer, pt[b * npages + npages - 1 - (c * pps + p)], 0, 0, 0)

    page_specs = [pl.BlockSpec((None, None, heads, head_dim, page), page_map(p)) for p in range(pps)]
    grid_spec = pltpu.PrefetchScalarGridSpec(
        num_scalar_prefetch=1,
        grid=(bsz, npages // pps),
        in_specs=[
            pl.BlockSpec((None, tq, width), lambda b, c, pt: (b, 0, 0)),
            pl.BlockSpec((heads * tq, 1), lambda b, c, pt: (0, 0)),
            pl.BlockSpec((None, None, tq, width), lambda b, c, pt: (layer, b, 0, 0)),
            pl.BlockSpec((None, None, tq, width), lambda b, c, pt: (layer, b, 0, 0)),
        ] + page_specs + page_specs,
        out_specs=pl.BlockSpec((None, tq, width), lambda b, c, pt: (b, 0, 0)),
        scratch_shapes=[pltpu.VMEM((heads * tq, width), F32), pltpu.VMEM((heads * tq, 1), F32)],
    )
    k4 = kbuf.reshape(kbuf.shape[0], bsz, tq, width)
    v4 = vbuf.reshape(vbuf.shape[0], bsz, tq, width)
    out = pl.pallas_call(
        functools.partial(_sb_sample_kernel, pages_per_step=pps, tq=tq, heads=heads, head_dim=head_dim),
        grid_spec=grid_spec,
        out_shape=jax.ShapeDtypeStruct((bsz, tq, width), F32),
        compiler_params=_params("parallel", "arbitrary"),
        name="sb_sample",
    )(page_table.reshape(-1), q.reshape(bsz, tq, width), bcol, k4, v4,
      *([cache_k] * pps), *([cache_v] * pps))
    return out.reshape(bsz * tq, width)


def _s5_discretize_kernel(lre_ref, lim_ref, ls_ref, bre_ref, bim_ref, are_ref, aim_ref, bbre_ref, bbim_ref):
    lam_re, lam_im = lre_ref[...], lim_ref[...]
    dt = jnp.exp(ls_ref[...])
    mag = jnp.exp(lam_re * dt)
    ab_re = mag * jnp.cos(lam_im * dt)
    ab_im = mag * jnp.sin(lam_im * dt)
    den = lam_re * lam_re + lam_im * lam_im
    xr = ab_re - 1.0
    f_re = (xr * lam_re + ab_im * lam_im) / den
    f_im = (ab_im * lam_re - xr * lam_im) / den
    are_ref[...] = ab_re
    aim_ref[...] = ab_im
    bbre_ref[...] = f_re * bre_ref[...] - f_im * bim_ref[...]
    bbim_ref[...] = f_re * bim_ref[...] + f_im * bre_ref[...]


def _s5_discretize(lam_re, lam_im, log_step, b_re, b_im):
    g, n = lam_re.shape
    ch = b_re.shape[-1]
    col = lambda a: a.reshape(g * n, 1)
    ls = jnp.broadcast_to(log_step[:, None], (g, n))
    return pl.pallas_call(
        _s5_discretize_kernel,
        out_shape=[jax.ShapeDtypeStruct((g * n, 1), F32)] * 2 + [jax.ShapeDtypeStruct((g * n, ch), F32)] * 2,
        name="s5_discretize",
    )(col(lam_re), col(lam_im), col(ls), b_re.reshape(g * n, ch), b_im.reshape(g * n, ch))


def _s5_kernel(u_ref, bmat_ref, arow_ref, h0_ref, cmat_ref, d_ref, wglu_ref, bglu_ref,
               y_ref, hfin_ref, hbuf_ref, hc_ref, *, tt, nb, ns, cb):
    ti = pl.program_id(1)
    width = u_ref.shape[-1]

    @pl.when(ti == 0)
    def _():
        hc_ref[...] = h0_ref[...]

    u = u_ref[...].reshape(tt * nb, width)
    hbuf_ref[...] = jnp.dot(u.astype(BF16), bmat_ref[...], preferred_element_type=F32)

    for c in range(ns // cb):
        re = slice(c * cb, (c + 1) * cb)
        im = slice(ns + c * cb, ns + (c + 1) * cb)
        a_re = jnp.broadcast_to(arow_ref[:, re], (SUBLANES, cb))
        a_im = jnp.broadcast_to(arow_ref[:, im], (SUBLANES, cb))

        def batch_rows(bb, _):
            r0 = pl.multiple_of(bb * SUBLANES, SUBLANES)

            def step(t, h):
                h_re, h_im = h
                r = pl.multiple_of(t * nb + r0, SUBLANES)
                n_re = a_re * h_re - a_im * h_im + hbuf_ref[pl.ds(r, SUBLANES), re]
                n_im = a_re * h_im + a_im * h_re + hbuf_ref[pl.ds(r, SUBLANES), im]
                hbuf_ref[pl.ds(r, SUBLANES), re] = n_re
                hbuf_ref[pl.ds(r, SUBLANES), im] = n_im
                return n_re, n_im

            h = (hc_ref[pl.ds(r0, SUBLANES), re], hc_ref[pl.ds(r0, SUBLANES), im])
            h = lax.fori_loop(0, tt, step, h, unroll=min(tt, 8))
            hc_ref[pl.ds(r0, SUBLANES), re] = h[0]
            hc_ref[pl.ds(r0, SUBLANES), im] = h[1]
            return 0

        lax.fori_loop(0, nb // SUBLANES, batch_rows, 0)

    y = jnp.dot(hbuf_ref[...].astype(BF16), cmat_ref[...], preferred_element_type=F32) + d_ref[...] * u
    y = 0.5 * y * (1.0 + jnp.tanh(math.sqrt(2.0 / math.pi) * (y + 0.044715 * (y * y * y))))
    z = jnp.dot(y.astype(BF16), wglu_ref[...], preferred_element_type=F32) + bglu_ref[...]
    y_ref[...] = (y * _sigmoid(z)).reshape(tt, nb, width)

    @pl.when(ti == pl.num_programs(1) - 1)
    def _():
        hfin_ref[...] = hc_ref[...]


def _s5(u_tm, bmat, arow, h0, cmat, d_row, wglu, bglu):
    t, b, width = u_tm.shape
    ns2 = bmat.shape[1]
    nb = _tile(b, 64, SUBLANES)
    tt = _tile(t, max(SUBLANES, 512 // nb), 1)
    const = lambda i, j: (0, 0)
    return pl.pallas_call(
        functools.partial(_s5_kernel, tt=tt, nb=nb, ns=ns2 // 2, cb=_tile(ns2 // 2, 512)),
        grid=(b // nb, t // tt),
        in_specs=[
            pl.BlockSpec((tt, nb, width), lambda i, j: (j, i, 0)),
            pl.BlockSpec((width, ns2), const),
            pl.BlockSpec((1, ns2), const),
            pl.BlockSpec((nb, ns2), lambda i, j: (i, 0)),
            pl.BlockSpec((ns2, width), const),
            pl.BlockSpec((1, width), const),
            pl.BlockSpec((width, width), const),
            pl.BlockSpec((1, width), const),
        ],
        out_specs=[
            pl.BlockSpec((tt, nb, width), lambda i, j: (j, i, 0)),
            pl.BlockSpec((nb, ns2), lambda i, j: (i, 0)),
        ],
        out_shape=[jax.ShapeDtypeStruct((t, b, width), F32), jax.ShapeDtypeStruct((b, ns2), F32)],
        scratch_shapes=[pltpu.VMEM((tt * nb, ns2), F32), pltpu.VMEM((nb, ns2), F32)],
        compiler_params=_params("parallel", "arbitrary"),
        name="s5",
    )(u_tm, bmat, arow, h0, cmat, d_row, wglu, bglu)


def _split(a):
    hi = a.astype(BF16)
    return hi, (a - hi.astype(F32)).astype(BF16)


def _mm3(a, b):
    d = lambda x, y: jnp.dot(x, y, preferred_element_type=F32)
    return d(a[0], b[0]) + d(a[0], b[1]) + d(a[1], b[0])


def _block_tri_inverse(a, n, c):
    row = lax.broadcasted_iota(jnp.int32, (n, n), 0)
    col = lax.broadcasted_iota(jnp.int32, (n, n), 1)
    eye = (row == col).astype(F32)

    def same_block(b):
        s = b.bit_length() - 1
        return lax.shift_right_logical(row, s) == lax.shift_right_logical(col, s)

    b0 = min(c, TRI_BASE)
    neg = -jnp.where(same_block(b0), a, 0.0)
    t = eye + neg
    p = _split(neg)
    k = 1
    while 2 * k < b0:
        p = _split(_mm3(p, p))
        t = t + _mm3(_split(t), p)
        k *= 2
    b = b0
    while b < c:
        off = jnp.where(jnp.logical_and(same_block(2 * b), jnp.logical_not(same_block(b))), a, 0.0)
        ts = _split(t)
        t = t - _mm3(_split(_mm3(ts, _split(off))), ts)
        b *= 2
    return t


def _dn_kernel(q_ref, k_ref, v_ref, ba_ref, gate_ref, cw_ref, prm_ref, nrm_ref, cbuf_ref, s0_ref,
               o_ref, sout_ref, cout_ref,
               xpad_ref, qs_ref, ks_ref, vs_ref, gb_ref, s_ref, *, bs, tt, chunk, nh, hd, taps):
    ti = pl.program_id(1)
    width = nh * hd
    halo = SUBLANES
    first = halo - (taps - 1)

    @pl.when(ti == 0)
    def _():
        s_ref[...] = s0_ref[...]
        xpad_ref[:, first:halo, :] = cbuf_ref[...]

    xpad_ref[:, halo:halo + tt, 0:width] = q_ref[...]
    xpad_ref[:, halo:halo + tt, width:2 * width] = k_ref[...]
    xpad_ref[:, halo:halo + tt, 2 * width:3 * width] = v_ref[...]
    cw = cw_ref[...]
    y = cw[0:1, :] * xpad_ref[:, first:first + tt, :]
    for i in range(1, taps):
        y = y + cw[i:i + 1, :] * xpad_ref[:, first + i:first + i + tt, :]
    cout_ref[...] = xpad_ref[:, halo + tt - (taps - 1):halo + tt, :]
    xpad_ref[:, 0:halo, :] = xpad_ref[:, tt:tt + halo, :]
    act = y * _sigmoid(y)
    for h in range(nh):
        sl = slice(h * hd, (h + 1) * hd)
        qh = act[:, :, h * hd:(h + 1) * hd]
        kh = act[:, :, width + h * hd:width + (h + 1) * hd]
        qs_ref[:, :, sl] = qh * lax.rsqrt(jnp.sum(qh * qh, axis=-1, keepdims=True) + NORM_EPS) * (hd ** -0.5)
        ks_ref[:, :, sl] = kh * lax.rsqrt(jnp.sum(kh * kh, axis=-1, keepdims=True) + NORM_EPS)
    vs_ref[...] = act[:, :, 2 * width:3 * width]
    ba = ba_ref[...]
    lane = lax.broadcasted_iota(jnp.int32, ba.shape, 2)
    gval = -jnp.exp(prm_ref[0:1, :]) * _softplus(ba + prm_ref[1:2, :])
    gb_ref[...] = jnp.where(lane < nh, _sigmoid(ba), jnp.where(lane < 2 * nh, gval, 0.0))

    pairs = [(s, h) for s in range(bs) for h in range(nh)]
    npairs = len(pairs)
    nrow = npairs * chunk
    lc = chunk.bit_length() - 1
    row = lax.broadcasted_iota(jnp.int32, (nrow, nrow), 0)
    col = lax.broadcasted_iota(jnp.int32, (nrow, nrow), 1)
    same = lax.shift_right_logical(row, lc) == lax.shift_right_logical(col, lc)
    lower = jnp.logical_and(same, row >= col)
    strict = jnp.logical_and(same, row > col)
    tri = lower.astype(BF16)
    last_sel = col == (row | (chunk - 1))
    prow = lax.shift_right_logical(lax.broadcasted_iota(jnp.int32, (nrow, LANES), 0), lc)
    plane = lax.broadcasted_iota(jnp.int32, (nrow, LANES), 1)
    head_of_row = prow & (nh - 1)
    is_beta = plane == head_of_row
    is_g = plane == head_of_row + nh
    ones8 = jnp.ones((BF16_ROWS, LANES), BF16)
    pair_of_row = lax.shift_right_logical(lax.broadcasted_iota(jnp.int32, (nrow, hd), 0), lc)

    def expand(x):
        return jnp.concatenate([jnp.where(pair_of_row == p, x, 0.0) for p in range(npairs)], axis=1)

    def one_chunk(c):
        rows = pl.ds(c * chunk, chunk)
        stack = lambda ref: jnp.concatenate([ref[s, rows, h * hd:(h + 1) * hd] for s, h in pairs], axis=0)
        qst, kst, vst = stack(qs_ref), stack(ks_ref), stack(vs_ref)
        gbst = jnp.concatenate([gb_ref[s, rows, :] for s, _ in pairs], axis=0)
        g_hi, g_lo = _split(gbst)
        gcum_all = (jnp.dot(tri, g_hi, preferred_element_type=F32)
                    + jnp.dot(tri, g_lo, preferred_element_type=F32))
        gsel = jnp.where(is_g, gcum_all, 0.0)
        gc = jnp.sum(gsel, axis=-1, keepdims=True)
        beta = jnp.sum(jnp.where(is_beta, gbst, 0.0), axis=-1, keepdims=True)
        s_hi, s_lo = _split(gsel)
        grow = (lax.dot_general(ones8, s_hi, _NT, preferred_element_type=F32)
                + lax.dot_general(ones8, s_lo, _NT, preferred_element_type=F32))[0:1, :]
        decay = jnp.where(lower, jnp.exp(jnp.minimum(gc - grow, 0.0)), 0.0)
        kb = kst * beta
        a = _mm(kb, kst, _NT) * jnp.where(strict, decay, 0.0)
        tinv = _block_tri_inverse(a, nrow, chunk)
        u = _mm(tinv, vst * beta)
        w = _mm(tinv, kb * jnp.exp(gc))
        attn = _mm(qst, kst, _NT) * decay
        s_stack = jnp.concatenate([s_ref[s, h] for s, h in pairs], axis=0)
        v_new = u - _mm(expand(w), s_stack)
        o = _mm(expand(qst * jnp.exp(gc)), s_stack) + _mm(attn, v_new)
        g_last = jnp.sum(jnp.where(last_sel, grow, 0.0), axis=-1, keepdims=True)
        upd = _mm(expand(kst * jnp.exp(g_last - gc)), v_new, _TN)
        gt = stack(gate_ref)
        on = _rms(o, nrm_ref[...]) * (gt * _sigmoid(gt))
        for idx, (s, h) in enumerate(pairs):
            gl = g_last[idx * chunk:idx * chunk + 1, :]
            s_ref[s, h] = s_ref[s, h] * jnp.exp(gl) + upd[idx * hd:(idx + 1) * hd, :]
            o_ref[s, rows, h * hd:(h + 1) * hd] = on[idx * chunk:(idx + 1) * chunk, :]

    for c in range(tt // chunk):
        one_chunk(c)

    @pl.when(ti == pl.num_programs(1) - 1)
    def _():
        sout_ref[...] = s_ref[...]


def _deltanet(rest3, col0, conv_w, prm, nrm, cbuf, s0, nh, hd):
    nseq, t, _ = rest3.shape
    width = nh * hd
    taps = conv_w.shape[0]
    chunk = min(DN_CHUNK, t)
    assert t % chunk == 0 and chunk & (chunk - 1) == 0 and t >= taps - 1
    tt = _tile(t, 256, chunk)
    bs = _tile(nseq, 8, 1) if tt == t and t <= SUBLANES else 1
    ba_col = (col0 + 4) * (width // LANES)
    blk = lambda cidx: pl.BlockSpec((bs, tt, width), lambda i, j: (i, j, cidx))
    const = lambda i, j: (0, 0)
    return pl.pallas_call(
        functools.partial(_dn_kernel, bs=bs, tt=tt, chunk=chunk, nh=nh, hd=hd, taps=taps),
        grid=(nseq // bs, t // tt),
        in_specs=[
            blk(col0 + 1), blk(col0 + 2), blk(col0 + 3),
            pl.BlockSpec((bs, tt, LANES), lambda i, j: (i, j, ba_col)),
            blk(col0),
            pl.BlockSpec((taps, 3 * width), const),
            pl.BlockSpec((SUBLANES, LANES), const),
            pl.BlockSpec((1, hd), const),
            pl.BlockSpec((bs, taps - 1, 3 * width), lambda i, j: (i, 0, 0)),
            pl.BlockSpec((bs, nh, hd, hd), lambda i, j: (i, 0, 0, 0)),
        ],
        out_specs=[
            pl.BlockSpec((bs, tt, width), lambda i, j: (i, j, 0)),
            pl.BlockSpec((bs, nh, hd, hd), lambda i, j: (i, 0, 0, 0)),
            pl.BlockSpec((bs, taps - 1, 3 * width), lambda i, j: (i, 0, 0)),
        ],
        out_shape=[
            jax.ShapeDtypeStruct((nseq, t, width), F32),
            jax.ShapeDtypeStruct((nseq, nh, hd, hd), F32),
            jax.ShapeDtypeStruct((nseq, taps - 1, 3 * width), F32),
        ],
        scratch_shapes=[
            pltpu.VMEM((bs, tt + 2 * SUBLANES, 3 * width), F32),
            pltpu.VMEM((bs, tt, width), F32),
            pltpu.VMEM((bs, tt, width), F32),
            pltpu.VMEM((bs, tt, width), F32),
            pltpu.VMEM((bs, tt, LANES), F32),
            pltpu.VMEM((bs, nh, hd, hd), F32),
        ],
        compiler_params=_params("parallel", "arbitrary"),
        name="deltanet",
    )(rest3, rest3, rest3, rest3, rest3, conv_w, prm, nrm, cbuf, s0)


def _outproj_kernel(oa_ref, ob_ref, oc_ref, ga_ref, gb_ref, w_ref, x_ref, o_ref, mix_ref):
    wa, wb = oa_ref.shape[-1], ob_ref.shape[-1]

    @pl.when(pl.program_id(1) == 0)
    def _():
        mix_ref[:, 0:wa] = _rms(oa_ref[...], ga_ref[...]).astype(BF16)
        mix_ref[:, wa:wa + wb] = _rms(ob_ref[...], gb_ref[...]).astype(BF16)
        mix_ref[:, wa + wb:] = oc_ref[...].astype(BF16)

    o_ref[...] = x_ref[...] + jnp.dot(mix_ref[...], w_ref[...], preferred_element_type=F32)


def _outproj(oa, ob, oc, ga, gb, w, x2d):
    m, d = x2d.shape
    tm = _tile(m, 512, SUBLANES)
    _, d_in, tn = w.shape
    row = lambda a: pl.BlockSpec((tm, a.shape[-1]), lambda i, j: (i, 0))
    const = lambda a: pl.BlockSpec((1, a.shape[-1]), lambda i, j: (0, 0))
    return pl.pallas_call(
        _outproj_kernel,
        grid=(m // tm, d // tn),
        in_specs=[row(oa), row(ob), row(oc), const(ga), const(gb),
                  pl.BlockSpec((None, d_in, tn), lambda i, j: (j, 0, 0)),
                  pl.BlockSpec((tm, tn), lambda i, j: (i, j))],
        out_specs=pl.BlockSpec((tm, tn), lambda i, j: (i, j)),
        out_shape=jax.ShapeDtypeStruct((m, d), F32),
        scratch_shapes=[pltpu.VMEM((tm, d_in), BF16)],
        compiler_params=_params("parallel", "arbitrary"),
        name="outproj",
    )(oa, ob, oc, ga, gb, w, x2d)


FFN_SUBTILE = 256


def _ffn_kernel(x_ref, g_ref, wg_ref, wv_ref, wd_ref, cw_ref, st_ref, o_ref, buf_ref, h_ref, act_ref, carry_ref,
                *, nf, tiles_per_seq, per_seq_rows):
    i = pl.program_id(0)
    f = pl.program_id(1)
    tm, tf = act_ref.shape

    def up():
        h = h_ref[...]
        if not per_seq_rows:
            @pl.when((i % tiles_per_seq) == 0)
            def _():
                carry_ref[f] = st_ref[...]

        tc = FFN_SUBTILE if tf % FFN_SUBTILE == 0 else tf
        row = lax.broadcasted_iota(jnp.int32, (tm, tc), 0)
        for s in range(tf // tc):
            cols = slice(s * tc, (s + 1) * tc)
            gate = jnp.dot(h, wg_ref[:, cols], preferred_element_type=F32)
            val = jnp.dot(h, wv_ref[:, cols], preferred_element_type=F32)
            prev1 = pltpu.roll(gate, 1, axis=0)
            prev2 = pltpu.roll(gate, 2, axis=0)
            if per_seq_rows:
                nseq = tm // per_seq_rows
                expand = lambda a: jnp.broadcast_to(a, (nseq, per_seq_rows, tc)).reshape(tm, tc)
                s0, s1 = expand(st_ref[:, 0:1, cols]), expand(st_ref[:, 1:2, cols])
                t = row & (per_seq_rows - 1)
                buf_ref[:, :, cols] = gate.reshape(nseq, per_seq_rows, tc)[:, per_seq_rows - 2:, :]
            else:
                s0, s1 = carry_ref[f, 0:1, cols], carry_ref[f, 1:2, cols]
                t = row
                tail = gate[tm - 2:tm, :]
                carry_ref[f, :, cols] = tail
                buf_ref[:, cols] = tail
            x1 = jnp.where(t >= 1, prev1, s1)
            x2 = jnp.where(t >= 2, prev2, jnp.where(t == 1, s1, s0))
            conv = cw_ref[2:3, cols] * gate + cw_ref[1:2, cols] * x1 + cw_ref[0:1, cols] * x2
            act_ref[:, cols] = (conv * _sigmoid(conv) * val).astype(BF16)

    def down():
        o_ref[...] += jnp.dot(act_ref[...], wd_ref[...], preferred_element_type=F32)

    @pl.when(f == 0)
    def _():
        h_ref[...] = _rms(x_ref[...], g_ref[...]).astype(BF16)
        o_ref[...] = x_ref[...]
        up()

    @pl.when(jnp.logical_and(f > 0, f < nf))
    def _():
        down()
        up()

    @pl.when(f == nf)
    def _():
        down()


def _ffn(x2d, gain, wg, wv, wd, conv_w, state, seq):
    m, d = x2d.shape
    nf, _, tf = wg.shape
    d_ff = nf * tf
    nseq = m // seq
    assert conv_w.shape[0] == 3
    up_tile = lambda f: jnp.minimum(f, nf - 1)
    down_tile = lambda f: jnp.maximum(f - 1, 0)
    if seq <= SUBLANES:
        assert seq == SUBLANES
        tm = _tile(m, FFN_ROWS, SUBLANES)
        per_seq_rows, tiles_per_seq = seq, 1
        st_spec = buf_spec = pl.BlockSpec((tm // seq, 2, tf), lambda i, f: (i, 0, up_tile(f)))
        n_buf = nseq
    else:
        tm = _tile(seq, FFN_ROWS, SUBLANES)
        per_seq_rows, tiles_per_seq = 0, seq // tm
        st_spec = pl.BlockSpec((None, 2, tf), lambda i, f: (i // tiles_per_seq, 0, up_tile(f)))
        buf_spec = pl.BlockSpec((None, 2, tf), lambda i, f: (i, 0, up_tile(f)))
        n_buf = m // tm
    y, buf = pl.pallas_call(
        functools.partial(_ffn_kernel, nf=nf, tiles_per_seq=tiles_per_seq, per_seq_rows=per_seq_rows),
        grid=(m // tm, nf + 1),
        in_specs=[
            pl.BlockSpec((tm, d), lambda i, f: (i, 0), pipeline_mode=pl.Buffered(1)),
            pl.BlockSpec((1, d), lambda i, f: (0, 0)),
            pl.BlockSpec((None, d, tf), lambda i, f: (up_tile(f), 0, 0)),
            pl.BlockSpec((None, d, tf), lambda i, f: (up_tile(f), 0, 0)),
            pl.BlockSpec((tf, d), lambda i, f: (down_tile(f), 0)),
            pl.BlockSpec((3, tf), lambda i, f: (0, up_tile(f))),
            st_spec,
        ],
        out_specs=[pl.BlockSpec((tm, d), lambda i, f: (i, 0)), buf_spec],
        out_shape=[jax.ShapeDtypeStruct((m, d), F32), jax.ShapeDtypeStruct((n_buf, 2, d_ff), F32)],
        scratch_shapes=[pltpu.VMEM((tm, d), BF16), pltpu.VMEM((tm, tf), BF16), pltpu.VMEM((nf, 2, tf), F32)],
        compiler_params=_params("arbitrary", "arbitrary"),
        name="convffn",
    )(x2d, gain, wg, wv, wd, conv_w, state)
    if not per_seq_rows:
        buf = buf.reshape(nseq, tiles_per_seq, 2, d_ff)[:, tiles_per_seq - 1]
    return y, buf


FFN_TILE = 512
FFN_ROWS = 1024
OUT_TILE = 1024


def _col_tiles(w, tn):
    d, n = w.shape
    return w.reshape(d, n // tn, tn).transpose(1, 0, 2)


def _layer_weights(l, w_in, sb_q_norm, sb_k_norm, ssm, dn_a_log, dn_dt_bias, ssm_w_glu, w_out, ffn_w_in, ffn_w_out,
                   dims):
    sbw, ssmw, dnw, nh = dims["sbw"], dims["ssmw"], dims["dnw"], dims["dn_heads"]
    heads, hd = dims["sb_heads"], dims["sb_hd"]
    d = w_in.shape[1]
    o_u = 3 * sbw
    o_qkv = o_u + ssmw
    o_ba = o_qkv + 3 * dnw
    o_gate = o_ba + 2 * nh
    n_rest = -(-(ssmw + 4 * dnw + LANES) // sbw) * sbw
    pad = n_rest - (ssmw + 4 * dnw + 2 * nh)
    w = w_in[l]
    w_perm = jnp.concatenate([
        w[:, :o_u], w[:, o_u:o_qkv], w[:, o_gate:o_gate + dnw], w[:, o_qkv:o_ba], w[:, o_ba:o_gate],
        jnp.zeros((d, pad), w.dtype)], axis=1).astype(BF16)
    head_of = jnp.arange(sbw) // hd
    gs = (head_of[:, None] == jnp.arange(LANES)[None, :]).astype(BF16)
    qg = jnp.tile(sb_q_norm[l], heads)[None, :]
    kg = jnp.tile(sb_k_norm[l], heads)[None, :]
    a_re, a_im, bb_re, bb_im = ssm["disc"]
    g, n, ch = dims["ssm_groups"], dims["ssm_state"], dims["ssm_ch"]
    ns = g * n
    grp_of_state = jnp.arange(ns) // n
    grp_of_chan = jnp.arange(ssmw) // ch
    blockmask = grp_of_chan[:, None] == grp_of_state[None, :]
    expand_b = lambda bb: jnp.where(blockmask, jnp.tile(bb.T, (g, 1)), 0.0)
    bmat = jnp.concatenate([expand_b(bb_re), expand_b(bb_im)], axis=1).astype(BF16)
    arow = jnp.concatenate([a_re, a_im], axis=0).reshape(1, 2 * ns)
    expand_c = lambda c: jnp.where(blockmask.T, jnp.tile(c.transpose(0, 2, 1).reshape(ns, ch), (1, g)), 0.0)
    cmat = jnp.concatenate([expand_c(ssm["c_re"][l]), -expand_c(ssm["c_im"][l])], axis=0).astype(BF16)
    prm = jnp.zeros((SUBLANES, LANES), F32)
    prm = prm.at[0, nh:2 * nh].set(dn_a_log[l]).at[1, nh:2 * nh].set(dn_dt_bias[l])
    d_ff = ffn_w_in.shape[2] // 2
    tf = _tile(d_ff, FFN_TILE)
    return dict(
        w_perm=_col_tiles(w_perm, sbw), gs=gs, e=gs.T, qg=qg, kg=kg, bmat=bmat, arow=arow, cmat=cmat, prm=prm,
        wglu=ssm_w_glu[l].astype(BF16), w_out=_col_tiles(w_out[l].astype(BF16), _tile(w_out.shape[2], OUT_TILE)),
        wg=_col_tiles(ffn_w_in[l][:, :d_ff].astype(BF16), tf), wv=_col_tiles(ffn_w_in[l][:, d_ff:].astype(BF16), tf),
        wd=ffn_w_out[l].astype(BF16), n_rest=n_rest)


def kernel(x_prompt, x_sample, cache_k, cache_v, page_table, state_ssm_re, state_ssm_im, state_delta, state_delta_conv, state_ffn_conv, norm_mix, w_in, sb_q_norm, sb_k_norm, sb_logit_bias, ssm_lambda_re, ssm_lambda_im, ssm_log_step, ssm_b_re, ssm_b_im, ssm_c_re, ssm_c_im, ssm_d, ssm_w_glu, ssm_b_glu, dn_conv_w, dn_a_log, dn_dt_bias, dn_norm, norm_out_sb, norm_out_ssm, w_out, norm_ffn, ffn_w_in, ffn_conv_w, ffn_w_out):
    depth = w_in.shape[0]
    bp, tp, d = x_prompt.shape
    bd, td, _ = x_sample.shape
    sb_heads, sb_hd = cache_k.shape[-2], cache_k.shape[-1]
    g, n = ssm_lambda_re.shape[1:]
    ch = ssm_b_re.shape[-1]
    dn_heads, dn_hd = dn_a_log.shape[1], dn_norm.shape[1]
    dims = dict(sbw=sb_heads * sb_hd, sb_heads=sb_heads, sb_hd=sb_hd, ssmw=g * ch, ssm_groups=g, ssm_state=n,
                ssm_ch=ch, dnw=dn_heads * dn_hd, dn_heads=dn_heads)
    sbw, ssmw, dnw, ns = dims["sbw"], dims["ssmw"], dims["dnw"], g * n
    assert ssmw % dnw == 0 and dnw % LANES == 0 and sbw % LANES == 0 and LANES % sb_hd == 0
    d_ff = ffn_conv_w.shape[2]
    col0 = ssmw // dnw
    q_scale = sb_hd ** -0.5 * LOG2E

    groups = {
        "p": dict(x=x_prompt.reshape(bp * tp, d), b=bp, t=tp, kbuf=None, vbuf=None, outs=[]),
        "s": dict(x=x_sample.reshape(bd * td, d), b=bd, t=td, kbuf=None, vbuf=None, outs=[]),
    }
    for l in range(depth):
        disc = _s5_discretize(ssm_lambda_re[l], ssm_lambda_im[l], ssm_log_step[l], ssm_b_re[l], ssm_b_im[l])
        lw = _layer_weights(l, w_in, sb_q_norm, sb_k_norm, dict(disc=disc, c_re=ssm_c_re, c_im=ssm_c_im),
                            dn_a_log, dn_dt_bias, ssm_w_glu, w_out, ffn_w_in, ffn_w_out, dims)
        for name, grp in groups.items():
            b, t, x = grp["b"], grp["t"], grp["x"]
            q, kbuf, vbuf, rest = _inproj(x, norm_mix[l][None, :], lw["w_perm"], lw["gs"], lw["e"], lw["qg"],
                                          lw["kg"], grp["kbuf"], grp["vbuf"], l, depth, sb_hd, q_scale, t,
                                          time_minor=(name == "p"))
            grp["kbuf"], grp["vbuf"] = kbuf, vbuf
            rest3 = rest.reshape(b, t, lw["n_rest"])
            if name == "p":
                o_a = _sb_prompt(q, kbuf, vbuf, sb_logit_bias[l], l, b, t, sb_hd)
                h0 = jnp.zeros((b, 2 * ns), F32)
                dn_s0 = jnp.zeros((b, dn_heads, dn_hd, dn_hd), F32)
                dn_c0 = jnp.zeros((b, dn_conv_w.shape[1] - 1, 3 * dnw), F32)
                ffn_c0 = jnp.zeros((b, ffn_conv_w.shape[1] - 1, d_ff), F32)
            else:
                o_a = _sb_sample(q, kbuf, vbuf, cache_k, cache_v, page_table, sb_logit_bias[l], l, b, t, sb_hd)
                h0 = jnp.concatenate([state_ssm_re[l].reshape(b, ns), state_ssm_im[l].reshape(b, ns)], axis=1)
                dn_s0, dn_c0, ffn_c0 = state_delta[l], state_delta_conv[l], state_ffn_conv[l]
            u_tm = rest3[:, :, :ssmw].transpose(1, 0, 2)
            y_tm, h_fin = _s5(u_tm, lw["bmat"], lw["arow"], h0, lw["cmat"], ssm_d[l][None, :], lw["wglu"],
                              ssm_b_glu[l][None, :])
            o_b = y_tm.transpose(1, 0, 2).reshape(b * t, ssmw)
            o_c, dn_s, dn_c = _deltanet(rest3, col0, dn_conv_w[l], lw["prm"], dn_norm[l][None, :], dn_c0, dn_s0,
                                        dn_heads, dn_hd)
            x = _outproj(o_a, o_b, o_c.reshape(b * t, dnw), norm_out_sb[l][None, :], norm_out_ssm[l][None, :],
                         lw["w_out"], x)
            x, ffn_c = _ffn(x, norm_ffn[l][None, :], lw["wg"], lw["wv"], lw["wd"], ffn_conv_w[l], ffn_c0, t)
            grp["x"] = x
            grp["outs"].append((h_fin[:, :ns].reshape(b, g, n), h_fin[:, ns:].reshape(b, g, n), dn_s, dn_c, ffn_c))

    res = {}
    for name, grp in groups.items():
        b, t = grp["b"], grp["t"]
        sre, sim, dn_s, dn_c, ffn_c = [jnp.stack(z) for z in zip(*grp["outs"])]
        if name == "p":
            heads_last = lambda a: a.reshape(depth, b, sb_heads, sb_hd, t).transpose(0, 1, 4, 2, 3)
        else:
            heads_last = lambda a: a.reshape(depth, b, t, sb_heads, sb_hd)
        res[name] = dict(
            y=grp["x"].reshape(b, t, d), k=heads_last(grp["kbuf"]), v=heads_last(grp["vbuf"]),
            sre=sre, sim=sim, dn_s=dn_s, dn_c=dn_c, ffn_c=ffn_c)
    p, s = res["p"], res["s"]
    return (p["y"], s["y"], p["k"], p["v"], s["k"], s["v"], p["sre"], p["sim"], s["sre"], s["sim"],
            p["dn_s"], s["dn_s"], p["dn_c"], s["dn_c"], p["ffn_c"], s["ffn_c"])
```

```python
import functools
import math

import jax
import jax.numpy as jnp
from jax import lax
from jax.experimental import pallas as pl
from jax.experimental.pallas import tpu as pltpu

F32 = jnp.float32
BF16 = jnp.bfloat16
HIGHEST = lax.Precision.HIGHEST
NORM_EPS = 1e-6
LANES = 128
SUBLANES = 8
BF16_ROWS = 16
VMEM_LIMIT_BYTES = 56 * 1024 * 1024
DN_CHUNK = 64
TRI_BASE = 16

_NT = (((1,), (1,)), ((), ()))
_TN = (((0,), (0,)), ((), ()))


def _tile(dim, pref, mult=LANES):
    if dim <= pref:
        return dim
    t = (pref // mult) * mult
    while t >= mult:
        if dim % t == 0:
            return t
        t -= mult
    raise ValueError(f"no tile for {dim} (pref {pref}, mult {mult})")


def _params(*sem):
    return pltpu.CompilerParams(dimension_semantics=sem, vmem_limit_bytes=VMEM_LIMIT_BYTES)


def _mm(a, b, dims=None):
    ok = all(d % BF16_ROWS == 0 for d in a.shape + b.shape)
    if ok:
        a, b = a.astype(BF16), b.astype(BF16)
    else:
        a, b = a.astype(F32), b.astype(F32)
    if dims is None:
        return jnp.dot(a, b, preferred_element_type=F32)
    return lax.dot_general(a, b, dims, preferred_element_type=F32)


def _mm_f32(a, b, dims=None):
    if dims is None:
        return jnp.dot(a, b, preferred_element_type=F32, precision=HIGHEST)
    return lax.dot_general(a, b, dims, preferred_element_type=F32, precision=HIGHEST)


def _softplus(z):
    return jnp.maximum(z, 0.0) + jnp.log1p(jnp.exp(-jnp.abs(z)))


def _sigmoid(z):
    return 1.0 / (1.0 + jnp.exp(-z))


def _rms(x, gain, eps=NORM_EPS):
    return x * lax.rsqrt(jnp.mean(x * x, axis=-1, keepdims=True) + eps) * gain


INPROJ_ROWS = 512


def _inproj_kernel(x_ref, g_ref, w_ref, gs_ref, e_ref, qg_ref, kg_ref, *refs, head_dim, q_scale, aliased,
                   time_minor):
    q_ref, k_ref, v_ref, r_ref, xn_ref = refs[2:] if aliased else refs
    j = pl.program_id(1)
    kv_layout = (lambda a: a.T) if time_minor else (lambda a: a)

    @pl.when(j == 0)
    def _():
        xn_ref[...] = _rms(x_ref[...], g_ref[...]).astype(BF16)

    y = jnp.dot(xn_ref[...], w_ref[...], preferred_element_type=F32)

    def head_norm(gain):
        ms = jnp.dot((y * y).astype(BF16), gs_ref[...], preferred_element_type=F32) * (1.0 / head_dim)
        inv = lax.rsqrt(ms + NORM_EPS)
        hi = inv.astype(BF16)
        lo = (inv - hi.astype(F32)).astype(BF16)
        sc = (jnp.dot(hi, e_ref[...], preferred_element_type=F32)
              + jnp.dot(lo, e_ref[...], preferred_element_type=F32))
        return y * sc * gain

    @pl.when(j == 0)
    def _():
        q_ref[...] = (head_norm(qg_ref[...]) * q_scale).astype(BF16)

    @pl.when(j == 1)
    def _():
        k_ref[...] = kv_layout(head_norm(kg_ref[...]))

    @pl.when(j == 2)
    def _():
        v_ref[...] = kv_layout(y)

    @pl.when(j >= 3)
    def _():
        r_ref[...] = y


def _inproj(x2d, gain, w_perm, gs, e, qg, kg, kbuf, vbuf, layer, depth, head_dim, q_scale, seq, time_minor):
    m, d = x2d.shape
    tn = gs.shape[0]
    assert w_perm.shape[1:] == (d, tn)
    n_rest = (w_perm.shape[0] - 3) * tn
    tm = _tile(seq, INPROJ_ROWS, LANES) if time_minor else _tile(m, INPROJ_ROWS, SUBLANES)
    tiles_per_seq = seq // tm
    aliased = kbuf is not None
    const = lambda i, j: (0, 0)
    in_specs = [
        pl.BlockSpec((tm, d), lambda i, j: (i, 0), pipeline_mode=pl.Buffered(1)),
        pl.BlockSpec((1, d), const),
        pl.BlockSpec((None, d, tn), lambda i, j: (j, 0, 0)),
        pl.BlockSpec((tn, LANES), const),
        pl.BlockSpec((LANES, tn), const),
        pl.BlockSpec((1, tn), const),
        pl.BlockSpec((1, tn), const),
    ]
    args = [x2d, gain, w_perm, gs, e, qg, kg]
    aliases = {}
    if aliased:
        in_specs += [pl.BlockSpec(memory_space=pl.ANY), pl.BlockSpec(memory_space=pl.ANY)]
        args += [kbuf, vbuf]
        aliases = {7: 1, 8: 2}
    if time_minor:
        kv_spec = pl.BlockSpec((None, None, tn, tm), lambda i, j: (layer, i // tiles_per_seq, 0, i % tiles_per_seq))
        kv_shape = jax.ShapeDtypeStruct((depth, m // seq, tn, seq), F32)
    else:
        kv_spec = pl.BlockSpec((None, tm, tn), lambda i, j: (layer, i, 0))
        kv_shape = jax.ShapeDtypeStruct((depth, m, tn), F32)
    out_specs = [
        pl.BlockSpec((tm, tn), lambda i, j: (i, 0)),
        kv_spec,
        kv_spec,
        pl.BlockSpec((tm, tn), lambda i, j: (i, jnp.maximum(j - 3, 0))),
    ]
    out_shape = [jax.ShapeDtypeStruct((m, tn), BF16), kv_shape, kv_shape, jax.ShapeDtypeStruct((m, n_rest), F32)]
    return pl.pallas_call(
        functools.partial(_inproj_kernel, head_dim=head_dim, q_scale=q_scale, aliased=aliased,
                          time_minor=time_minor),
        grid=(m // tm, 3 + n_rest // tn),
        in_specs=in_specs, out_specs=out_specs, out_shape=out_shape,
        scratch_shapes=[pltpu.VMEM((tm, d), BF16)],
        input_output_aliases=aliases,
        compiler_params=_params("parallel", "arbitrary"),
        name="inproj",
    )(*args)


LOG2E = math.log2(math.e)


def _softplus2(zs):
    neg_abs = lax.bitcast_convert_type(lax.bitcast_convert_type(zs, jnp.uint32) | jnp.uint32(0x80000000), F32)
    return jnp.maximum(zs, 0.0) + jnp.log2(1.0 + jnp.exp2(neg_abs))


SB_BLOCK = 256
SB_ROW_SUBBLOCKS = 4


def _sb_prompt_kernel(bias_ref, q_ref, k_ref, v_ref, o_ref, *, blk, nsub, head_dim, heads_per_step):
    hp = pl.program_id(1)
    qi = pl.program_id(2)
    row = lax.broadcasted_iota(jnp.int32, (blk, blk), 0)
    col = lax.broadcasted_iota(jnp.int32, (blk, blk), 1)
    later = (row > col).astype(BF16)
    visible = col < row
    q = q_ref[...]
    heads = range(heads_per_step)
    lanes = [slice(hh * head_dim, (hh + 1) * head_dim) for hh in heads]
    qs = [[q[j * blk:(j + 1) * blk, lanes[hh]] for j in range(nsub)] for hh in heads]
    biases = [bias_ref[hp * heads_per_step + hh] * LOG2E for hh in heads]

    def load(ki):
        r0 = pl.multiple_of(ki * blk, blk)
        kb = k_ref[:, pl.ds(r0, blk)].astype(BF16)
        vb = v_ref[:, pl.ds(r0, blk)].astype(BF16)
        return [(kb[lanes[hh], :], vb[lanes[hh], :]) for hh in heads]

    def block(qh, kv, bias, oc, masked):
        o_acc, carry = oc
        z = jnp.dot(qh, kv[0], preferred_element_type=F32) + bias
        sp = _softplus2(z)
        if masked:
            sp = jnp.where(visible, sp, 0.0)
        after = jnp.dot(sp.astype(BF16), later, preferred_element_type=F32)
        w = jnp.exp2(z - sp - after - carry)
        if masked:
            w = jnp.where(visible, w, 0.0)
        o_acc = o_acc + lax.dot_general(w.astype(BF16), kv[1], _NT, preferred_element_type=F32)
        return o_acc, carry + after[:, 0:1] + sp[:, 0:1]

    zero = (jnp.zeros((blk, head_dim), F32), jnp.zeros((blk, 1), F32))
    state = [[zero for _ in range(nsub)] for _ in heads]
    for d in reversed(range(nsub)):
        kv = load(qi * nsub + d)
        for hh in heads:
            for j in range(d, nsub):
                state[hh][j] = block(qs[hh][j], kv[hh], biases[hh], state[hh][j], j == d)

    def older(it, flat):
        kv = load(qi * nsub - 1 - it)
        return tuple(block(qs[hh][j], kv[hh], biases[hh], flat[hh * nsub + j], False)
                     for hh in heads for j in range(nsub))

    flat = lax.fori_loop(0, qi * nsub, older, tuple(state[hh][j] for hh in heads for j in range(nsub)))
    o_ref[...] = jnp.concatenate(
        [jnp.concatenate([flat[hh * nsub + j][0] for hh in heads], axis=-1) for j in range(nsub)], axis=0)


def _sb_prompt(q, kbuf, vbuf, bias, layer, bsz, seq, head_dim):
    width = q.shape[-1]
    heads_per_step = LANES // head_dim
    blk = _tile(seq, SB_BLOCK)
    nsub = max(n for n in (1, 2, 4, 8) if n <= SB_ROW_SUBBLOCKS and seq % (n * blk) == 0)
    rows = nsub * blk
    q3 = q.reshape(bsz, seq, width)
    return pl.pallas_call(
        functools.partial(_sb_prompt_kernel, blk=blk, nsub=nsub, head_dim=head_dim, heads_per_step=heads_per_step),
        grid=(bsz, width // LANES, seq // rows),
        in_specs=[
            pl.BlockSpec(memory_space=pltpu.SMEM),
            pl.BlockSpec((None, rows, LANES), lambda b, h, i: (b, i, h)),
            pl.BlockSpec((None, None, LANES, seq), lambda b, h, i: (layer, b, h, 0)),
            pl.BlockSpec((None, None, LANES, seq), lambda b, h, i: (layer, b, h, 0)),
        ],
        out_specs=pl.BlockSpec((None, rows, LANES), lambda b, h, i: (b, i, h)),
        out_shape=jax.ShapeDtypeStruct((bsz, seq, width), F32),
        compiler_params=_params("parallel", "parallel", "arbitrary"),
        name="sb_prompt",
    )(bias, q3, kbuf, vbuf).reshape(bsz * seq, width)


SB_PAGE_GROUP = 4


def _sb_sample_kernel(pt_ref, q_ref, bcol_ref, kn_ref, vn_ref, *refs, pages_per_step, tq, heads, head_dim):
    del pt_ref
    kp = refs[:pages_per_step]
    vp = refs[pages_per_step:2 * pages_per_step]
    o_ref, acc_ref, carry_ref = refs[2 * pages_per_step:]
    c = pl.program_id(1)
    nr = heads * tq
    ltq = tq.bit_length() - 1
    page = kp[0].shape[-1]
    qf = q_ref[...].astype(F32)
    width = qf.shape[-1]
    bcol = bcol_ref[...] * LOG2E
    r = lax.broadcasted_iota(jnp.int32, (nr, width), 0)
    col = lax.broadcasted_iota(jnp.int32, (nr, width), 1)
    own_head = lax.shift_right_logical(r, ltq) == lax.shift_right_logical(col, head_dim.bit_length() - 1)
    qbd = jnp.where(own_head, jnp.concatenate([qf] * heads, axis=0), 0.0)

    @pl.when(c == 0)
    def _():
        z = lax.dot_general(qbd, kn_ref[...], _NT, preferred_element_type=F32) + bcol
        key = lax.broadcasted_iota(jnp.int32, (nr, tq), 1)
        qry = lax.broadcasted_iota(jnp.int32, (nr, tq), 0) & (tq - 1)
        vis = key < qry
        sp = jnp.where(vis, _softplus2(z), 0.0)
        later = (lax.broadcasted_iota(jnp.int32, (tq, tq), 0)
                 > lax.broadcasted_iota(jnp.int32, (tq, tq), 1)).astype(F32)
        w = jnp.where(vis, jnp.exp2(z - sp - _mm_f32(sp, later)), 0.0)
        acc_ref[...] = jnp.dot(w, vn_ref[...], preferred_element_type=F32)
        carry_ref[...] = jnp.sum(sp, axis=-1, keepdims=True)

    qbd16 = qbd.astype(BF16)
    gsz = SB_PAGE_GROUP if pages_per_step % SB_PAGE_GROUP == 0 else 1
    nk = gsz * page
    later = (lax.broadcasted_iota(jnp.int32, (nk, nk), 0)
             > lax.broadcasted_iota(jnp.int32, (nk, nk), 1)).astype(BF16)
    side_by_side = lambda refs: jnp.concatenate(
        [r[...].reshape(width, page).astype(BF16) for r in reversed(refs)], axis=1)
    for g in range(pages_per_step // gsz):
        kt = side_by_side(kp[g * gsz:(g + 1) * gsz])
        z = jnp.dot(qbd16, kt, preferred_element_type=F32) + bcol
        sp = _softplus2(z)
        after = jnp.dot(sp.astype(BF16), later, preferred_element_type=F32)
        carry = carry_ref[...]
        w = jnp.exp2(z - sp - after - carry).astype(BF16)
        vt = side_by_side(vp[g * gsz:(g + 1) * gsz])
        acc_ref[...] += lax.dot_general(w, vt, _NT, preferred_element_type=F32)
        carry_ref[...] = carry + after[:, 0:1] + sp[:, 0:1]

    @pl.when(c == pl.num_programs(1) - 1)
    def _():
        acc = jnp.where(own_head, acc_ref[...], 0.0)
        o_ref[...] = acc.reshape(heads, tq, width).sum(axis=0)


def _sb_sample(q, kbuf, vbuf, cache_k, cache_v, page_table, bias, layer, bsz, tq, head_dim):
    width = q.shape[-1]
    heads = width // head_dim
    assert tq & (tq - 1) == 0 and head_dim & (head_dim - 1) == 0
    page = cache_k.shape[2]
    npages = page_table.shape[1]
    pps = _tile(npages, 8, 1)
    bcol = jnp.repeat(bias.astype(F32), tq)[:, None]
    cache_k = cache_k.transpose(0, 1, 3, 4, 2)
    cache_v = cache_v.transpose(0, 1, 3, 4, 2)

    def page_map(p):
        return lambda b, c, pt: (layer, pt[b * npages + npages - 1 - (c * pps + p)], 0, 0, 0)

    page_specs = [pl.BlockSpec((None, None, heads, head_dim, page), page_map(p)) for p in range(pps)]
    grid_spec = pltpu.PrefetchScalarGridSpec(
        num_scalar_prefetch=1,
        grid=(bsz, npages // pps),
        in_specs=[
            pl.BlockSpec((None, tq, width), lambda b, c, pt: (b, 0, 0)),
            pl.BlockSpec((heads * tq, 1), lambda b, c, pt: (0, 0)),
            pl.BlockSpec((None, None, tq, width), lambda b, c, pt: (layer, b, 0, 0)),
            pl.BlockSpec((None, None, tq, width), lambda b, c, pt: (layer, b, 0, 0)),
        ] + page_specs + page_specs,
        out_specs=pl.BlockSpec((None, tq, width), lambda b, c, pt: (b, 0, 0)),
        scratch_shapes=[pltpu.VMEM((heads * tq, width), F32), pltpu.VMEM((heads * tq, 1), F32)],
    )
    k4 = kbuf.reshape(kbuf.shape[0], bsz, tq, width)
    v4 = vbuf.reshape(vbuf.shape[0], bsz, tq, width)
    out = pl.pallas_call(
        functools.partial(_sb_sample_kernel, pages_per_step=pps, tq=tq, heads=heads, head_dim=head_dim),
        grid_spec=grid_spec,
        out_shape=jax.ShapeDtypeStruct((bsz, tq, width), F32),
        compiler_params=_params("parallel", "arbitrary"),
        name="sb_sample",
    )(page_table.reshape(-1), q.reshape(bsz, tq, width), bcol, k4, v4,
      *([cache_k] * pps), *([cache_v] * pps))
    return out.reshape(bsz * tq, width)


def _s5_discretize_kernel(lre_ref, lim_ref, ls_ref, bre_ref, bim_ref, are_ref, aim_ref, bbre_ref, bbim_ref):
    lam_re, lam_im = lre_ref[...], lim_ref[...]
    dt = jnp.exp(ls_ref[...])
    mag = jnp.exp(lam_re * dt)
    ab_re = mag * jnp.cos(lam_im * dt)
    ab_im = mag * jnp.sin(lam_im * dt)
    den = lam_re * lam_re + lam_im * lam_im
    xr = ab_re - 1.0
    f_re = (xr * lam_re + ab_im * lam_im) / den
    f_im = (ab_im * lam_re - xr * lam_im) / den
    are_ref[...] = ab_re
    aim_ref[...] = ab_im
    bbre_ref[...] = f_re * bre_ref[...] - f_im * bim_ref[...]
    bbim_ref[...] = f_re * bim_ref[...] + f_im * bre_ref[...]


def _s5_discretize(lam_re, lam_im, log_step, b_re, b_im):
    g, n = lam_re.shape
    ch = b_re.shape[-1]
    col = lambda a: a.reshape(g * n, 1)
    ls = jnp.broadcast_to(log_step[:, None], (g, n))
    return pl.pallas_call(
        _s5_discretize_kernel,
        out_shape=[jax.ShapeDtypeStruct((g * n, 1), F32)] * 2 + [jax.ShapeDtypeStruct((g * n, ch), F32)] * 2,
        name="s5_discretize",
    )(col(lam_re), col(lam_im), col(ls), b_re.reshape(g * n, ch), b_im.reshape(g * n, ch))


def _s5_kernel(u_ref, bmat_ref, arow_ref, h0_ref, cmat_ref, d_ref, wglu_ref, bglu_ref,
               y_ref, hfin_ref, hbuf_ref, hc_ref, *, tt, nb, ns, cb, nsplit):
    ti = pl.program_id(1)
    width = u_ref.shape[-1]

    @pl.when(ti == 0)
    def _():
        hc_ref[...] = h0_ref[...]

    u = u_ref[...].reshape(tt * nb, width)
    kw, nsh = width // nsplit, ns // nsplit
    u16 = u.astype(BF16)
    for hf in range(nsplit):
        chans = slice(hf * kw, (hf + 1) * kw)
        for part in (0, ns):
            states = slice(part + hf * nsh, part + (hf + 1) * nsh)
            hbuf_ref[:, states] = jnp.dot(u16[:, chans], bmat_ref[chans, states], preferred_element_type=F32)

    for c in range(ns // cb):
        re = slice(c * cb, (c + 1) * cb)
        im = slice(ns + c * cb, ns + (c + 1) * cb)
        a_re = jnp.broadcast_to(arow_ref[:, re], (SUBLANES, cb))
        a_im = jnp.broadcast_to(arow_ref[:, im], (SUBLANES, cb))

        def batch_rows(bb, _):
            r0 = pl.multiple_of(bb * SUBLANES, SUBLANES)

            def step(t, h):
                h_re, h_im = h
                r = pl.multiple_of(t * nb + r0, SUBLANES)
                n_re = a_re * h_re - a_im * h_im + hbuf_ref[pl.ds(r, SUBLANES), re]
                n_im = a_re * h_im + a_im * h_re + hbuf_ref[pl.ds(r, SUBLANES), im]
                hbuf_ref[pl.ds(r, SUBLANES), re] = n_re
                hbuf_ref[pl.ds(r, SUBLANES), im] = n_im
                return n_re, n_im

            h = (hc_ref[pl.ds(r0, SUBLANES), re], hc_ref[pl.ds(r0, SUBLANES), im])
            h = lax.fori_loop(0, tt, step, h, unroll=min(tt, 8))
            hc_ref[pl.ds(r0, SUBLANES), re] = h[0]
            hc_ref[pl.ds(r0, SUBLANES), im] = h[1]
            return 0

        lax.fori_loop(0, nb // SUBLANES, batch_rows, 0)

    halves = []
    for hf in range(nsplit):
        chans = slice(hf * kw, (hf + 1) * kw)
        acc = None
        for part in (0, ns):
            states = slice(part + hf * nsh, part + (hf + 1) * nsh)
            term = jnp.dot(hbuf_ref[:, states].astype(BF16), cmat_ref[states, chans], preferred_element_type=F32)
            acc = term if acc is None else acc + term
        halves.append(acc)
    y = (jnp.concatenate(halves, axis=1) if nsplit > 1 else halves[0]) + d_ref[...] * u
    y = 0.5 * y * (1.0 + jnp.tanh(math.sqrt(2.0 / math.pi) * (y + 0.044715 * (y * y * y))))
    z = jnp.dot(y.astype(BF16), wglu_ref[...], preferred_element_type=F32) + bglu_ref[...]
    y_ref[...] = (y * _sigmoid(z)).reshape(tt, nb, width)

    @pl.when(ti == pl.num_programs(1) - 1)
    def _():
        hfin_ref[...] = hc_ref[...]


S5_SLAB = 256


def _s5(u_tm, bmat, arow, h0, cmat, d_row, wglu, bglu, group_channels):
    t, b, width = u_tm.shape
    ns2 = bmat.shape[1]
    nsplit = width // S5_SLAB if width % S5_SLAB == 0 and S5_SLAB % group_channels == 0 else 1
    nb = _tile(b, 64, SUBLANES)
    tt = _tile(t, max(SUBLANES, 512 // nb), 1)
    const = lambda i, j: (0, 0)
    return pl.pallas_call(
        functools.partial(_s5_kernel, tt=tt, nb=nb, ns=ns2 // 2, cb=_tile(ns2 // 2, 512), nsplit=nsplit),
        grid=(b // nb, t // tt),
        in_specs=[
            pl.BlockSpec((tt, nb, width), lambda i, j: (j, i, 0)),
            pl.BlockSpec((width, ns2), const),
            pl.BlockSpec((1, ns2), const),
            pl.BlockSpec((nb, ns2), lambda i, j: (i, 0)),
            pl.BlockSpec((ns2, width), const),
            pl.BlockSpec((1, width), const),
            pl.BlockSpec((width, width), const),
            pl.BlockSpec((1, width), const),
        ],
        out_specs=[
            pl.BlockSpec((tt, nb, width), lambda i, j: (j, i, 0)),
            pl.BlockSpec((nb, ns2), lambda i, j: (i, 0)),
        ],
        out_shape=[jax.ShapeDtypeStruct((t, b, width), F32), jax.ShapeDtypeStruct((b, ns2), F32)],
        scratch_shapes=[pltpu.VMEM((tt * nb, ns2), F32), pltpu.VMEM((nb, ns2), F32)],
        compiler_params=_params("parallel", "arbitrary"),
        name="s5",
    )(u_tm, bmat, arow, h0, cmat, d_row, wglu, bglu)


def _split(a):
    hi = a.astype(BF16)
    return hi, (a - hi.astype(F32)).astype(BF16)


def _mm3(a, b):
    d = lambda x, y: jnp.dot(x, y, preferred_element_type=F32)
    return d(a[0], b[0]) + d(a[0], b[1]) + d(a[1], b[0])


def _block_tri_inverse(a, n, c):
    row = lax.broadcasted_iota(jnp.int32, (n, n), 0)
    col = lax.broadcasted_iota(jnp.int32, (n, n), 1)
    eye = (row == col).astype(F32)

    def same_block(b):
        s = b.bit_length() - 1
        return lax.shift_right_logical(row, s) == lax.shift_right_logical(col, s)

    b0 = min(c, TRI_BASE)
    neg = -jnp.where(same_block(b0), a, 0.0)
    t = eye + neg
    p = _split(neg)
    k = 1
    while 2 * k < b0:
        p = _split(_mm3(p, p))
        t = t + _mm3(_split(t), p)
        k *= 2
    b = b0
    while b < c:
        off = jnp.where(jnp.logical_and(same_block(2 * b), jnp.logical_not(same_block(b))), a, 0.0)
        ts = _split(t)
        t = t - _mm3(_split(_mm3(ts, _split(off))), ts)
        b *= 2
    return t


def _dn_kernel(q_ref, k_ref, v_ref, ba_ref, gate_ref, cw_ref, prm_ref, nrm_ref, cbuf_ref, s0_ref,
               o_ref, sout_ref, cout_ref,
               xpad_ref, qs_ref, ks_ref, vs_ref, gb_ref, s_ref, *, bs, tt, chunk, nh, hd, taps):
    ti = pl.program_id(1)
    width = nh * hd
    halo = SUBLANES
    first = halo - (taps - 1)

    @pl.when(ti == 0)
    def _():
        s_ref[...] = s0_ref[...]
        xpad_ref[:, first:halo, :] = cbuf_ref[...]

    xpad_ref[:, halo:halo + tt, 0:width] = q_ref[...]
    xpad_ref[:, halo:halo + tt, width:2 * width] = k_ref[...]
    xpad_ref[:, halo:halo + tt, 2 * width:3 * width] = v_ref[...]
    cw = cw_ref[...]
    y = cw[0:1, :] * xpad_ref[:, first:first + tt, :]
    for i in range(1, taps):
        y = y + cw[i:i + 1, :] * xpad_ref[:, first + i:first + i + tt, :]
    cout_ref[...] = xpad_ref[:, halo + tt - (taps - 1):halo + tt, :]
    xpad_ref[:, 0:halo, :] = xpad_ref[:, tt:tt + halo, :]
    act = y * _sigmoid(y)
    for h in range(nh):
        sl = slice(h * hd, (h + 1) * hd)
        qh = act[:, :, h * hd:(h + 1) * hd]
        kh = act[:, :, width + h * hd:width + (h + 1) * hd]
        qs_ref[:, :, sl] = qh * lax.rsqrt(jnp.sum(qh * qh, axis=-1, keepdims=True) + NORM_EPS) * (hd ** -0.5)
        ks_ref[:, :, sl] = kh * lax.rsqrt(jnp.sum(kh * kh, axis=-1, keepdims=True) + NORM_EPS)
    vs_ref[...] = act[:, :, 2 * width:3 * width]
    ba = ba_ref[...]
    lane = lax.broadcasted_iota(jnp.int32, ba.shape, 2)
    gval = -jnp.exp(prm_ref[0:1, :]) * _softplus(ba + prm_ref[1:2, :])
    gb_ref[...] = jnp.where(lane < nh, _sigmoid(ba), jnp.where(lane < 2 * nh, gval, 0.0))

    pairs = [(s, h) for s in range(bs) for h in range(nh)]
    npairs = len(pairs)
    nrow = npairs * chunk
    lc = chunk.bit_length() - 1
    row = lax.broadcasted_iota(jnp.int32, (nrow, nrow), 0)
    col = lax.broadcasted_iota(jnp.int32, (nrow, nrow), 1)
    same = lax.shift_right_logical(row, lc) == lax.shift_right_logical(col, lc)
    lower = jnp.logical_and(same, row >= col)
    strict = jnp.logical_and(same, row > col)
    tri = lower.astype(BF16)
    last_sel = col == (row | (chunk - 1))
    prow = lax.shift_right_logical(lax.broadcasted_iota(jnp.int32, (nrow, LANES), 0), lc)
    plane = lax.broadcasted_iota(jnp.int32, (nrow, LANES), 1)
    head_of_row = prow & (nh - 1)
    is_beta = plane == head_of_row
    is_g = plane == head_of_row + nh
    ones8 = jnp.ones((BF16_ROWS, LANES), BF16)
    pair_of_row = lax.shift_right_logical(lax.broadcasted_iota(jnp.int32, (nrow, hd), 0), lc)

    def expand(x):
        return jnp.concatenate([jnp.where(pair_of_row == p, x, 0.0) for p in range(npairs)], axis=1)

    def one_chunk(c):
        rows = pl.ds(c * chunk, chunk)
        stack = lambda ref: jnp.concatenate([ref[s, rows, h * hd:(h + 1) * hd] for s, h in pairs], axis=0)
        qst, kst, vst = stack(qs_ref), stack(ks_ref), stack(vs_ref)
        gbst = jnp.concatenate([gb_ref[s, rows, :] for s, _ in pairs], axis=0)
        g_hi, g_lo = _split(gbst)
        gcum_all = (jnp.dot(tri, g_hi, preferred_element_type=F32)
                    + jnp.dot(tri, g_lo, preferred_element_type=F32))
        gsel = jnp.where(is_g, gcum_all, 0.0)
        gc = jnp.sum(gsel, axis=-1, keepdims=True)
        beta = jnp.sum(jnp.where(is_beta, gbst, 0.0), axis=-1, keepdims=True)
        s_hi, s_lo = _split(gsel)
        grow = (lax.dot_general(ones8, s_hi, _NT, preferred_element_type=F32)
                + lax.dot_general(ones8, s_lo, _NT, preferred_element_type=F32))[0:1, :]
        decay = jnp.where(lower, jnp.exp(jnp.minimum(gc - grow, 0.0)), 0.0)
        kb = kst * beta
        a = _mm(kb, kst, _NT) * jnp.where(strict, decay, 0.0)
        tinv = _block_tri_inverse(a, nrow, chunk)
        u = _mm(tinv, vst * beta)
        w = _mm(tinv, kb * jnp.exp(gc))
        attn = _mm(qst, kst, _NT) * decay
        s_stack = jnp.concatenate([s_ref[s, h] for s, h in pairs], axis=0)
        v_new = u - _mm(expand(w), s_stack)
        o = _mm(expand(qst * jnp.exp(gc)), s_stack) + _mm(attn, v_new)
        g_last = jnp.sum(jnp.where(last_sel, grow, 0.0), axis=-1, keepdims=True)
        upd = _mm(expand(kst * jnp.exp(g_last - gc)), v_new, _TN)
        gt = stack(gate_ref)
        on = _rms(o, nrm_ref[...]) * (gt * _sigmoid(gt))
        for idx, (s, h) in enumerate(pairs):
            gl = g_last[idx * chunk:idx * chunk + 1, :]
            s_ref[s, h] = s_ref[s, h] * jnp.exp(gl) + upd[idx * hd:(idx + 1) * hd, :]
            o_ref[s, rows, h * hd:(h + 1) * hd] = on[idx * chunk:(idx + 1) * chunk, :]

    for c in range(tt // chunk):
        one_chunk(c)

    @pl.when(ti == pl.num_programs(1) - 1)
    def _():
        sout_ref[...] = s_ref[...]


def _deltanet(rest3, col0, conv_w, prm, nrm, cbuf, s0, nh, hd):
    nseq, t, _ = rest3.shape
    width = nh * hd
    taps = conv_w.shape[0]
    chunk = min(DN_CHUNK, t)
    assert t % chunk == 0 and chunk & (chunk - 1) == 0 and t >= taps - 1
    tt = _tile(t, 256, chunk)
    bs = _tile(nseq, 8, 1) if tt == t and t <= SUBLANES else 1
    ba_col = (col0 + 4) * (width // LANES)
    blk = lambda cidx: pl.BlockSpec((bs, tt, width), lambda i, j: (i, j, cidx))
    const = lambda i, j: (0, 0)
    return pl.pallas_call(
        functools.partial(_dn_kernel, bs=bs, tt=tt, chunk=chunk, nh=nh, hd=hd, taps=taps),
        grid=(nseq // bs, t // tt),
        in_specs=[
            blk(col0 + 1), blk(col0 + 2), blk(col0 + 3),
            pl.BlockSpec((bs, tt, LANES), lambda i, j: (i, j, ba_col)),
            blk(col0),
            pl.BlockSpec((taps, 3 * width), const),
            pl.BlockSpec((SUBLANES, LANES), const),
            pl.BlockSpec((1, hd), const),
            pl.BlockSpec((bs, taps - 1, 3 * width), lambda i, j: (i, 0, 0)),
            pl.BlockSpec((bs, nh, hd, hd), lambda i, j: (i, 0, 0, 0)),
        ],
        out_specs=[
            pl.BlockSpec((bs, tt, width), lambda i, j: (i, j, 0)),
            pl.BlockSpec((bs, nh, hd, hd), lambda i, j: (i, 0, 0, 0)),
            pl.BlockSpec((bs, taps - 1, 3 * width), lambda i, j: (i, 0, 0)),
        ],
        out_shape=[
            jax.ShapeDtypeStruct((nseq, t, width), F32),
            jax.ShapeDtypeStruct((nseq, nh, hd, hd), F32),
            jax.ShapeDtypeStruct((nseq, taps - 1, 3 * width), F32),
        ],
        scratch_shapes=[
            pltpu.VMEM((bs, tt + 2 * SUBLANES, 3 * width), F32),
            pltpu.VMEM((bs, tt, width), F32),
            pltpu.VMEM((bs, tt, width), F32),
            pltpu.VMEM((bs, tt, width), F32),
            pltpu.VMEM((bs, tt, LANES), F32),
            pltpu.VMEM((bs, nh, hd, hd), F32),
        ],
        compiler_params=_params("parallel", "arbitrary"),
        name="deltanet",
    )(rest3, rest3, rest3, rest3, rest3, conv_w, prm, nrm, cbuf, s0)


def _outproj_kernel(oa_ref, ob_ref, oc_ref, ga_ref, gb_ref, w_ref, x_ref, o_ref, mix_ref):
    wa, wb = oa_ref.shape[-1], ob_ref.shape[-1]

    @pl.when(pl.program_id(1) == 0)
    def _():
        mix_ref[:, 0:wa] = _rms(oa_ref[...], ga_ref[...]).astype(BF16)
        mix_ref[:, wa:wa + wb] = _rms(ob_ref[...], gb_ref[...]).astype(BF16)
        mix_ref[:, wa + wb:] = oc_ref[...].astype(BF16)

    o_ref[...] = x_ref[...] + jnp.dot(mix_ref[...], w_ref[...], preferred_element_type=F32)


def _outproj(oa, ob, oc, ga, gb, w, x2d):
    m, d = x2d.shape
    tm = _tile(m, 512, SUBLANES)
    _, d_in, tn = w.shape
    row = lambda a: pl.BlockSpec((tm, a.shape[-1]), lambda i, j: (i, 0))
    const = lambda a: pl.BlockSpec((1, a.shape[-1]), lambda i, j: (0, 0))
    return pl.pallas_call(
        _outproj_kernel,
        grid=(m // tm, d // tn),
        in_specs=[row(oa), row(ob), row(oc), const(ga), const(gb),
                  pl.BlockSpec((None, d_in, tn), lambda i, j: (j, 0, 0)),
                  pl.BlockSpec((tm, tn), lambda i, j: (i, j))],
        out_specs=pl.BlockSpec((tm, tn), lambda i, j: (i, j)),
        out_shape=jax.ShapeDtypeStruct((m, d), F32),
        scratch_shapes=[pltpu.VMEM((tm, d_in), BF16)],
        compiler_params=_params("parallel", "arbitrary"),
        name="outproj",
    )(oa, ob, oc, ga, gb, w, x2d)


FFN_SUBTILE = 256


def _ffn_kernel(x_ref, g_ref, wg_ref, wv_ref, wd_ref, cw_ref, st_ref, o_ref, buf_ref, h_ref, act_ref, carry_ref,
                *, nf, tiles_per_seq, per_seq_rows):
    i = pl.program_id(0)
    f = pl.program_id(1)
    tm, tf = act_ref.shape

    def up():
        h = h_ref[...]
        if not per_seq_rows:
            @pl.when((i % tiles_per_seq) == 0)
            def _():
                carry_ref[f] = st_ref[...]

        tc = FFN_SUBTILE if tf % FFN_SUBTILE == 0 else tf
        row = lax.broadcasted_iota(jnp.int32, (tm, tc), 0)
        for s in range(tf // tc):
            cols = slice(s * tc, (s + 1) * tc)
            gate = jnp.dot(h, wg_ref[:, cols], preferred_element_type=F32)
            val = jnp.dot(h, wv_ref[:, cols], preferred_element_type=F32)
            prev1 = pltpu.roll(gate, 1, axis=0)
            prev2 = pltpu.roll(gate, 2, axis=0)
            if per_seq_rows:
                nseq = tm // per_seq_rows
                expand = lambda a: jnp.broadcast_to(a, (nseq, per_seq_rows, tc)).reshape(tm, tc)
                s0, s1 = expand(st_ref[:, 0:1, cols]), expand(st_ref[:, 1:2, cols])
                t = row & (per_seq_rows - 1)
                buf_ref[:, :, cols] = gate.reshape(nseq, per_seq_rows, tc)[:, per_seq_rows - 2:, :]
            else:
                s0, s1 = carry_ref[f, 0:1, cols], carry_ref[f, 1:2, cols]
                t = row
                tail = gate[tm - 2:tm, :]
                carry_ref[f, :, cols] = tail
                buf_ref[:, cols] = tail
            x1 = jnp.where(t >= 1, prev1, s1)
            x2 = jnp.where(t >= 2, prev2, jnp.where(t == 1, s1, s0))
            conv = cw_ref[2:3, cols] * gate + cw_ref[1:2, cols] * x1 + cw_ref[0:1, cols] * x2
            act_ref[:, cols] = (conv * _sigmoid(conv) * val).astype(BF16)

    def down():
        o_ref[...] += jnp.dot(act_ref[...], wd_ref[...], preferred_element_type=F32)

    @pl.when(f == 0)
    def _():
        h_ref[...] = _rms(x_ref[...], g_ref[...]).astype(BF16)
        o_ref[...] = x_ref[...]
        up()

    @pl.when(jnp.logical_and(f > 0, f < nf))
    def _():
        down()
        up()

    @pl.when(f == nf)
    def _():
        down()


def _ffn(x2d, gain, wg, wv, wd, conv_w, state, seq):
    m, d = x2d.shape
    d_ff = wg.shape[1]
    tf = _tile(d_ff, FFN_TILE)
    nf = d_ff // tf
    nseq = m // seq
    assert conv_w.shape[0] == 3
    up_tile = lambda f: jnp.minimum(f, nf - 1)
    down_tile = lambda f: jnp.maximum(f - 1, 0)
    if seq <= SUBLANES:
        assert seq == SUBLANES
        tm = _tile(m, FFN_ROWS, SUBLANES)
        per_seq_rows, tiles_per_seq = seq, 1
        st_spec = buf_spec = pl.BlockSpec((tm // seq, 2, tf), lambda i, f: (i, 0, up_tile(f)))
        n_buf = nseq
    else:
        tm = _tile(seq, FFN_ROWS, SUBLANES)
        per_seq_rows, tiles_per_seq = 0, seq // tm
        st_spec = pl.BlockSpec((None, 2, tf), lambda i, f: (i // tiles_per_seq, 0, up_tile(f)))
        buf_spec = pl.BlockSpec((None, 2, tf), lambda i, f: (i, 0, up_tile(f)))
        n_buf = m // tm
    y, buf = pl.pallas_call(
        functools.partial(_ffn_kernel, nf=nf, tiles_per_seq=tiles_per_seq, per_seq_rows=per_seq_rows),
        grid=(m // tm, nf + 1),
        in_specs=[
            pl.BlockSpec((tm, d), lambda i, f: (i, 0), pipeline_mode=pl.Buffered(1)),
            pl.BlockSpec((1, d), lambda i, f: (0, 0)),
            pl.BlockSpec((d, tf), lambda i, f: (0, up_tile(f))),
            pl.BlockSpec((d, tf), lambda i, f: (0, up_tile(f))),
            pl.BlockSpec((tf, d), lambda i, f: (down_tile(f), 0)),
            pl.BlockSpec((3, tf), lambda i, f: (0, up_tile(f))),
            st_spec,
        ],
        out_specs=[pl.BlockSpec((tm, d), lambda i, f: (i, 0)), buf_spec],
        out_shape=[jax.ShapeDtypeStruct((m, d), F32), jax.ShapeDtypeStruct((n_buf, 2, d_ff), F32)],
        scratch_shapes=[pltpu.VMEM((tm, d), BF16), pltpu.VMEM((tm, tf), BF16), pltpu.VMEM((nf, 2, tf), F32)],
        compiler_params=_params("arbitrary", "arbitrary"),
        name="convffn",
    )(x2d, gain, wg, wv, wd, conv_w, state)
    if not per_seq_rows:
        buf = buf.reshape(nseq, tiles_per_seq, 2, d_ff)[:, tiles_per_seq - 1]
    return y, buf


FFN_TILE = 512
FFN_ROWS = 1024
OUT_TILE = 1024


def _col_tiles(w, tn):
    d, n = w.shape
    return w.reshape(d, n // tn, tn).transpose(1, 0, 2)


def _layer_weights(l, w_in, sb_q_norm, sb_k_norm, ssm, dn_a_log, dn_dt_bias, ssm_w_glu, w_out, ffn_w_in, ffn_w_out,
                   dims):
    sbw, ssmw, dnw, nh = dims["sbw"], dims["ssmw"], dims["dnw"], dims["dn_heads"]
    heads, hd = dims["sb_heads"], dims["sb_hd"]
    d = w_in.shape[1]
    o_u = 3 * sbw
    o_qkv = o_u + ssmw
    o_ba = o_qkv + 3 * dnw
    o_gate = o_ba + 2 * nh
    n_rest = -(-(ssmw + 4 * dnw + LANES) // sbw) * sbw
    pad = n_rest - (ssmw + 4 * dnw + 2 * nh)
    w = w_in[l]
    w_perm = jnp.concatenate([
        w[:, :o_u], w[:, o_u:o_qkv], w[:, o_gate:o_gate + dnw], w[:, o_qkv:o_ba], w[:, o_ba:o_gate],
        jnp.zeros((d, pad), w.dtype)], axis=1).astype(BF16)
    head_of = jnp.arange(sbw) // hd
    gs = (head_of[:, None] == jnp.arange(LANES)[None, :]).astype(BF16)
    qg = jnp.tile(sb_q_norm[l], heads)[None, :]
    kg = jnp.tile(sb_k_norm[l], heads)[None, :]
    a_re, a_im, bb_re, bb_im = ssm["disc"]
    g, n, ch = dims["ssm_groups"], dims["ssm_state"], dims["ssm_ch"]
    ns = g * n
    grp_of_state = jnp.arange(ns) // n
    grp_of_chan = jnp.arange(ssmw) // ch
    blockmask = grp_of_chan[:, None] == grp_of_state[None, :]
    expand_b = lambda bb: jnp.where(blockmask, jnp.tile(bb.T, (g, 1)), 0.0)
    bmat = jnp.concatenate([expand_b(bb_re), expand_b(bb_im)], axis=1).astype(BF16)
    arow = jnp.concatenate([a_re, a_im], axis=0).reshape(1, 2 * ns)
    expand_c = lambda c: jnp.where(blockmask.T, jnp.tile(c.transpose(0, 2, 1).reshape(ns, ch), (1, g)), 0.0)
    cmat = jnp.concatenate([expand_c(ssm["c_re"][l]), -expand_c(ssm["c_im"][l])], axis=0).astype(BF16)
    prm = jnp.zeros((SUBLANES, LANES), F32)
    prm = prm.at[0, nh:2 * nh].set(dn_a_log[l]).at[1, nh:2 * nh].set(dn_dt_bias[l])
    d_ff = ffn_w_in.shape[2] // 2
    return dict(
        w_perm=_col_tiles(w_perm, sbw), gs=gs, e=gs.T, qg=qg, kg=kg, bmat=bmat, arow=arow, cmat=cmat, prm=prm,
        wglu=ssm_w_glu[l].astype(BF16), w_out=_col_tiles(w_out[l].astype(BF16), _tile(w_out.shape[2], OUT_TILE)),
        wg=ffn_w_in[l][:, :d_ff].astype(BF16), wv=ffn_w_in[l][:, d_ff:].astype(BF16),
        wd=ffn_w_out[l].astype(BF16), n_rest=n_rest)


def kernel(x_prompt, x_sample, cache_k, cache_v, page_table, state_ssm_re, state_ssm_im, state_delta, state_delta_conv, state_ffn_conv, norm_mix, w_in, sb_q_norm, sb_k_norm, sb_logit_bias, ssm_lambda_re, ssm_lambda_im, ssm_log_step, ssm_b_re, ssm_b_im, ssm_c_re, ssm_c_im, ssm_d, ssm_w_glu, ssm_b_glu, dn_conv_w, dn_a_log, dn_dt_bias, dn_norm, norm_out_sb, norm_out_ssm, w_out, norm_ffn, ffn_w_in, ffn_conv_w, ffn_w_out):
    depth = w_in.shape[0]
    bp, tp, d = x_prompt.shape
    bd, td, _ = x_sample.shape
    sb_heads, sb_hd = cache_k.shape[-2], cache_k.shape[-1]
    g, n = ssm_lambda_re.shape[1:]
    ch = ssm_b_re.shape[-1]
    dn_heads, dn_hd = dn_a_log.shape[1], dn_norm.shape[1]
    dims = dict(sbw=sb_heads * sb_hd, sb_heads=sb_heads, sb_hd=sb_hd, ssmw=g * ch, ssm_groups=g, ssm_state=n,
                ssm_ch=ch, dnw=dn_heads * dn_hd, dn_heads=dn_heads)
    sbw, ssmw, dnw, ns = dims["sbw"], dims["ssmw"], dims["dnw"], g * n
    assert ssmw % dnw == 0 and dnw % LANES == 0 and sbw % LANES == 0 and LANES % sb_hd == 0
    d_ff = ffn_conv_w.shape[2]
    col0 = ssmw // dnw
    q_scale = sb_hd ** -0.5 * LOG2E

    groups = {
        "p": dict(x=x_prompt.reshape(bp * tp, d), b=bp, t=tp, kbuf=None, vbuf=None, outs=[]),
        "s": dict(x=x_sample.reshape(bd * td, d), b=bd, t=td, kbuf=None, vbuf=None, outs=[]),
    }
    for l in range(depth):
        disc = _s5_discretize(ssm_lambda_re[l], ssm_lambda_im[l], ssm_log_step[l], ssm_b_re[l], ssm_b_im[l])
        lw = _layer_weights(l, w_in, sb_q_norm, sb_k_norm, dict(disc=disc, c_re=ssm_c_re, c_im=ssm_c_im),
                            dn_a_log, dn_dt_bias, ssm_w_glu, w_out, ffn_w_in, ffn_w_out, dims)
        for name, grp in groups.items():
            b, t, x = grp["b"], grp["t"], grp["x"]
            q, kbuf, vbuf, rest = _inproj(x, norm_mix[l][None, :], lw["w_perm"], lw["gs"], lw["e"], lw["qg"],
                                          lw["kg"], grp["kbuf"], grp["vbuf"], l, depth, sb_hd, q_scale, t,
                                          time_minor=(name == "p"))
            grp["kbuf"], grp["vbuf"] = kbuf, vbuf
            rest3 = rest.reshape(b, t, lw["n_rest"])
            if name == "p":
                o_a = _sb_prompt(q, kbuf, vbuf, sb_logit_bias[l], l, b, t, sb_hd)
                h0 = jnp.zeros((b, 2 * ns), F32)
                dn_s0 = jnp.zeros((b, dn_heads, dn_hd, dn_hd), F32)
                dn_c0 = jnp.zeros((b, dn_conv_w.shape[1] - 1, 3 * dnw), F32)
                ffn_c0 = jnp.zeros((b, ffn_conv_w.shape[1] - 1, d_ff), F32)
            else:
                o_a = _sb_sample(q, kbuf, vbuf, cache_k, cache_v, page_table, sb_logit_bias[l], l, b, t, sb_hd)
                h0 = jnp.concatenate([state_ssm_re[l].reshape(b, ns), state_ssm_im[l].reshape(b, ns)], axis=1)
                dn_s0, dn_c0, ffn_c0 = state_delta[l], state_delta_conv[l], state_ffn_conv[l]
            u_tm = rest3[:, :, :ssmw].transpose(1, 0, 2)
            y_tm, h_fin = _s5(u_tm, lw["bmat"], lw["arow"], h0, lw["cmat"], ssm_d[l][None, :], lw["wglu"],
                              ssm_b_glu[l][None, :], ch)
            o_b = y_tm.transpose(1, 0, 2).reshape(b * t, ssmw)
            o_c, dn_s, dn_c = _deltanet(rest3, col0, dn_conv_w[l], lw["prm"], dn_norm[l][None, :], dn_c0, dn_s0,
                                        dn_heads, dn_hd)
            x = _outproj(o_a, o_b, o_c.reshape(b * t, dnw), norm_out_sb[l][None, :], norm_out_ssm[l][None, :],
                         lw["w_out"], x)
            x, ffn_c = _ffn(x, norm_ffn[l][None, :], lw["wg"], lw["wv"], lw["wd"], ffn_conv_w[l], ffn_c0, t)
            grp["x"] = x
            grp["outs"].append((h_fin[:, :ns].reshape(b, g, n), h_fin[:, ns:].reshape(b, g, n), dn_s, dn_c, ffn_c))

    res = {}
    for name, grp in groups.items():
        b, t = grp["b"], grp["t"]
        sre, sim, dn_s, dn_c, ffn_c = [jnp.stack(z) for z in zip(*grp["outs"])]
        if name == "p":
            heads_last = lambda a: a.reshape(depth, b, sb_heads, sb_hd, t).transpose(0, 1, 4, 2, 3)
        else:
            heads_last = lambda a: a.reshape(depth, b, t, sb_heads, sb_hd)
        res[name] = dict(
            y=grp["x"].reshape(b, t, d), k=heads_last(grp["kbuf"]), v=heads_last(grp["vbuf"]),
            sre=sre, sim=sim, dn_s=dn_s, dn_c=dn_c, ffn_c=ffn_c)
    p, s = res["p"], res["s"]
    return (p["y"], s["y"], p["k"], p["v"], s["k"], s["v"], p["sre"], p["sim"], s["sre"], s["sim"],
            p["dn_s"], s["dn_s"], p["dn_c"], s["dn_c"], p["ffn_c"], s["ffn_c"])
```

```python
import functools
import math

import jax
import jax.numpy as jnp
from jax import lax
from jax.experimental import pallas as pl
from jax.experimental.pallas import tpu as pltpu

F32 = jnp.float32
BF16 = jnp.bfloat16
HIGHEST = lax.Precision.HIGHEST
NORM_EPS = 1e-6
LANES = 128
SUBLANES = 8
BF16_ROWS = 16
VMEM_LIMIT_BYTES = 56 * 1024 * 1024
DN_CHUNK = 64
TRI_BASE = 16

_NT = (((1,), (1,)), ((), ()))
_TN = (((0,), (0,)), ((), ()))


def _tile(dim, pref, mult=LANES):
    if dim <= pref:
        return dim
    t = (pref // mult) * mult
    while t >= mult:
        if dim % t == 0:
            return t
        t -= mult
    raise ValueError(f"no tile for {dim} (pref {pref}, mult {mult})")


def _params(*sem):
    return pltpu.CompilerParams(dimension_semantics=sem, vmem_limit_bytes=VMEM_LIMIT_BYTES)


def _mm(a, b, dims=None):
    ok = all(d % BF16_ROWS == 0 for d in a.shape + b.shape)
    if ok:
        a, b = a.astype(BF16), b.astype(BF16)
    else:
        a, b = a.astype(F32), b.astype(F32)
    if dims is None:
        return jnp.dot(a, b, preferred_element_type=F32)
    return lax.dot_general(a, b, dims, preferred_element_type=F32)


def _mm_f32(a, b, dims=None):
    if dims is None:
        return jnp.dot(a, b, preferred_element_type=F32, precision=HIGHEST)
    return lax.dot_general(a, b, dims, preferred_element_type=F32, precision=HIGHEST)


def _softplus(z):
    return jnp.maximum(z, 0.0) + jnp.log1p(jnp.exp(-jnp.abs(z)))


def _sigmoid(z):
    return 1.0 / (1.0 + jnp.exp(-z))


def _rms(x, gain, eps=NORM_EPS):
    return x * lax.rsqrt(jnp.mean(x * x, axis=-1, keepdims=True) + eps) * gain


INPROJ_ROWS = 512
SB_PAGES_PER_STEP = 16


def _inproj_kernel(x_ref, g_ref, w_ref, gs_ref, e_ref, qg_ref, kg_ref, *refs, head_dim, q_scale, aliased,
                   time_minor):
    q_ref, k_ref, v_ref, r_ref, xn_ref = refs[2:] if aliased else refs
    j = pl.program_id(1)
    kv_layout = (lambda a: a.T) if time_minor else (lambda a: a)

    @pl.when(j == 0)
    def _():
        xn_ref[...] = _rms(x_ref[...], g_ref[...]).astype(BF16)

    y = jnp.dot(xn_ref[...], w_ref[...], preferred_element_type=F32)

    def head_norm(gain):
        ms = jnp.dot((y * y).astype(BF16), gs_ref[...], preferred_element_type=F32) * (1.0 / head_dim)
        inv = lax.rsqrt(ms + NORM_EPS)
        hi = inv.astype(BF16)
        lo = (inv - hi.astype(F32)).astype(BF16)
        sc = (jnp.dot(hi, e_ref[...], preferred_element_type=F32)
              + jnp.dot(lo, e_ref[...], preferred_element_type=F32))
        return y * sc * gain

    @pl.when(j == 0)
    def _():
        q_ref[...] = (head_norm(qg_ref[...]) * q_scale).astype(BF16)

    @pl.when(j == 1)
    def _():
        k_ref[...] = kv_layout(head_norm(kg_ref[...]))

    @pl.when(j == 2)
    def _():
        v_ref[...] = kv_layout(y)

    @pl.when(j >= 3)
    def _():
        r_ref[...] = y


def _inproj(x2d, gain, w_perm, gs, e, qg, kg, kbuf, vbuf, layer, depth, head_dim, q_scale, seq, time_minor):
    m, d = x2d.shape
    tn = gs.shape[0]
    assert w_perm.shape[1:] == (d, tn)
    n_rest = (w_perm.shape[0] - 3) * tn
    tm = _tile(seq, INPROJ_ROWS, LANES) if time_minor else _tile(m, INPROJ_ROWS, SUBLANES)
    tiles_per_seq = seq // tm
    aliased = kbuf is not None
    const = lambda i, j: (0, 0)
    in_specs = [
        pl.BlockSpec((tm, d), lambda i, j: (i, 0)),
        pl.BlockSpec((1, d), const),
        pl.BlockSpec((None, d, tn), lambda i, j: (j, 0, 0)),
        pl.BlockSpec((tn, LANES), const),
        pl.BlockSpec((LANES, tn), const),
        pl.BlockSpec((1, tn), const),
        pl.BlockSpec((1, tn), const),
    ]
    args = [x2d, gain, w_perm, gs, e, qg, kg]
    aliases = {}
    if aliased:
        in_specs += [pl.BlockSpec(memory_space=pl.ANY), pl.BlockSpec(memory_space=pl.ANY)]
        args += [kbuf, vbuf]
        aliases = {7: 1, 8: 2}
    if time_minor:
        kv_spec = pl.BlockSpec((None, None, tn, tm), lambda i, j: (layer, i // tiles_per_seq, 0, i % tiles_per_seq))
        kv_shape = jax.ShapeDtypeStruct((depth, m // seq, tn, seq), F32)
    else:
        kv_spec = pl.BlockSpec((None, tm, tn), lambda i, j: (layer, i, 0))
        kv_shape = jax.ShapeDtypeStruct((depth, m, tn), F32)
    out_specs = [
        pl.BlockSpec((tm, tn), lambda i, j: (i, 0)),
        kv_spec,
        kv_spec,
        pl.BlockSpec((tm, tn), lambda i, j: (i, jnp.maximum(j - 3, 0))),
    ]
    out_shape = [jax.ShapeDtypeStruct((m, tn), BF16), kv_shape, kv_shape, jax.ShapeDtypeStruct((m, n_rest), F32)]
    return pl.pallas_call(
        functools.partial(_inproj_kernel, head_dim=head_dim, q_scale=q_scale, aliased=aliased,
                          time_minor=time_minor),
        grid=(m // tm, 3 + n_rest // tn),
        in_specs=in_specs, out_specs=out_specs, out_shape=out_shape,
        scratch_shapes=[pltpu.VMEM((tm, d), BF16)],
        input_output_aliases=aliases,
        compiler_params=_params("parallel", "arbitrary"),
        name="inproj",
    )(*args)


LOG2E = math.log2(math.e)


def _softplus2(zs):
    neg_abs = lax.bitcast_convert_type(lax.bitcast_convert_type(zs, jnp.uint32) | jnp.uint32(0x80000000), F32)
    return jnp.maximum(zs, 0.0) + jnp.log2(1.0 + jnp.exp2(neg_abs))


SB_BLOCK = 256
SB_ROW_SUBBLOCKS = 4


def _sb_prompt_kernel(bias_ref, q_ref, k_ref, v_ref, o_ref, *, blk, nsub, head_dim, heads_per_step):
    hp = pl.program_id(1)
    qi = pl.program_id(2)
    row = lax.broadcasted_iota(jnp.int32, (blk, blk), 0)
    col = lax.broadcasted_iota(jnp.int32, (blk, blk), 1)
    later = (row > col).astype(BF16)
    visible = col < row
    q = q_ref[...]
    heads = range(heads_per_step)
    lanes = [slice(hh * head_dim, (hh + 1) * head_dim) for hh in heads]
    qs = [[q[j * blk:(j + 1) * blk, lanes[hh]] for j in range(nsub)] for hh in heads]
    biases = [bias_ref[hp * heads_per_step + hh] * LOG2E for hh in heads]

    def load(ki):
        r0 = pl.multiple_of(ki * blk, blk)
        kb = k_ref[:, pl.ds(r0, blk)].astype(BF16)
        vb = v_ref[:, pl.ds(r0, blk)].astype(BF16)
        return [(kb[lanes[hh], :], vb[lanes[hh], :]) for hh in heads]

    def block(qh, kv, bias, oc, masked):
        o_acc, carry = oc
        z = jnp.dot(qh, kv[0], preferred_element_type=F32) + bias
        sp = _softplus2(z)
        if masked:
            sp = jnp.where(visible, sp, 0.0)
        after = jnp.dot(sp.astype(BF16), later, preferred_element_type=F32)
        w = jnp.exp2(z - sp - after - carry)
        if masked:
            w = jnp.where(visible, w, 0.0)
        o_acc = o_acc + lax.dot_general(w.astype(BF16), kv[1], _NT, preferred_element_type=F32)
        return o_acc, carry + after[:, 0:1] + sp[:, 0:1]

    zero = (jnp.zeros((blk, head_dim), F32), jnp.zeros((blk, 1), F32))
    state = [[zero for _ in range(nsub)] for _ in heads]
    for d in reversed(range(nsub)):
        kv = load(qi * nsub + d)
        for hh in heads:
            for j in range(d, nsub):
                state[hh][j] = block(qs[hh][j], kv[hh], biases[hh], state[hh][j], j == d)

    def older(it, flat):
        kv = load(qi * nsub - 1 - it)
        return tuple(block(qs[hh][j], kv[hh], biases[hh], flat[hh * nsub + j], False)
                     for hh in heads for j in range(nsub))

    flat = lax.fori_loop(0, qi * nsub, older, tuple(state[hh][j] for hh in heads for j in range(nsub)))
    o_ref[...] = jnp.concatenate(
        [jnp.concatenate([flat[hh * nsub + j][0] for hh in heads], axis=-1) for j in range(nsub)], axis=0)


def _sb_prompt(q, kbuf, vbuf, bias, layer, bsz, seq, head_dim):
    width = q.shape[-1]
    heads_per_step = LANES // head_dim
    blk = _tile(seq, SB_BLOCK)
    nsub = max(n for n in (1, 2, 4, 8) if n <= SB_ROW_SUBBLOCKS and seq % (n * blk) == 0)
    rows = nsub * blk
    q3 = q.reshape(bsz, seq, width)
    return pl.pallas_call(
        functools.partial(_sb_prompt_kernel, blk=blk, nsub=nsub, head_dim=head_dim, heads_per_step=heads_per_step),
        grid=(bsz, width // LANES, seq // rows),
        in_specs=[
            pl.BlockSpec(memory_space=pltpu.SMEM),
            pl.BlockSpec((None, rows, LANES), lambda b, h, i: (b, i, h)),
            pl.BlockSpec((None, None, LANES, seq), lambda b, h, i: (layer, b, h, 0)),
            pl.BlockSpec((None, None, LANES, seq), lambda b, h, i: (layer, b, h, 0)),
        ],
        out_specs=pl.BlockSpec((None, rows, LANES), lambda b, h, i: (b, i, h)),
        out_shape=jax.ShapeDtypeStruct((bsz, seq, width), F32),
        compiler_params=_params("parallel", "parallel", "arbitrary"),
        name="sb_prompt",
    )(bias, q3, kbuf, vbuf).reshape(bsz * seq, width)


SB_PAGE_GROUP = 4


def _sb_sample_kernel(pt_ref, q_ref, bcol_ref, kn_ref, vn_ref, *refs, pages_per_step, tq, heads, head_dim):
    del pt_ref
    kp = refs[:pages_per_step]
    vp = refs[pages_per_step:2 * pages_per_step]
    o_ref, acc_ref, carry_ref = refs[2 * pages_per_step:]
    c = pl.program_id(1)
    nr = heads * tq
    ltq = tq.bit_length() - 1
    page = kp[0].shape[-1]
    qf = q_ref[...].astype(F32)
    width = qf.shape[-1]
    bcol = bcol_ref[...] * LOG2E
    r = lax.broadcasted_iota(jnp.int32, (nr, width), 0)
    col = lax.broadcasted_iota(jnp.int32, (nr, width), 1)
    own_head = lax.shift_right_logical(r, ltq) == lax.shift_right_logical(col, head_dim.bit_length() - 1)
    qbd = jnp.where(own_head, jnp.concatenate([qf] * heads, axis=0), 0.0)

    @pl.when(c == 0)
    def _():
        z = lax.dot_general(qbd, kn_ref[...], _NT, preferred_element_type=F32) + bcol
        key = lax.broadcasted_iota(jnp.int32, (nr, tq), 1)
        qry = lax.broadcasted_iota(jnp.int32, (nr, tq), 0) & (tq - 1)
        vis = key < qry
        sp = jnp.where(vis, _softplus2(z), 0.0)
        later = (lax.broadcasted_iota(jnp.int32, (tq, tq), 0)
                 > lax.broadcasted_iota(jnp.int32, (tq, tq), 1)).astype(F32)
        w = jnp.where(vis, jnp.exp2(z - sp - _mm_f32(sp, later)), 0.0)
        acc_ref[...] = jnp.dot(w, vn_ref[...], preferred_element_type=F32)
        carry_ref[...] = jnp.sum(sp, axis=-1, keepdims=True)

    qbd16 = qbd.astype(BF16)
    gsz = SB_PAGE_GROUP if pages_per_step % SB_PAGE_GROUP == 0 else 1
    nk = gsz * page
    later = (lax.broadcasted_iota(jnp.int32, (nk, nk), 0)
             > lax.broadcasted_iota(jnp.int32, (nk, nk), 1)).astype(BF16)
    side_by_side = lambda refs: jnp.concatenate(
        [r[...].reshape(width, page).astype(BF16) for r in reversed(refs)], axis=1)
    for g in range(pages_per_step // gsz):
        kt = side_by_side(kp[g * gsz:(g + 1) * gsz])
        z = jnp.dot(qbd16, kt, preferred_element_type=F32) + bcol
        sp = _softplus2(z)
        after = jnp.dot(sp.astype(BF16), later, preferred_element_type=F32)
        carry = carry_ref[...]
        w = jnp.exp2(z - sp - after - carry).astype(BF16)
        vt = side_by_side(vp[g * gsz:(g + 1) * gsz])
        acc_ref[...] += lax.dot_general(w, vt, _NT, preferred_element_type=F32)
        carry_ref[...] = carry + after[:, 0:1] + sp[:, 0:1]

    @pl.when(c == pl.num_programs(1) - 1)
    def _():
        acc = jnp.where(own_head, acc_ref[...], 0.0)
        o_ref[...] = acc.reshape(heads, tq, width).sum(axis=0)


def _sb_sample(q, kbuf, vbuf, cache_k, cache_v, page_table, bias, layer, bsz, tq, head_dim):
    width = q.shape[-1]
    heads = width // head_dim
    assert tq & (tq - 1) == 0 and head_dim & (head_dim - 1) == 0
    page = cache_k.shape[2]
    npages = page_table.shape[1]
    pps = _tile(npages, SB_PAGES_PER_STEP, 1)
    bcol =jnp.repeat(bias.astype(F32), tq)[:, None]
    cache_k = cache_k.transpose(0, 1, 3, 4, 2)
    cache_v = cache_v.transpose(0, 1, 3, 4, 2)

    def page_map(p):
        return lambda b, c, pt: (layer, pt[b * npages + npages - 1 - (c * pps + p)], 0, 0, 0)

    page_specs = [pl.BlockSpec((None, None, heads, head_dim, page), page_map(p)) for p in range(pps)]
    grid_spec = pltpu.PrefetchScalarGridSpec(
        num_scalar_prefetch=1,
        grid=(bsz, npages // pps),
        in_specs=[
            pl.BlockSpec((None, tq, width), lambda b, c, pt: (b, 0, 0)),
            pl.BlockSpec((heads * tq, 1), lambda b, c, pt: (0, 0)),
            pl.BlockSpec((None, None, tq, width), lambda b, c, pt: (layer, b, 0, 0)),
            pl.BlockSpec((None, None, tq, width), lambda b, c, pt: (layer, b, 0, 0)),
        ] + page_specs + page_specs,
        out_specs=pl.BlockSpec((None, tq, width), lambda b, c, pt: (b, 0, 0)),
        scratch_shapes=[pltpu.VMEM((heads * tq, width), F32), pltpu.VMEM((heads * tq, 1), F32)],
    )
    k4 = kbuf.reshape(kbuf.shape[0], bsz, tq, width)
    v4 = vbuf.reshape(vbuf.shape[0], bsz, tq, width)
    out = pl.pallas_call(
        functools.partial(_sb_sample_kernel, pages_per_step=pps, tq=tq, heads=heads, head_dim=head_dim),
        grid_spec=grid_spec,
        out_shape=jax.ShapeDtypeStruct((bsz, tq, width), F32),
        compiler_params=_params("parallel", "arbitrary"),
        name="sb_sample",
    )(page_table.reshape(-1), q.reshape(bsz, tq, width), bcol, k4, v4,
      *([cache_k] * pps), *([cache_v] * pps))
    return out.reshape(bsz * tq, width)


def _s5_discretize_kernel(lre_ref, lim_ref, ls_ref, bre_ref, bim_ref, are_ref, aim_ref, bbre_ref, bbim_ref):
    lam_re, lam_im = lre_ref[...], lim_ref[...]
    dt = jnp.exp(ls_ref[...])
    mag = jnp.exp(lam_re * dt)
    ab_re = mag * jnp.cos(lam_im * dt)
    ab_im = mag * jnp.sin(lam_im * dt)
    den = lam_re * lam_re + lam_im * lam_im
    xr = ab_re - 1.0
    f_re = (xr * lam_re + ab_im * lam_im) / den
    f_im = (ab_im * lam_re - xr * lam_im) / den
    are_ref[...] = ab_re
    aim_ref[...] = ab_im
    bbre_ref[...] = f_re * bre_ref[...] - f_im * bim_ref[...]
    bbim_ref[...] = f_re * bim_ref[...] + f_im * bre_ref[...]


def _s5_discretize(lam_re, lam_im, log_step, b_re, b_im):
    g, n = lam_re.shape
    ch = b_re.shape[-1]
    col = lambda a: a.reshape(g * n, 1)
    ls = jnp.broadcast_to(log_step[:, None], (g, n))
    return pl.pallas_call(
        _s5_discretize_kernel,
        out_shape=[jax.ShapeDtypeStruct((g * n, 1), F32)] * 2 + [jax.ShapeDtypeStruct((g * n, ch), F32)] * 2,
        name="s5_discretize",
    )(col(lam_re), col(lam_im), col(ls), b_re.reshape(g * n, ch), b_im.reshape(g * n, ch))


def _s5_kernel(u_ref, bmat_ref, arow_ref, h0_ref, cmat_ref, d_ref, wglu_ref, bglu_ref,
               y_ref, hfin_ref, hbuf_ref, hc_ref, *, tt, nb, ns, cb, nsplit):
    ti = pl.program_id(1)
    width = u_ref.shape[-1]

    @pl.when(ti == 0)
    def _():
        hc_ref[...] = h0_ref[...]

    u = u_ref[...].reshape(tt * nb, width)
    kw, nsh = width // nsplit, ns // nsplit
    u16 = u.astype(BF16)
    for hf in range(nsplit):
        chans = slice(hf * kw, (hf + 1) * kw)
        for part in (0, ns):
            states = slice(part + hf * nsh, part + (hf + 1) * nsh)
            hbuf_ref[:, states] = jnp.dot(u16[:, chans], bmat_ref[chans, states], preferred_element_type=F32)

    for c in range(ns // cb):
        re = slice(c * cb, (c + 1) * cb)
        im = slice(ns + c * cb, ns + (c + 1) * cb)
        a_re = jnp.broadcast_to(arow_ref[:, re], (SUBLANES, cb))
        a_im = jnp.broadcast_to(arow_ref[:, im], (SUBLANES, cb))

        def batch_rows(bb, _):
            r0 = pl.multiple_of(bb * SUBLANES, SUBLANES)

            def step(t, h):
                h_re, h_im = h
                r = pl.multiple_of(t * nb + r0, SUBLANES)
                n_re = a_re * h_re - a_im * h_im + hbuf_ref[pl.ds(r, SUBLANES), re]
                n_im = a_re * h_im + a_im * h_re + hbuf_ref[pl.ds(r, SUBLANES), im]
                hbuf_ref[pl.ds(r, SUBLANES), re] = n_re
                hbuf_ref[pl.ds(r, SUBLANES), im] = n_im
                return n_re, n_im

            h = (hc_ref[pl.ds(r0, SUBLANES), re], hc_ref[pl.ds(r0, SUBLANES), im])
            h = lax.fori_loop(0, tt, step, h, unroll=min(tt, 8))
            hc_ref[pl.ds(r0, SUBLANES), re] = h[0]
            hc_ref[pl.ds(r0, SUBLANES), im] = h[1]
            return 0

        lax.fori_loop(0, nb // SUBLANES, batch_rows, 0)

    halves = []
    for hf in range(nsplit):
        chans = slice(hf * kw, (hf + 1) * kw)
        acc = None
        for part in (0, ns):
            states = slice(part + hf * nsh, part + (hf + 1) * nsh)
            term = jnp.dot(hbuf_ref[:, states].astype(BF16), cmat_ref[states, chans], preferred_element_type=F32)
            acc = term if acc is None else acc + term
        halves.append(acc)
    y = (jnp.concatenate(halves, axis=1) if nsplit > 1 else halves[0]) + d_ref[...] * u
    y = 0.5 * y * (1.0 + jnp.tanh(math.sqrt(2.0 / math.pi) * (y + 0.044715 * (y * y * y))))
    z = jnp.dot(y.astype(BF16), wglu_ref[...], preferred_element_type=F32) + bglu_ref[...]
    y_ref[...] = (y * _sigmoid(z)).reshape(tt, nb, width)

    @pl.when(ti == pl.num_programs(1) - 1)
    def _():
        hfin_ref[...] = hc_ref[...]


S5_SLAB = 256


def _s5(u_tm, bmat, arow, h0, cmat, d_row, wglu, bglu, group_channels):
    t, b, width = u_tm.shape
    ns2 = bmat.shape[1]
    nsplit = width // S5_SLAB if width % S5_SLAB == 0 and S5_SLAB % group_channels == 0 else 1
    nb = _tile(b, 64, SUBLANES)
    tt = _tile(t, max(SUBLANES, 512 // nb), 1)
    const = lambda i, j: (0, 0)
    return pl.pallas_call(
        functools.partial(_s5_kernel, tt=tt, nb=nb, ns=ns2 // 2, cb=_tile(ns2 // 2, 512), nsplit=nsplit),
        grid=(b // nb, t // tt),
        in_specs=[
            pl.BlockSpec((tt, nb, width), lambda i, j: (j, i, 0)),
            pl.BlockSpec((width, ns2), const),
            pl.BlockSpec((1, ns2), const),
            pl.BlockSpec((nb, ns2), lambda i, j: (i, 0)),
            pl.BlockSpec((ns2, width), const),
            pl.BlockSpec((1, width), const),
            pl.BlockSpec((width, width), const),
            pl.BlockSpec((1, width), const),
        ],
        out_specs=[
            pl.BlockSpec((tt, nb, width), lambda i, j: (j, i, 0)),
            pl.BlockSpec((nb, ns2), lambda i, j: (i, 0)),
        ],
        out_shape=[jax.ShapeDtypeStruct((t, b, width), F32), jax.ShapeDtypeStruct((b, ns2), F32)],
        scratch_shapes=[pltpu.VMEM((tt * nb, ns2), F32), pltpu.VMEM((nb, ns2), F32)],
        compiler_params=_params("parallel", "arbitrary"),
        name="s5",
    )(u_tm, bmat, arow, h0, cmat, d_row, wglu, bglu)


def _split(a):
    hi = a.astype(BF16)
    return hi, (a - hi.astype(F32)).astype(BF16)


def _mm3(a, b):
    d = lambda x, y: jnp.dot(x, y, preferred_element_type=F32)
    return d(a[0], b[0]) + d(a[0], b[1]) + d(a[1], b[0])


def _block_tri_inverse(a, n, c):
    row = lax.broadcasted_iota(jnp.int32, (n, n), 0)
    col = lax.broadcasted_iota(jnp.int32, (n, n), 1)
    eye = (row == col).astype(F32)

    def same_block(b):
        s = b.bit_length() - 1
        return lax.shift_right_logical(row, s) == lax.shift_right_logical(col, s)

    b0 = min(c, TRI_BASE)
    neg = -jnp.where(same_block(b0), a, 0.0)
    t = eye + neg
    p = _split(neg)
    k = 1
    while 2 * k < b0:
        p = _split(_mm3(p, p))
        t = t + _mm3(_split(t), p)
        k *= 2
    b = b0
    while b < c:
        off = jnp.where(jnp.logical_and(same_block(2 * b), jnp.logical_not(same_block(b))), a, 0.0)
        ts = _split(t)
        t = t - _mm3(_split(_mm3(ts, _split(off))), ts)
        b *= 2
    return t


def _dn_kernel(q_ref, k_ref, v_ref, ba_ref, gate_ref, cw_ref, prm_ref, nrm_ref, cbuf_ref, s0_ref,
               o_ref, sout_ref, cout_ref,
               xpad_ref, qs_ref, ks_ref, vs_ref, gb_ref, s_ref, *, bs, tt, chunk, nh, hd, taps):
    ti = pl.program_id(1)
    width = nh * hd
    halo = SUBLANES
    first = halo - (taps - 1)

    @pl.when(ti == 0)
    def _():
        s_ref[...] = s0_ref[...]
        xpad_ref[:, first:halo, :] = cbuf_ref[...]

    xpad_ref[:, halo:halo + tt, 0:width] = q_ref[...]
    xpad_ref[:, halo:halo + tt, width:2 * width] = k_ref[...]
    xpad_ref[:, halo:halo + tt, 2 * width:3 * width] = v_ref[...]
    cw = cw_ref[...]
    y = cw[0:1, :] * xpad_ref[:, first:first + tt, :]
    for i in range(1, taps):
        y = y + cw[i:i + 1, :] * xpad_ref[:, first + i:first + i + tt, :]
    cout_ref[...] = xpad_ref[:, halo + tt - (taps - 1):halo + tt, :]
    xpad_ref[:, 0:halo, :] = xpad_ref[:, tt:tt + halo, :]
    act = y * _sigmoid(y)
    for h in range(nh):
        sl = slice(h * hd, (h + 1) * hd)
        qh = act[:, :, h * hd:(h + 1) * hd]
        kh = act[:, :, width + h * hd:width + (h + 1) * hd]
        qs_ref[:, :, sl] = qh * lax.rsqrt(jnp.sum(qh * qh, axis=-1, keepdims=True) + NORM_EPS) * (hd ** -0.5)
        ks_ref[:, :, sl] = kh * lax.rsqrt(jnp.sum(kh * kh, axis=-1, keepdims=True) + NORM_EPS)
    vs_ref[...] = act[:, :, 2 * width:3 * width]
    ba = ba_ref[...]
    lane = lax.broadcasted_iota(jnp.int32, ba.shape, 2)
    gval = -jnp.exp(prm_ref[0:1, :]) * _softplus(ba + prm_ref[1:2, :])
    gb_ref[...] = jnp.where(lane < nh, _sigmoid(ba), jnp.where(lane < 2 * nh, gval, 0.0))

    pairs = [(s, h) for s in range(bs) for h in range(nh)]
    npairs = len(pairs)
    nrow = npairs * chunk
    lc = chunk.bit_length() - 1
    row = lax.broadcasted_iota(jnp.int32, (nrow, nrow), 0)
    col = lax.broadcasted_iota(jnp.int32, (nrow, nrow), 1)
    same = lax.shift_right_logical(row, lc) == lax.shift_right_logical(col, lc)
    lower = jnp.logical_and(same, row >= col)
    strict = jnp.logical_and(same, row > col)
    tri = lower.astype(BF16)
    last_sel = col == (row | (chunk - 1))
    prow = lax.shift_right_logical(lax.broadcasted_iota(jnp.int32, (nrow, LANES), 0), lc)
    plane = lax.broadcasted_iota(jnp.int32, (nrow, LANES), 1)
    head_of_row = prow & (nh - 1)
    is_beta = plane == head_of_row
    is_g = plane == head_of_row + nh
    ones8 = jnp.ones((BF16_ROWS, LANES), BF16)
    pair_of_row = lax.shift_right_logical(lax.broadcasted_iota(jnp.int32, (nrow, hd), 0), lc)

    def expand(x):
        return jnp.concatenate([jnp.where(pair_of_row == p, x, 0.0) for p in range(npairs)], axis=1)

    def one_chunk(c):
        rows = pl.ds(c * chunk, chunk)
        stack = lambda ref: jnp.concatenate([ref[s, rows, h * hd:(h + 1) * hd] for s, h in pairs], axis=0)
        qst, kst, vst = stack(qs_ref), stack(ks_ref), stack(vs_ref)
        gbst = jnp.concatenate([gb_ref[s, rows, :] for s, _ in pairs], axis=0)
        g_hi, g_lo = _split(gbst)
        gcum_all = (jnp.dot(tri, g_hi, preferred_element_type=F32)
                    + jnp.dot(tri, g_lo, preferred_element_type=F32))
        gsel = jnp.where(is_g, gcum_all, 0.0)
        gc = jnp.sum(gsel, axis=-1, keepdims=True)
        beta = jnp.sum(jnp.where(is_beta, gbst, 0.0), axis=-1, keepdims=True)
        s_hi, s_lo = _split(gsel)
        grow = (lax.dot_general(ones8, s_hi, _NT, preferred_element_type=F32)
                + lax.dot_general(ones8, s_lo, _NT, preferred_element_type=F32))[0:1, :]
        decay = jnp.where(lower, jnp.exp(jnp.minimum(gc - grow, 0.0)), 0.0)
        kb = kst * beta
        a = _mm(kb, kst, _NT) * jnp.where(strict, decay, 0.0)
        tinv = _block_tri_inverse(a, nrow, chunk)
        u = _mm(tinv, vst * beta)
        w = _mm(tinv, kb * jnp.exp(gc))
        attn = _mm(qst, kst, _NT) * decay
        s_stack = jnp.concatenate([s_ref[s, h] for s, h in pairs], axis=0)
        v_new = u - _mm(expand(w), s_stack)
        o = _mm(expand(qst * jnp.exp(gc)), s_stack) + _mm(attn, v_new)
        g_last = jnp.sum(jnp.where(last_sel, grow, 0.0), axis=-1, keepdims=True)
        upd = _mm(expand(kst * jnp.exp(g_last - gc)), v_new, _TN)
        gt = stack(gate_ref)
        on = _rms(o, nrm_ref[...]) * (gt * _sigmoid(gt))
        for idx, (s, h) in enumerate(pairs):
            gl = g_last[idx * chunk:idx * chunk + 1, :]
            s_ref[s, h] = s_ref[s, h] * jnp.exp(gl) + upd[idx * hd:(idx + 1) * hd, :]
            o_ref[s, rows, h * hd:(h + 1) * hd] = on[idx * chunk:(idx + 1) * chunk, :]

    for c in range(tt // chunk):
        one_chunk(c)

    @pl.when(ti == pl.num_programs(1) - 1)
    def _():
        sout_ref[...] = s_ref[...]


def _deltanet(rest3, col0, conv_w, prm, nrm, cbuf, s0, nh, hd):
    nseq, t, _ = rest3.shape
    width = nh * hd
    taps = conv_w.shape[0]
    chunk = min(DN_CHUNK, t)
    assert t % chunk == 0 and chunk & (chunk - 1) == 0 and t >= taps - 1
    tt = _tile(t, 256, chunk)
    bs = _tile(nseq, 8, 1) if tt == t and t <= SUBLANES else 1
    ba_col = (col0 + 4) * (width // LANES)
    blk = lambda cidx: pl.BlockSpec((bs, tt, width), lambda i, j: (i, j, cidx))
    const = lambda i, j: (0, 0)
    return pl.pallas_call(
        functools.partial(_dn_kernel, bs=bs, tt=tt, chunk=chunk, nh=nh, hd=hd, taps=taps),
        grid=(nseq // bs, t // tt),
        in_specs=[
            blk(col0 + 1), blk(col0 + 2), blk(col0 + 3),
            pl.BlockSpec((bs, tt, LANES), lambda i, j: (i, j, ba_col)),
            blk(col0),
            pl.BlockSpec((taps, 3 * width), const),
            pl.BlockSpec((SUBLANES, LANES), const),
            pl.BlockSpec((1, hd), const),
            pl.BlockSpec((bs, taps - 1, 3 * width), lambda i, j: (i, 0, 0)),
            pl.BlockSpec((bs, nh, hd, hd), lambda i, j: (i, 0, 0, 0)),
        ],
        out_specs=[
            pl.BlockSpec((bs, tt, width), lambda i, j: (i, j, 0)),
            pl.BlockSpec((bs, nh, hd, hd), lambda i, j: (i, 0, 0, 0)),
            pl.BlockSpec((bs, taps - 1, 3 * width), lambda i, j: (i, 0, 0)),
        ],
        out_shape=[
            jax.ShapeDtypeStruct((nseq, t, width), F32),
            jax.ShapeDtypeStruct((nseq, nh, hd, hd), F32),
            jax.ShapeDtypeStruct((nseq, taps - 1, 3 * width), F32),
        ],
        scratch_shapes=[
            pltpu.VMEM((bs, tt + 2 * SUBLANES, 3 * width), F32),
            pltpu.VMEM((bs, tt, width), F32),
            pltpu.VMEM((bs, tt, width), F32),
            pltpu.VMEM((bs, tt, width), F32),
            pltpu.VMEM((bs, tt, LANES), F32),
            pltpu.VMEM((bs, nh, hd, hd), F32),
        ],
        compiler_params=_params("parallel", "arbitrary"),
        name="deltanet",
    )(rest3, rest3, rest3, rest3, rest3, conv_w, prm, nrm, cbuf, s0)


def _outproj_kernel(oa_ref, ob_ref, oc_ref, ga_ref, gb_ref, w_ref, x_ref, o_ref, mix_ref):
    wa, wb = oa_ref.shape[-1], ob_ref.shape[-1]

    @pl.when(pl.program_id(1) == 0)
    def _():
        mix_ref[:, 0:wa] = _rms(oa_ref[...], ga_ref[...]).astype(BF16)
        mix_ref[:, wa:wa + wb] = _rms(ob_ref[...], gb_ref[...]).astype(BF16)
        mix_ref[:, wa + wb:] = oc_ref[...].astype(BF16)

    o_ref[...] = x_ref[...] + jnp.dot(mix_ref[...], w_ref[...], preferred_element_type=F32)


def _outproj(oa, ob, oc, ga, gb, w, x2d):
    m, d = x2d.shape
    tm = _tile(m, 512, SUBLANES)
    _, d_in, tn = w.shape
    row = lambda a: pl.BlockSpec((tm, a.shape[-1]), lambda i, j: (i, 0))
    const = lambda a: pl.BlockSpec((1, a.shape[-1]), lambda i, j: (0, 0))
    return pl.pallas_call(
        _outproj_kernel,
        grid=(m // tm, d // tn),
        in_specs=[row(oa), row(ob), row(oc), const(ga), const(gb),
                  pl.BlockSpec((None, d_in, tn), lambda i, j: (j, 0, 0)),
                  pl.BlockSpec((tm, tn), lambda i, j: (i, j))],
        out_specs=pl.BlockSpec((tm, tn), lambda i, j: (i, j)),
        out_shape=jax.ShapeDtypeStruct((m, d), F32),
        scratch_shapes=[pltpu.VMEM((tm, d_in), BF16)],
        compiler_params=_params("parallel", "arbitrary"),
        name="outproj",
    )(oa, ob, oc, ga, gb, w, x2d)


FFN_SUBTILE = 256


def _ffn_kernel(x_ref, g_ref, wg_ref, wv_ref, wd_ref, cw_ref, st_ref, o_ref, buf_ref, h_ref, act_ref, carry_ref,
                *, nf, tiles_per_seq, per_seq_rows):
    i = pl.program_id(0)
    f = pl.program_id(1)
    tm, tf = act_ref.shape

    def up():
        h = h_ref[...]
        if not per_seq_rows:
            @pl.when((i % tiles_per_seq) == 0)
            def _():
                carry_ref[f] = st_ref[...]

        tc = FFN_SUBTILE if tf % FFN_SUBTILE == 0 else tf
        row = lax.broadcasted_iota(jnp.int32, (tm, tc), 0)
        for s in range(tf // tc):
            cols = slice(s * tc, (s + 1) * tc)
            gate = jnp.dot(h, wg_ref[:, cols], preferred_element_type=F32)
            val = jnp.dot(h, wv_ref[:, cols], preferred_element_type=F32)
            prev1 = pltpu.roll(gate, 1, axis=0)
            prev2 = pltpu.roll(gate, 2, axis=0)
            if per_seq_rows:
                nseq = tm // per_seq_rows
                expand = lambda a: jnp.broadcast_to(a, (nseq, per_seq_rows, tc)).reshape(tm, tc)
                s0, s1 = expand(st_ref[:, 0:1, cols]), expand(st_ref[:, 1:2, cols])
                t = row & (per_seq_rows - 1)
                buf_ref[:, :, cols] = gate.reshape(nseq, per_seq_rows, tc)[:, per_seq_rows - 2:, :]
            else:
                s0, s1 = carry_ref[f, 0:1, cols], carry_ref[f, 1:2, cols]
                t = row
                tail = gate[tm - 2:tm, :]
                carry_ref[f, :, cols] = tail
                buf_ref[:, cols] = tail
            x1 = jnp.where(t >= 1, prev1, s1)
            x2 = jnp.where(t >= 2, prev2, jnp.where(t == 1, s1, s0))
            conv = cw_ref[2:3, cols] * gate + cw_ref[1:2, cols] * x1 + cw_ref[0:1, cols] * x2
            act_ref[:, cols] = (conv * _sigmoid(conv) * val).astype(BF16)

    def down():
        o_ref[...] += jnp.dot(act_ref[...], wd_ref[...], preferred_element_type=F32)

    @pl.when(f == 0)
    def _():
        h_ref[...] = _rms(x_ref[...], g_ref[...]).astype(BF16)
        o_ref[...] = x_ref[...]
        up()

    @pl.when(jnp.logical_and(f > 0, f < nf))
    def _():
        down()
        up()

    @pl.when(f == nf)
    def _():
        down()


def _ffn(x2d, gain, wg, wv, wd, conv_w, state, seq):
    m, d = x2d.shape
    d_ff = wg.shape[1]
    tf = _tile(d_ff, FFN_TILE)
    nf = d_ff // tf
    nseq = m // seq
    assert conv_w.shape[0] == 3
    up_tile = lambda f: jnp.minimum(f, nf - 1)
    down_tile = lambda f: jnp.maximum(f - 1, 0)
    if seq <= SUBLANES:
        assert seq == SUBLANES
        tm = _tile(m, FFN_ROWS, SUBLANES)
        per_seq_rows, tiles_per_seq = seq, 1
        st_spec = buf_spec = pl.BlockSpec((tm // seq, 2, tf), lambda i, f: (i, 0, up_tile(f)))
        n_buf = nseq
    else:
        tm = _tile(seq, FFN_ROWS, SUBLANES)
        per_seq_rows, tiles_per_seq = 0, seq // tm
        st_spec = pl.BlockSpec((None, 2, tf), lambda i, f: (i // tiles_per_seq, 0, up_tile(f)))
        buf_spec = pl.BlockSpec((None, 2, tf), lambda i, f: (i, 0, up_tile(f)))
        n_buf = m // tm
    y, buf = pl.pallas_call(
        functools.partial(_ffn_kernel, nf=nf, tiles_per_seq=tiles_per_seq, per_seq_rows=per_seq_rows),
        grid=(m // tm, nf + 1),
        in_specs=[
            pl.BlockSpec((tm, d), lambda i, f: (i, 0), pipeline_mode=pl.Buffered(1)),
            pl.BlockSpec((1, d), lambda i, f: (0, 0)),
            pl.BlockSpec((d, tf), lambda i, f: (0, up_tile(f))),
            pl.BlockSpec((d, tf), lambda i, f: (0, up_tile(f))),
            pl.BlockSpec((tf, d), lambda i, f: (down_tile(f), 0)),
            pl.BlockSpec((3, tf), lambda i, f: (0, up_tile(f))),
            st_spec,
        ],
        out_specs=[pl.BlockSpec((tm, d), lambda i, f: (i, 0)), buf_spec],
        out_shape=[jax.ShapeDtypeStruct((m, d), F32), jax.ShapeDtypeStruct((n_buf, 2, d_ff), F32)],
        scratch_shapes=[pltpu.VMEM((tm, d), BF16), pltpu.VMEM((tm, tf), BF16), pltpu.VMEM((nf, 2, tf), F32)],
        compiler_params=_params("arbitrary", "arbitrary"),
        name="convffn",
    )(x2d, gain, wg, wv, wd, conv_w, state)
    if not per_seq_rows:
        buf = buf.reshape(nseq, tiles_per_seq, 2, d_ff)[:, tiles_per_seq - 1]
    return y, buf


FFN_TILE = 512
FFN_ROWS = 1024
OUT_TILE = 1024


def _col_tiles(w, tn):
    d, n = w.shape
    return w.reshape(d, n // tn, tn).transpose(1, 0, 2)


def _layer_weights(l, w_in, sb_q_norm, sb_k_norm, ssm, dn_a_log, dn_dt_bias, ssm_w_glu, w_out, ffn_w_in, ffn_w_out,
                   dims):
    sbw, ssmw, dnw, nh = dims["sbw"], dims["ssmw"], dims["dnw"], dims["dn_heads"]
    heads, hd = dims["sb_heads"], dims["sb_hd"]
    d = w_in.shape[1]
    o_u = 3 * sbw
    o_qkv = o_u + ssmw
    o_ba = o_qkv + 3 * dnw
    o_gate = o_ba + 2 * nh
    n_rest = -(-(ssmw + 4 * dnw + LANES) // sbw) * sbw
    pad = n_rest - (ssmw + 4 * dnw + 2 * nh)
    w = w_in[l]
    w_perm = jnp.concatenate([
        w[:, :o_u], w[:, o_u:o_qkv], w[:, o_gate:o_gate + dnw], w[:, o_qkv:o_ba], w[:, o_ba:o_gate],
        jnp.zeros((d, pad), w.dtype)], axis=1).astype(BF16)
    head_of = jnp.arange(sbw) // hd
    gs = (head_of[:, None] == jnp.arange(LANES)[None, :]).astype(BF16)
    qg = jnp.tile(sb_q_norm[l], heads)[None, :]
    kg = jnp.tile(sb_k_norm[l], heads)[None, :]
    a_re, a_im, bb_re, bb_im = ssm["disc"]
    g, n, ch = dims["ssm_groups"], dims["ssm_state"], dims["ssm_ch"]
    ns = g * n
    grp_of_state = jnp.arange(ns) // n
    grp_of_chan = jnp.arange(ssmw) // ch
    blockmask = grp_of_chan[:, None] == grp_of_state[None, :]
    expand_b = lambda bb: jnp.where(blockmask, jnp.tile(bb.T, (g, 1)), 0.0)
    bmat = jnp.concatenate([expand_b(bb_re), expand_b(bb_im)], axis=1).astype(BF16)
    arow = jnp.concatenate([a_re, a_im], axis=0).reshape(1, 2 * ns)
    expand_c = lambda c: jnp.where(blockmask.T, jnp.tile(c.transpose(0, 2, 1).reshape(ns, ch), (1, g)), 0.0)
    cmat = jnp.concatenate([expand_c(ssm["c_re"][l]), -expand_c(ssm["c_im"][l])], axis=0).astype(BF16)
    prm = jnp.zeros((SUBLANES, LANES), F32)
    prm = prm.at[0, nh:2 * nh].set(dn_a_log[l]).at[1, nh:2 * nh].set(dn_dt_bias[l])
    d_ff = ffn_w_in.shape[2] // 2
    return dict(
        w_perm=_col_tiles(w_perm, sbw), gs=gs, e=gs.T, qg=qg, kg=kg, bmat=bmat, arow=arow, cmat=cmat, prm=prm,
        wglu=ssm_w_glu[l].astype(BF16), w_out=_col_tiles(w_out[l].astype(BF16), _tile(w_out.shape[2], OUT_TILE)),
        wg=ffn_w_in[l][:, :d_ff].astype(BF16), wv=ffn_w_in[l][:, d_ff:].astype(BF16),
        wd=ffn_w_out[l].astype(BF16), n_rest=n_rest)


def kernel(x_prompt, x_sample, cache_k, cache_v, page_table, state_ssm_re, state_ssm_im, state_delta, state_delta_conv, state_ffn_conv, norm_mix, w_in, sb_q_norm, sb_k_norm, sb_logit_bias, ssm_lambda_re, ssm_lambda_im, ssm_log_step, ssm_b_re, ssm_b_im, ssm_c_re, ssm_c_im, ssm_d, ssm_w_glu, ssm_b_glu, dn_conv_w, dn_a_log, dn_dt_bias, dn_norm, norm_out_sb, norm_out_ssm, w_out, norm_ffn, ffn_w_in, ffn_conv_w, ffn_w_out):
    depth = w_in.shape[0]
    bp, tp, d = x_prompt.shape
    bd, td, _ = x_sample.shape
    sb_heads, sb_hd = cache_k.shape[-2], cache_k.shape[-1]
    g, n = ssm_lambda_re.shape[1:]
    ch = ssm_b_re.shape[-1]
    dn_heads, dn_hd = dn_a_log.shape[1], dn_norm.shape[1]
    dims = dict(sbw=sb_heads * sb_hd, sb_heads=sb_heads, sb_hd=sb_hd, ssmw=g * ch, ssm_groups=g, ssm_state=n,
                ssm_ch=ch, dnw=dn_heads * dn_hd, dn_heads=dn_heads)
    sbw, ssmw, dnw, ns = dims["sbw"], dims["ssmw"], dims["dnw"], g * n
    assert ssmw % dnw == 0 and dnw % LANES == 0 and sbw % LANES == 0 and LANES % sb_hd == 0
    d_ff = ffn_conv_w.shape[2]
    col0 = ssmw // dnw
    q_scale = sb_hd ** -0.5 * LOG2E

    groups = {
        "p": dict(x=x_prompt.reshape(bp * tp, d), b=bp, t=tp, kbuf=None, vbuf=None, outs=[]),
        "s": dict(x=x_sample.reshape(bd * td, d), b=bd, t=td, kbuf=None, vbuf=None, outs=[]),
    }
    for l in range(depth):
        disc = _s5_discretize(ssm_lambda_re[l], ssm_lambda_im[l], ssm_log_step[l], ssm_b_re[l], ssm_b_im[l])
        lw = _layer_weights(l, w_in, sb_q_norm, sb_k_norm, dict(disc=disc, c_re=ssm_c_re, c_im=ssm_c_im),
                            dn_a_log, dn_dt_bias, ssm_w_glu, w_out, ffn_w_in, ffn_w_out, dims)
        for name, grp in groups.items():
            b, t, x = grp["b"], grp["t"], grp["x"]
            q, kbuf, vbuf, rest = _inproj(x, norm_mix[l][None, :], lw["w_perm"], lw["gs"], lw["e"], lw["qg"],
                                          lw["kg"], grp["kbuf"], grp["vbuf"], l, depth, sb_hd, q_scale, t,
                                          time_minor=(name == "p"))
            grp["kbuf"], grp["vbuf"] = kbuf, vbuf
            rest3 = rest.reshape(b, t, lw["n_rest"])
            if name == "p":
                o_a = _sb_prompt(q, kbuf, vbuf, sb_logit_bias[l], l, b, t, sb_hd)
                h0 = jnp.zeros((b, 2 * ns), F32)
                dn_s0 = jnp.zeros((b, dn_heads, dn_hd, dn_hd), F32)
                dn_c0 = jnp.zeros((b, dn_conv_w.shape[1] - 1, 3 * dnw), F32)
                ffn_c0 = jnp.zeros((b, ffn_conv_w.shape[1] - 1, d_ff), F32)
            else:
                o_a = _sb_sample(q, kbuf, vbuf, cache_k, cache_v, page_table, sb_logit_bias[l], l, b, t, sb_hd)
                h0 = jnp.concatenate([state_ssm_re[l].reshape(b, ns), state_ssm_im[l].reshape(b, ns)], axis=1)
                dn_s0, dn_c0, ffn_c0 = state_delta[l], state_delta_conv[l], state_ffn_conv[l]
            u_tm = rest3[:, :, :ssmw].transpose(1, 0, 2)
            y_tm, h_fin = _s5(u_tm, lw["bmat"], lw["arow"], h0, lw["cmat"], ssm_d[l][None, :], lw["wglu"],
                              ssm_b_glu[l][None, :], ch)
            o_b = y_tm.transpose(1, 0, 2).reshape(b * t, ssmw)
            o_c, dn_s, dn_c = _deltanet(rest3, col0, dn_conv_w[l], lw["prm"], dn_norm[l][None, :], dn_c0, dn_s0,
                                        dn_heads, dn_hd)
            x = _outproj(o_a, o_b, o_c.reshape(b * t, dnw), norm_out_sb[l][None, :], norm_out_ssm[l][None, :],
                         lw["w_out"], x)
            x, ffn_c = _ffn(x, norm_ffn[l][None, :], lw["wg"], lw["wv"], lw["wd"], ffn_conv_w[l], ffn_c0, t)
            grp["x"] = x
            grp["outs"].append((h_fin[:, :ns].reshape(b, g, n), h_fin[:, ns:].reshape(b, g, n), dn_s, dn_c, ffn_c))

    res = {}
    for name, grp in groups.items():
        b, t = grp["b"], grp["t"]
        sre, sim, dn_s, dn_c, ffn_c = [jnp.stack(z) for z in zip(*grp["outs"])]
        if name == "p":
            heads_last = lambda a: a.reshape(depth, b, sb_heads, sb_hd, t).transpose(0, 1, 4, 2, 3)
        else:
            heads_last = lambda a: a.reshape(depth, b, t, sb_heads, sb_hd)
        res[name] = dict(
            y=grp["x"].reshape(b, t, d), k=heads_last(grp["kbuf"]), v=heads_last(grp["vbuf"]),
            sre=sre, sim=sim, dn_s=dn_s, dn_c=dn_c, ffn_c=ffn_c)
    p, s = res["p"], res["s"]
    return (p["y"], s["y"], p["k"], p["v"], s["k"], s["v"], p["sre"], p["sim"], s["sre"], s["sim"],
            p["dn_s"], s["dn_s"], p["dn_c"], s["dn_c"], p["ffn_c"], s["ffn_c"])
```

```python
import functools
import math

import jax
import jax.numpy as jnp
from jax import lax
from jax.experimental import pallas as pl
from jax.experimental.pallas import tpu as pltpu

F32 = jnp.float32
BF16 = jnp.bfloat16
HIGHEST = lax.Precision.HIGHEST
NORM_EPS = 1e-6
LANES = 128
SUBLANES = 8
BF16_ROWS = 16
VMEM_LIMIT_BYTES = 56 * 1024 * 1024
DN_CHUNK = 64
TRI_BASE = 16

_NT = (((1,), (1,)), ((), ()))
_TN = (((0,), (0,)), ((), ()))


def _tile(dim, pref, mult=LANES):
    if dim <= pref:
        return dim
    t = (pref // mult) * mult
    while t >= mult:
        if dim % t == 0:
            return t
        t -= mult
    raise ValueError(f"no tile for {dim} (pref {pref}, mult {mult})")


def _params(*sem):
    return pltpu.CompilerParams(dimension_semantics=sem, vmem_limit_bytes=VMEM_LIMIT_BYTES)


def _mm(a, b, dims=None):
    ok = all(d % BF16_ROWS == 0 for d in a.shape + b.shape)
    if ok:
        a, b = a.astype(BF16), b.astype(BF16)
    else:
        a, b = a.astype(F32), b.astype(F32)
    if dims is None:
        return jnp.dot(a, b, preferred_element_type=F32)
    return lax.dot_general(a, b, dims, preferred_element_type=F32)


def _mm_f32(a, b, dims=None):
    if dims is None:
        return jnp.dot(a, b, preferred_element_type=F32, precision=HIGHEST)
    return lax.dot_general(a, b, dims, preferred_element_type=F32, precision=HIGHEST)


def _softplus(z):
    return jnp.maximum(z, 0.0) + jnp.log1p(jnp.exp(-jnp.abs(z)))


def _sigmoid(z):
    return 1.0 / (1.0 + jnp.exp(-z))


def _rms(x, gain, eps=NORM_EPS):
    return x * lax.rsqrt(jnp.mean(x * x, axis=-1, keepdims=True) + eps) * gain


INPROJ_ROWS = 512
SB_PAGES_PER_STEP = 16


def _inproj_kernel(x_ref, g_ref, w_ref, gs_ref, e_ref, qg_ref, kg_ref, *refs, head_dim, q_scale, aliased,
                   time_minor):
    q_ref, k_ref, v_ref, r_ref, xn_ref = refs[2:] if aliased else refs
    j = pl.program_id(1)
    kv_layout = (lambda a: a.T) if time_minor else (lambda a: a)

    @pl.when(j == 0)
    def _():
        xn_ref[...] = _rms(x_ref[...], g_ref[...]).astype(BF16)

    y = jnp.dot(xn_ref[...], w_ref[...], preferred_element_type=F32)

    def head_norm(gain):
        ms = jnp.dot((y * y).astype(BF16), gs_ref[...], preferred_element_type=F32) * (1.0 / head_dim)
        inv = lax.rsqrt(ms + NORM_EPS)
        hi = inv.astype(BF16)
        lo = (inv - hi.astype(F32)).astype(BF16)
        sc = (jnp.dot(hi, e_ref[...], preferred_element_type=F32)
              + jnp.dot(lo, e_ref[...], preferred_element_type=F32))
        return y * sc * gain

    @pl.when(j == 0)
    def _():
        q_ref[...] = (head_norm(qg_ref[...]) * q_scale).astype(BF16)

    @pl.when(j == 1)
    def _():
        k_ref[...] = kv_layout(head_norm(kg_ref[...]))

    @pl.when(j == 2)
    def _():
        v_ref[...] = kv_layout(y)

    @pl.when(j >= 3)
    def _():
        r_ref[...] = y


def _inproj(x2d, gain, w_perm, gs, e, qg, kg, kbuf, vbuf, layer, depth, head_dim, q_scale, seq, time_minor):
    m, d = x2d.shape
    tn = gs.shape[0]
    assert w_perm.shape[1:] == (d, tn)
    n_rest = (w_perm.shape[0] - 3) * tn
    tm = _tile(seq, INPROJ_ROWS, LANES) if time_minor else _tile(m, INPROJ_ROWS, SUBLANES)
    tiles_per_seq = seq // tm
    aliased = kbuf is not None
    const = lambda i, j: (0, 0)
    in_specs = [
        pl.BlockSpec((tm, d), lambda i, j: (i, 0)),
        pl.BlockSpec((1, d), const),
        pl.BlockSpec((None, d, tn), lambda i, j: (j, 0, 0)),
        pl.BlockSpec((tn, LANES), const),
        pl.BlockSpec((LANES, tn), const),
        pl.BlockSpec((1, tn), const),
        pl.BlockSpec((1, tn), const),
    ]
    args = [x2d, gain, w_perm, gs, e, qg, kg]
    aliases = {}
    if aliased:
        in_specs += [pl.BlockSpec(memory_space=pl.ANY), pl.BlockSpec(memory_space=pl.ANY)]
        args += [kbuf, vbuf]
        aliases = {7: 1, 8: 2}
    if time_minor:
        kv_spec = pl.BlockSpec((None, None, tn, tm), lambda i, j: (layer, i // tiles_per_seq, 0, i % tiles_per_seq))
        kv_shape = jax.ShapeDtypeStruct((depth, m // seq, tn, seq), F32)
    else:
        kv_spec = pl.BlockSpec((None, tm, tn), lambda i, j: (layer, i, 0))
        kv_shape = jax.ShapeDtypeStruct((depth, m, tn), F32)
    out_specs = [
        pl.BlockSpec((tm, tn), lambda i, j: (i, 0)),
        kv_spec,
        kv_spec,
        pl.BlockSpec((tm, tn), lambda i, j: (i, jnp.maximum(j - 3, 0))),
    ]
    out_shape = [jax.ShapeDtypeStruct((m, tn), BF16), kv_shape, kv_shape, jax.ShapeDtypeStruct((m, n_rest), F32)]
    return pl.pallas_call(
        functools.partial(_inproj_kernel, head_dim=head_dim, q_scale=q_scale, aliased=aliased,
                          time_minor=time_minor),
        grid=(m // tm, 3 + n_rest // tn),
        in_specs=in_specs, out_specs=out_specs, out_shape=out_shape,
        scratch_shapes=[pltpu.VMEM((tm, d), BF16)],
        input_output_aliases=aliases,
        compiler_params=_params("parallel", "arbitrary"),
        name="inproj",
    )(*args)


LOG2E = math.log2(math.e)


def _softplus2(zs):
    neg_abs = lax.bitcast_convert_type(lax.bitcast_convert_type(zs, jnp.uint32) | jnp.uint32(0x80000000), F32)
    return jnp.maximum(zs, 0.0) + jnp.log2(1.0 + jnp.exp2(neg_abs))


SB_BLOCK = 256
SB_ROW_SUBBLOCKS = 8


def _sb_prompt_kernel(bias_ref, q_ref, k_ref, v_ref, o_ref, *, blk, nsub, head_dim, heads_per_step):
    hp = pl.program_id(1)
    qi = pl.program_id(2)
    row = lax.broadcasted_iota(jnp.int32, (blk, blk), 0)
    col = lax.broadcasted_iota(jnp.int32, (blk, blk), 1)
    later = (row > col).astype(BF16)
    visible = col < row
    q = q_ref[...]
    heads = range(heads_per_step)
    lanes = [slice(hh * head_dim, (hh + 1) * head_dim) for hh in heads]
    qs = [[q[j * blk:(j + 1) * blk, lanes[hh]] for j in range(nsub)] for hh in heads]
    biases = [bias_ref[hp * heads_per_step + hh] * LOG2E for hh in heads]

    def load(ki):
        r0 = pl.multiple_of(ki * blk, blk)
        kb = k_ref[:, pl.ds(r0, blk)].astype(BF16)
        vb = v_ref[:, pl.ds(r0, blk)].astype(BF16)
        return [(kb[lanes[hh], :], vb[lanes[hh], :]) for hh in heads]

    def block(qh, kv, bias, oc, masked):
        o_acc, carry = oc
        z = jnp.dot(qh, kv[0], preferred_element_type=F32) + bias
        sp = _softplus2(z)
        if masked:
            sp = jnp.where(visible, sp, 0.0)
        after = jnp.dot(sp.astype(BF16), later, preferred_element_type=F32)
        w = jnp.exp2(z - sp - after - carry)
        if masked:
            w = jnp.where(visible, w, 0.0)
        o_acc = o_acc + lax.dot_general(w.astype(BF16), kv[1], _NT, preferred_element_type=F32)
        return o_acc, carry + after[:, 0:1] + sp[:, 0:1]

    zero = (jnp.zeros((blk, head_dim), F32), jnp.zeros((blk, 1), F32))
    state = [[zero for _ in range(nsub)] for _ in heads]
    for d in reversed(range(nsub)):
        kv = load(qi * nsub + d)
        for hh in heads:
            for j in range(d, nsub):
                state[hh][j] = block(qs[hh][j], kv[hh], biases[hh], state[hh][j], j == d)

    def older(it, flat):
        kv = load(qi * nsub - 1 - it)
        return tuple(block(qs[hh][j], kv[hh], biases[hh], flat[hh * nsub + j], False)
                     for hh in heads for j in range(nsub))

    flat = lax.fori_loop(0, qi * nsub, older, tuple(state[hh][j] for hh in heads for j in range(nsub)))
    o_ref[...] = jnp.concatenate(
        [jnp.concatenate([flat[hh * nsub + j][0] for hh in heads], axis=-1) for j in range(nsub)], axis=0)


def _sb_prompt(q, kbuf, vbuf, bias, layer, bsz, seq, head_dim):
    width = q.shape[-1]
    heads_per_step = LANES // head_dim
    blk = _tile(seq, SB_BLOCK)
    nsub = max(n for n in (1, 2, 4, 8) if n <= SB_ROW_SUBBLOCKS and seq % (n * blk) == 0)
    rows = nsub * blk
    q3 = q.reshape(bsz, seq, width)
    return pl.pallas_call(
        functools.partial(_sb_prompt_kernel, blk=blk, nsub=nsub, head_dim=head_dim, heads_per_step=heads_per_step),
        grid=(bsz, width // LANES, seq // rows),
        in_specs=[
            pl.BlockSpec(memory_space=pltpu.SMEM),
            pl.BlockSpec((None, rows, LANES), lambda b, h, i: (b, i, h)),
            pl.BlockSpec((None, None, LANES, seq), lambda b, h, i: (layer, b, h, 0)),
            pl.BlockSpec((None, None, LANES, seq), lambda b, h, i: (layer, b, h, 0)),
        ],
        out_specs=pl.BlockSpec((None, rows, LANES), lambda b, h, i: (b, i, h)),
        out_shape=jax.ShapeDtypeStruct((bsz, seq, width), F32),
        compiler_params=_params("parallel", "parallel", "arbitrary"),
        name="sb_prompt",
    )(bias, q3, kbuf, vbuf).reshape(bsz * seq, width)


SB_PAGE_GROUP = 4


def _sb_sample_kernel(pt_ref, q_ref, bcol_ref, kn_ref, vn_ref, *refs, pages_per_step, tq, heads, head_dim):
    del pt_ref
    kp = refs[:pages_per_step]
    vp = refs[pages_per_step:2 * pages_per_step]
    o_ref, acc_ref, carry_ref = refs[2 * pages_per_step:]
    c = pl.program_id(1)
    nr = heads * tq
    ltq = tq.bit_length() - 1
    page = kp[0].shape[-1]
    qf = q_ref[...].astype(F32)
    width = qf.shape[-1]
    bcol = bcol_ref[...] * LOG2E
    r = lax.broadcasted_iota(jnp.int32, (nr, width), 0)
    col = lax.broadcasted_iota(jnp.int32, (nr, width), 1)
    own_head = lax.shift_right_logical(r, ltq) == lax.shift_right_logical(col, head_dim.bit_length() - 1)
    qbd = jnp.where(own_head, jnp.concatenate([qf] * heads, axis=0), 0.0)

    @pl.when(c == 0)
    def _():
        z = lax.dot_general(qbd, kn_ref[...], _NT, preferred_element_type=F32) + bcol
        key = lax.broadcasted_iota(jnp.int32, (nr, tq), 1)
        qry = lax.broadcasted_iota(jnp.int32, (nr, tq), 0) & (tq - 1)
        vis = key < qry
        sp = jnp.where(vis, _softplus2(z), 0.0)
        later = (lax.broadcasted_iota(jnp.int32, (tq, tq), 0)
                 > lax.broadcasted_iota(jnp.int32, (tq, tq), 1)).astype(F32)
        w = jnp.where(vis, jnp.exp2(z - sp - _mm_f32(sp, later)), 0.0)
        acc_ref[...] = jnp.dot(w, vn_ref[...], preferred_element_type=F32)
        carry_ref[...] = jnp.sum(sp, axis=-1, keepdims=True)

    qbd16 = qbd.astype(BF16)
    gsz = SB_PAGE_GROUP if pages_per_step % SB_PAGE_GROUP == 0 else 1
    nk = gsz * page
    later = (lax.broadcasted_iota(jnp.int32, (nk, nk), 0)
             > lax.broadcasted_iota(jnp.int32, (nk, nk), 1)).astype(BF16)
    side_by_side = lambda refs: jnp.concatenate(
        [r[...].reshape(width, page).astype(BF16) for r in reversed(refs)], axis=1)
    for g in range(pages_per_step // gsz):
        kt = side_by_side(kp[g * gsz:(g + 1) * gsz])
        z = jnp.dot(qbd16, kt, preferred_element_type=F32) + bcol
        sp = _softplus2(z)
        after = jnp.dot(sp.astype(BF16), later, preferred_element_type=F32)
        carry = carry_ref[...]
        w = jnp.exp2(z - sp - after - carry).astype(BF16)
        vt = side_by_side(vp[g * gsz:(g + 1) * gsz])
        acc_ref[...] += lax.dot_general(w, vt, _NT, preferred_element_type=F32)
        carry_ref[...] = carry + after[:, 0:1] + sp[:, 0:1]

    @pl.when(c == pl.num_programs(1) - 1)
    def _():
        acc = jnp.where(own_head, acc_ref[...], 0.0)
        o_ref[...] = acc.reshape(heads, tq, width).sum(axis=0)


def _sb_sample(q, kbuf, vbuf, cache_k, cache_v, page_table, bias, layer, bsz, tq, head_dim):
    width = q.shape[-1]
    heads = width // head_dim
    assert tq & (tq - 1) == 0 and head_dim & (head_dim - 1) == 0
    page = cache_k.shape[2]
    npages = page_table.shape[1]
    pps = _tile(npages, SB_PAGES_PER_STEP, 1)
    bcol =jnp.repeat(bias.astype(F32), tq)[:, None]
    cache_k = cache_k.transpose(0, 1, 3, 4, 2)
    cache_v = cache_v.transpose(0, 1, 3, 4, 2)

    def page_map(p):
        return lambda b, c, pt: (layer, pt[b * npages + npages - 1 - (c * pps + p)], 0, 0, 0)

    page_specs = [pl.BlockSpec((None, None, heads, head_dim, page), page_map(p)) for p in range(pps)]
    grid_spec = pltpu.PrefetchScalarGridSpec(
        num_scalar_prefetch=1,
        grid=(bsz, npages // pps),
        in_specs=[
            pl.BlockSpec((None, tq, width), lambda b, c, pt: (b, 0, 0)),
            pl.BlockSpec((heads * tq, 1), lambda b, c, pt: (0, 0)),
            pl.BlockSpec((None, None, tq, width), lambda b, c, pt: (layer, b, 0, 0)),
            pl.BlockSpec((None, None, tq, width), lambda b, c, pt: (layer, b, 0, 0)),
        ] + page_specs + page_specs,
        out_specs=pl.BlockSpec((None, tq, width), lambda b, c, pt: (b, 0, 0)),
        scratch_shapes=[pltpu.VMEM((heads * tq, width), F32), pltpu.VMEM((heads * tq, 1), F32)],
    )
    k4 = kbuf.reshape(kbuf.shape[0], bsz, tq, width)
    v4 = vbuf.reshape(vbuf.shape[0], bsz, tq, width)
    out = pl.pallas_call(
        functools.partial(_sb_sample_kernel, pages_per_step=pps, tq=tq, heads=heads, head_dim=head_dim),
        grid_spec=grid_spec,
        out_shape=jax.ShapeDtypeStruct((bsz, tq, width), F32),
        compiler_params=_params("parallel", "arbitrary"),
        name="sb_sample",
    )(page_table.reshape(-1), q.reshape(bsz, tq, width), bcol, k4, v4,
      *([cache_k] * pps), *([cache_v] * pps))
    return out.reshape(bsz * tq, width)


def _s5_discretize_kernel(lre_ref, lim_ref, ls_ref, bre_ref, bim_ref, are_ref, aim_ref, bbre_ref, bbim_ref):
    lam_re, lam_im = lre_ref[...], lim_ref[...]
    dt = jnp.exp(ls_ref[...])
    mag = jnp.exp(lam_re * dt)
    ab_re = mag * jnp.cos(lam_im * dt)
    ab_im = mag * jnp.sin(lam_im * dt)
    den = lam_re * lam_re + lam_im * lam_im
    xr = ab_re - 1.0
    f_re = (xr * lam_re + ab_im * lam_im) / den
    f_im = (ab_im * lam_re - xr * lam_im) / den
    are_ref[...] = ab_re
    aim_ref[...] = ab_im
    bbre_ref[...] = f_re * bre_ref[...] - f_im * bim_ref[...]
    bbim_ref[...] = f_re * bim_ref[...] + f_im * bre_ref[...]


def _s5_discretize(lam_re, lam_im, log_step, b_re, b_im):
    g, n = lam_re.shape
    ch = b_re.shape[-1]
    col = lambda a: a.reshape(g * n, 1)
    ls = jnp.broadcast_to(log_step[:, None], (g, n))
    return pl.pallas_call(
        _s5_discretize_kernel,
        out_shape=[jax.ShapeDtypeStruct((g * n, 1), F32)] * 2 + [jax.ShapeDtypeStruct((g * n, ch), F32)] * 2,
        name="s5_discretize",
    )(col(lam_re), col(lam_im), col(ls), b_re.reshape(g * n, ch), b_im.reshape(g * n, ch))


def _s5_kernel(u_ref, bmat_ref, arow_ref, h0_ref, cmat_ref, d_ref, wglu_ref, bglu_ref,
               y_ref, hfin_ref, hbuf_ref, hc_ref, *, tt, nb, ns, cb, nsplit):
    ti = pl.program_id(1)
    width = u_ref.shape[-1]

    @pl.when(ti == 0)
    def _():
        hc_ref[...] = h0_ref[...]

    u = u_ref[...].reshape(tt * nb, width)
    kw, nsh = width // nsplit, ns // nsplit
    u16 = u.astype(BF16)
    for hf in range(nsplit):
        chans = slice(hf * kw, (hf + 1) * kw)
        for part in (0, ns):
            states = slice(part + hf * nsh, part + (hf + 1) * nsh)
            hbuf_ref[:, states] = jnp.dot(u16[:, chans], bmat_ref[chans, states], preferred_element_type=F32)

    for c in range(ns // cb):
        re = slice(c * cb, (c + 1) * cb)
        im = slice(ns + c * cb, ns + (c + 1) * cb)
        a_re = jnp.broadcast_to(arow_ref[:, re], (SUBLANES, cb))
        a_im = jnp.broadcast_to(arow_ref[:, im], (SUBLANES, cb))

        def batch_rows(bb, _):
            r0 = pl.multiple_of(bb * SUBLANES, SUBLANES)

            def step(t, h):
                h_re, h_im = h
                r = pl.multiple_of(t * nb + r0, SUBLANES)
                n_re = a_re * h_re - a_im * h_im + hbuf_ref[pl.ds(r, SUBLANES), re]
                n_im = a_re * h_im + a_im * h_re + hbuf_ref[pl.ds(r, SUBLANES), im]
                hbuf_ref[pl.ds(r, SUBLANES), re] = n_re
                hbuf_ref[pl.ds(r, SUBLANES), im] = n_im
                return n_re, n_im

            h = (hc_ref[pl.ds(r0, SUBLANES), re], hc_ref[pl.ds(r0, SUBLANES), im])
            h = lax.fori_loop(0, tt, step, h, unroll=min(tt, 8))
            hc_ref[pl.ds(r0, SUBLANES), re] = h[0]
            hc_ref[pl.ds(r0, SUBLANES), im] = h[1]
            return 0

        lax.fori_loop(0, nb // SUBLANES, batch_rows, 0)

    halves = []
    for hf in range(nsplit):
        chans = slice(hf * kw, (hf + 1) * kw)
        acc = None
        for part in (0, ns):
            states = slice(part + hf * nsh, part + (hf + 1) * nsh)
            term = jnp.dot(hbuf_ref[:, states].astype(BF16), cmat_ref[states, chans], preferred_element_type=F32)
            acc = term if acc is None else acc + term
        halves.append(acc)
    y = (jnp.concatenate(halves, axis=1) if nsplit > 1 else halves[0]) + d_ref[...] * u
    y = 0.5 * y * (1.0 + jnp.tanh(math.sqrt(2.0 / math.pi) * (y + 0.044715 * (y * y * y))))
    z = jnp.dot(y.astype(BF16), wglu_ref[...], preferred_element_type=F32) + bglu_ref[...]
    y_ref[...] = (y * _sigmoid(z)).reshape(tt, nb, width)

    @pl.when(ti == pl.num_programs(1) - 1)
    def _():
        hfin_ref[...] = hc_ref[...]


S5_SLAB = 256


def _s5(u_tm, bmat, arow, h0, cmat, d_row, wglu, bglu, group_channels):
    t, b, width = u_tm.shape
    ns2 = bmat.shape[1]
    nsplit = width // S5_SLAB if width % S5_SLAB == 0 and S5_SLAB % group_channels == 0 else 1
    nb = _tile(b, 64, SUBLANES)
    tt = _tile(t, max(SUBLANES, 512 // nb), 1)
    const = lambda i, j: (0, 0)
    return pl.pallas_call(
        functools.partial(_s5_kernel, tt=tt, nb=nb, ns=ns2 // 2, cb=_tile(ns2 // 2, 512), nsplit=nsplit),
        grid=(b // nb, t // tt),
        in_specs=[
            pl.BlockSpec((tt, nb, width), lambda i, j: (j, i, 0)),
            pl.BlockSpec((width, ns2), const),
            pl.BlockSpec((1, ns2), const),
            pl.BlockSpec((nb, ns2), lambda i, j: (i, 0)),
            pl.BlockSpec((ns2, width), const),
            pl.BlockSpec((1, width), const),
            pl.BlockSpec((width, width), const),
            pl.BlockSpec((1, width), const),
        ],
        out_specs=[
            pl.BlockSpec((tt, nb, width), lambda i, j: (j, i, 0)),
            pl.BlockSpec((nb, ns2), lambda i, j: (i, 0)),
        ],
        out_shape=[jax.ShapeDtypeStruct((t, b, width), F32), jax.ShapeDtypeStruct((b, ns2), F32)],
        scratch_shapes=[pltpu.VMEM((tt * nb, ns2), F32), pltpu.VMEM((nb, ns2), F32)],
        compiler_params=_params("parallel", "arbitrary"),
        name="s5",
    )(u_tm, bmat, arow, h0, cmat, d_row, wglu, bglu)


def _split(a):
    hi = a.astype(BF16)
    return hi, (a - hi.astype(F32)).astype(BF16)


def _mm3(a, b):
    d = lambda x, y: jnp.dot(x, y, preferred_element_type=F32)
    return d(a[0], b[0]) + d(a[0], b[1]) + d(a[1], b[0])


def _block_tri_inverse(a, n, c):
    row = lax.broadcasted_iota(jnp.int32, (n, n), 0)
    col = lax.broadcasted_iota(jnp.int32, (n, n), 1)
    eye = (row == col).astype(F32)

    def same_block(b):
        s = b.bit_length() - 1
        return lax.shift_right_logical(row, s) == lax.shift_right_logical(col, s)

    b0 = min(c, TRI_BASE)
    neg = -jnp.where(same_block(b0), a, 0.0)
    t = eye + neg
    p = _split(neg)
    k = 1
    while 2 * k < b0:
        p = _split(_mm3(p, p))
        t = t + _mm3(_split(t), p)
        k *= 2
    b = b0
    while b < c:
        off = jnp.where(jnp.logical_and(same_block(2 * b), jnp.logical_not(same_block(b))), a, 0.0)
        ts = _split(t)
        t = t - _mm3(_split(_mm3(ts, _split(off))), ts)
        b *= 2
    return t


def _dn_kernel(q_ref, k_ref, v_ref, ba_ref, gate_ref, cw_ref, prm_ref, nrm_ref, cbuf_ref, s0_ref,
               o_ref, sout_ref, cout_ref,
               xpad_ref, qs_ref, ks_ref, vs_ref, gb_ref, s_ref, *, bs, tt, chunk, nh, hd, taps):
    ti = pl.program_id(1)
    width = nh * hd
    halo = SUBLANES
    first = halo - (taps - 1)

    @pl.when(ti == 0)
    def _():
        s_ref[...] = s0_ref[...]
        xpad_ref[:, first:halo, :] = cbuf_ref[...]

    xpad_ref[:, halo:halo + tt, 0:width] = q_ref[...]
    xpad_ref[:, halo:halo + tt, width:2 * width] = k_ref[...]
    xpad_ref[:, halo:halo + tt, 2 * width:3 * width] = v_ref[...]
    cw = cw_ref[...]
    y = cw[0:1, :] * xpad_ref[:, first:first + tt, :]
    for i in range(1, taps):
        y = y + cw[i:i + 1, :] * xpad_ref[:, first + i:first + i + tt, :]
    cout_ref[...] = xpad_ref[:, halo + tt - (taps - 1):halo + tt, :]
    xpad_ref[:, 0:halo, :] = xpad_ref[:, tt:tt + halo, :]
    act = y * _sigmoid(y)
    for h in range(nh):
        sl = slice(h * hd, (h + 1) * hd)
        qh = act[:, :, h * hd:(h + 1) * hd]
        kh = act[:, :, width + h * hd:width + (h + 1) * hd]
        qs_ref[:, :, sl] = qh * lax.rsqrt(jnp.sum(qh * qh, axis=-1, keepdims=True) + NORM_EPS) * (hd ** -0.5)
        ks_ref[:, :, sl] = kh * lax.rsqrt(jnp.sum(kh * kh, axis=-1, keepdims=True) + NORM_EPS)
    vs_ref[...] = act[:, :, 2 * width:3 * width]
    ba = ba_ref[...]
    lane = lax.broadcasted_iota(jnp.int32, ba.shape, 2)
    gval = -jnp.exp(prm_ref[0:1, :]) * _softplus(ba + prm_ref[1:2, :])
    gb_ref[...] = jnp.where(lane < nh, _sigmoid(ba), jnp.where(lane < 2 * nh, gval, 0.0))

    pairs = [(s, h) for s in range(bs) for h in range(nh)]
    npairs = len(pairs)
    nrow = npairs * chunk
    lc = chunk.bit_length() - 1
    row = lax.broadcasted_iota(jnp.int32, (nrow, nrow), 0)
    col = lax.broadcasted_iota(jnp.int32, (nrow, nrow), 1)
    same = lax.shift_right_logical(row, lc) == lax.shift_right_logical(col, lc)
    lower = jnp.logical_and(same, row >= col)
    strict = jnp.logical_and(same, row > col)
    tri = lower.astype(BF16)
    last_sel = col == (row | (chunk - 1))
    prow = lax.shift_right_logical(lax.broadcasted_iota(jnp.int32, (nrow, LANES), 0), lc)
    plane = lax.broadcasted_iota(jnp.int32, (nrow, LANES), 1)
    head_of_row = prow & (nh - 1)
    is_beta = plane == head_of_row
    is_g = plane == head_of_row + nh
    ones8 = jnp.ones((BF16_ROWS, LANES), BF16)
    pair_of_row = lax.shift_right_logical(lax.broadcasted_iota(jnp.int32, (nrow, hd), 0), lc)

    def expand(x):
        return jnp.concatenate([jnp.where(pair_of_row == p, x, 0.0) for p in range(npairs)], axis=1)

    def one_chunk(c):
        rows = pl.ds(c * chunk, chunk)
        stack = lambda ref: jnp.concatenate([ref[s, rows, h * hd:(h + 1) * hd] for s, h in pairs], axis=0)
        qst, kst, vst = stack(qs_ref), stack(ks_ref), stack(vs_ref)
        gbst = jnp.concatenate([gb_ref[s, rows, :] for s, _ in pairs], axis=0)
        g_hi, g_lo = _split(gbst)
        gcum_all = (jnp.dot(tri, g_hi, preferred_element_type=F32)
                    + jnp.dot(tri, g_lo, preferred_element_type=F32))
        gsel = jnp.where(is_g, gcum_all, 0.0)
        gc = jnp.sum(gsel, axis=-1, keepdims=True)
        beta = jnp.sum(jnp.where(is_beta, gbst, 0.0), axis=-1, keepdims=True)
        s_hi, s_lo = _split(gsel)
        grow = (lax.dot_general(ones8, s_hi, _NT, preferred_element_type=F32)
                + lax.dot_general(ones8, s_lo, _NT, preferred_element_type=F32))[0:1, :]
        decay = jnp.where(lower, jnp.exp(jnp.minimum(gc - grow, 0.0)), 0.0)
        kb = kst * beta
        a = _mm(kb, kst, _NT) * jnp.where(strict, decay, 0.0)
        tinv = _block_tri_inverse(a, nrow, chunk)
        u = _mm(tinv, vst * beta)
        w = _mm(tinv, kb * jnp.exp(gc))
        attn = _mm(qst, kst, _NT) * decay
        s_stack = jnp.concatenate([s_ref[s, h] for s, h in pairs], axis=0)
        v_new = u - _mm(expand(w), s_stack)
        o = _mm(expand(qst * jnp.exp(gc)), s_stack) + _mm(attn, v_new)
        g_last = jnp.sum(jnp.where(last_sel, grow, 0.0), axis=-1, keepdims=True)
        upd = _mm(expand(kst * jnp.exp(g_last - gc)), v_new, _TN)
        gt = stack(gate_ref)
        on = _rms(o, nrm_ref[...]) * (gt * _sigmoid(gt))
        for idx, (s, h) in enumerate(pairs):
            gl = g_last[idx * chunk:idx * chunk + 1, :]
            s_ref[s, h] = s_ref[s, h] * jnp.exp(gl) + upd[idx * hd:(idx + 1) * hd, :]
            o_ref[s, rows, h * hd:(h + 1) * hd] = on[idx * chunk:(idx + 1) * chunk, :]

    for c in range(tt // chunk):
        one_chunk(c)

    @pl.when(ti == pl.num_programs(1) - 1)
    def _():
        sout_ref[...] = s_ref[...]


def _deltanet(rest3, col0, conv_w, prm, nrm, cbuf, s0, nh, hd):
    nseq, t, _ = rest3.shape
    width = nh * hd
    taps = conv_w.shape[0]
    chunk = min(DN_CHUNK, t)
    assert t % chunk == 0 and chunk & (chunk - 1) == 0 and t >= taps - 1
    tt = _tile(t, 256, chunk)
    bs = _tile(nseq, 8, 1) if tt == t and t <= SUBLANES else 1
    ba_col = (col0 + 4) * (width // LANES)
    blk = lambda cidx: pl.BlockSpec((bs, tt, width), lambda i, j: (i, j, cidx))
    const = lambda i, j: (0, 0)
    return pl.pallas_call(
        functools.partial(_dn_kernel, bs=bs, tt=tt, chunk=chunk, nh=nh, hd=hd, taps=taps),
        grid=(nseq // bs, t // tt),
        in_specs=[
            blk(col0 + 1), blk(col0 + 2), blk(col0 + 3),
            pl.BlockSpec((bs, tt, LANES), lambda i, j: (i, j, ba_col)),
            blk(col0),
            pl.BlockSpec((taps, 3 * width), const),
            pl.BlockSpec((SUBLANES, LANES), const),
            pl.BlockSpec((1, hd), const),
            pl.BlockSpec((bs, taps - 1, 3 * width), lambda i, j: (i, 0, 0)),
            pl.BlockSpec((bs, nh, hd, hd), lambda i, j: (i, 0, 0, 0)),
        ],
        out_specs=[
            pl.BlockSpec((bs, tt, width), lambda i, j: (i, j, 0)),
            pl.BlockSpec((bs, nh, hd, hd), lambda i, j: (i, 0, 0, 0)),
            pl.BlockSpec((bs, taps - 1, 3 * width), lambda i, j: (i, 0, 0)),
        ],
        out_shape=[
            jax.ShapeDtypeStruct((nseq, t, width), F32),
            jax.ShapeDtypeStruct((nseq, nh, hd, hd), F32),
            jax.ShapeDtypeStruct((nseq, taps - 1, 3 * width), F32),
        ],
        scratch_shapes=[
            pltpu.VMEM((bs, tt + 2 * SUBLANES, 3 * width), F32),
            pltpu.VMEM((bs, tt, width), F32),
            pltpu.VMEM((bs, tt, width), F32),
            pltpu.VMEM((bs, tt, width), F32),
            pltpu.VMEM((bs, tt, LANES), F32),
            pltpu.VMEM((bs, nh, hd, hd), F32),
        ],
        compiler_params=_params("parallel", "arbitrary"),
        name="deltanet",
    )(rest3, rest3, rest3, rest3, rest3, conv_w, prm, nrm, cbuf, s0)


def _outproj_kernel(oa_ref, ob_ref, oc_ref, ga_ref, gb_ref, w_ref, x_ref, o_ref, mix_ref):
    wa, wb = oa_ref.shape[-1], ob_ref.shape[-1]

    @pl.when(pl.program_id(1) == 0)
    def _():
        mix_ref[:, 0:wa] = _rms(oa_ref[...], ga_ref[...]).astype(BF16)
        mix_ref[:, wa:wa + wb] = _rms(ob_ref[...], gb_ref[...]).astype(BF16)
        mix_ref[:, wa + wb:] = oc_ref[...].astype(BF16)

    o_ref[...] = x_ref[...] + jnp.dot(mix_ref[...], w_ref[...], preferred_element_type=F32)


def _outproj(oa, ob, oc, ga, gb, w, x2d):
    m, d = x2d.shape
    tm = _tile(m, 512, SUBLANES)
    _, d_in, tn = w.shape
    row = lambda a: pl.BlockSpec((tm, a.shape[-1]), lambda i, j: (i, 0))
    const = lambda a: pl.BlockSpec((1, a.shape[-1]), lambda i, j: (0, 0))
    return pl.pallas_call(
        _outproj_kernel,
        grid=(m // tm, d // tn),
        in_specs=[row(oa), row(ob), row(oc), const(ga), const(gb),
                  pl.BlockSpec((None, d_in, tn), lambda i, j: (j, 0, 0)),
                  pl.BlockSpec((tm, tn), lambda i, j: (i, j))],
        out_specs=pl.BlockSpec((tm, tn), lambda i, j: (i, j)),
        out_shape=jax.ShapeDtypeStruct((m, d), F32),
        scratch_shapes=[pltpu.VMEM((tm, d_in), BF16)],
        compiler_params=_params("parallel", "arbitrary"),
        name="outproj",
    )(oa, ob, oc, ga, gb, w, x2d)


FFN_SUBTILE = 256


def _ffn_kernel(x_ref, g_ref, wg_ref, wv_ref, wd_ref, cw_ref, st_ref, o_ref, buf_ref, h_ref, act_ref, carry_ref,
                *, nf, tiles_per_seq, per_seq_rows):
    i = pl.program_id(0)
    f = pl.program_id(1)
    tm, tf = act_ref.shape

    def up():
        h = h_ref[...]
        if not per_seq_rows:
            @pl.when((i % tiles_per_seq) == 0)
            def _():
                carry_ref[f] = st_ref[...]

        tc = FFN_SUBTILE if tf % FFN_SUBTILE == 0 else tf
        row = lax.broadcasted_iota(jnp.int32, (tm, tc), 0)
        for s in range(tf // tc):
            cols = slice(s * tc, (s + 1) * tc)
            gate = jnp.dot(h, wg_ref[:, cols], preferred_element_type=F32)
            val = jnp.dot(h, wv_ref[:, cols], preferred_element_type=F32)
            prev1 = pltpu.roll(gate, 1, axis=0)
            prev2 = pltpu.roll(gate, 2, axis=0)
            if per_seq_rows:
                nseq = tm // per_seq_rows
                expand = lambda a: jnp.broadcast_to(a, (nseq, per_seq_rows, tc)).reshape(tm, tc)
                s0, s1 = expand(st_ref[:, 0:1, cols]), expand(st_ref[:, 1:2, cols])
                t = row & (per_seq_rows - 1)
                buf_ref[:, :, cols] = gate.reshape(nseq, per_seq_rows, tc)[:, per_seq_rows - 2:, :]
            else:
                s0, s1 = carry_ref[f, 0:1, cols], carry_ref[f, 1:2, cols]
                t = row
                tail = gate[tm - 2:tm, :]
                carry_ref[f, :, cols] = tail
                buf_ref[:, cols] = tail
            x1 = jnp.where(t >= 1, prev1, s1)
            x2 = jnp.where(t >= 2, prev2, jnp.where(t == 1, s1, s0))
            conv = cw_ref[2:3, cols] * gate + cw_ref[1:2, cols] * x1 + cw_ref[0:1, cols] * x2
            act_ref[:, cols] = (conv * _sigmoid(conv) * val).astype(BF16)

    def down():
        o_ref[...] += jnp.dot(act_ref[...], wd_ref[...], preferred_element_type=F32)

    @pl.when(f == 0)
    def _():
        h_ref[...] = _rms(x_ref[...], g_ref[...]).astype(BF16)
        o_ref[...] = x_ref[...]
        up()

    @pl.when(jnp.logical_and(f > 0, f < nf))
    def _():
        down()
        up()

    @pl.when(f == nf)
    def _():
        down()


def _ffn(x2d, gain, wg, wv, wd, conv_w, state, seq):
    m, d = x2d.shape
    d_ff = wg.shape[1]
    tf = _tile(d_ff, FFN_TILE)
    nf = d_ff // tf
    nseq = m // seq
    assert conv_w.shape[0] == 3
    up_tile = lambda f: jnp.minimum(f, nf - 1)
    down_tile = lambda f: jnp.maximum(f - 1, 0)
    if seq <= SUBLANES:
        assert seq == SUBLANES
        tm = _tile(m, FFN_ROWS, SUBLANES)
        per_seq_rows, tiles_per_seq = seq, 1
        st_spec = buf_spec = pl.BlockSpec((tm // seq, 2, tf), lambda i, f: (i, 0, up_tile(f)))
        n_buf = nseq
    else:
        tm = _tile(seq, FFN_ROWS, SUBLANES)
        per_seq_rows, tiles_per_seq = 0, seq // tm
        st_spec = pl.BlockSpec((None, 2, tf), lambda i, f: (i // tiles_per_seq, 0, up_tile(f)))
        buf_spec = pl.BlockSpec((None, 2, tf), lambda i, f: (i, 0, up_tile(f)))
        n_buf = m // tm
    y, buf = pl.pallas_call(
        functools.partial(_ffn_kernel, nf=nf, tiles_per_seq=tiles_per_seq, per_seq_rows=per_seq_rows),
        grid=(m // tm, nf + 1),
        in_specs=[
            pl.BlockSpec((tm, d), lambda i, f: (i, 0), pipeline_mode=pl.Buffered(1)),
            pl.BlockSpec((1, d), lambda i, f: (0, 0)),
            pl.BlockSpec((d, tf), lambda i, f: (0, up_tile(f))),
            pl.BlockSpec((d, tf), lambda i, f: (0, up_tile(f))),
            pl.BlockSpec((tf, d), lambda i, f: (down_tile(f), 0)),
            pl.BlockSpec((3, tf), lambda i, f: (0, up_tile(f))),
            st_spec,
        ],
        out_specs=[pl.BlockSpec((tm, d), lambda i, f: (i, 0)), buf_spec],
        out_shape=[jax.ShapeDtypeStruct((m, d), F32), jax.ShapeDtypeStruct((n_buf, 2, d_ff), F32)],
        scratch_shapes=[pltpu.VMEM((tm, d), BF16), pltpu.VMEM((tm, tf), BF16), pltpu.VMEM((nf, 2, tf), F32)],
        compiler_params=_params("arbitrary", "arbitrary"),
        name="convffn",
    )(x2d, gain, wg, wv, wd, conv_w, state)
    if not per_seq_rows:
        buf = buf.reshape(nseq, tiles_per_seq, 2, d_ff)[:, tiles_per_seq - 1]
    return y, buf


FFN_TILE = 512
FFN_ROWS = 1024
OUT_TILE = 1024


def _col_tiles(w, tn):
    d, n = w.shape
    return w.reshape(d, n // tn, tn).transpose(1, 0, 2)


def _layer_weights(l, w_in, sb_q_norm, sb_k_norm, ssm, dn_a_log, dn_dt_bias, ssm_w_glu, w_out, ffn_w_in, ffn_w_out,
                   dims):
    sbw, ssmw, dnw, nh = dims["sbw"], dims["ssmw"], dims["dnw"], dims["dn_heads"]
    heads, hd = dims["sb_heads"], dims["sb_hd"]
    d = w_in.shape[1]
    o_u = 3 * sbw
    o_qkv = o_u + ssmw
    o_ba = o_qkv + 3 * dnw
    o_gate = o_ba + 2 * nh
    n_rest = -(-(ssmw + 4 * dnw + LANES) // sbw) * sbw
    pad = n_rest - (ssmw + 4 * dnw + 2 * nh)
    w = w_in[l]
    w_perm = jnp.concatenate([
        w[:, :o_u], w[:, o_u:o_qkv], w[:, o_gate:o_gate + dnw], w[:, o_qkv:o_ba], w[:, o_ba:o_gate],
        jnp.zeros((d, pad), w.dtype)], axis=1).astype(BF16)
    head_of = jnp.arange(sbw) // hd
    gs = (head_of[:, None] == jnp.arange(LANES)[None, :]).astype(BF16)
    qg = jnp.tile(sb_q_norm[l], heads)[None, :]
    kg = jnp.tile(sb_k_norm[l], heads)[None, :]
    a_re, a_im, bb_re, bb_im = ssm["disc"]
    g, n, ch = dims["ssm_groups"], dims["ssm_state"], dims["ssm_ch"]
    ns = g * n
    grp_of_state = jnp.arange(ns) // n
    grp_of_chan = jnp.arange(ssmw) // ch
    blockmask = grp_of_chan[:, None] == grp_of_state[None, :]
    expand_b = lambda bb: jnp.where(blockmask, jnp.tile(bb.T, (g, 1)), 0.0)
    bmat = jnp.concatenate([expand_b(bb_re), expand_b(bb_im)], axis=1).astype(BF16)
    arow = jnp.concatenate([a_re, a_im], axis=0).reshape(1, 2 * ns)
    expand_c = lambda c: jnp.where(blockmask.T, jnp.tile(c.transpose(0, 2, 1).reshape(ns, ch), (1, g)), 0.0)
    cmat = jnp.concatenate([expand_c(ssm["c_re"][l]), -expand_c(ssm["c_im"][l])], axis=0).astype(BF16)
    prm = jnp.zeros((SUBLANES, LANES), F32)
    prm = prm.at[0, nh:2 * nh].set(dn_a_log[l]).at[1, nh:2 * nh].set(dn_dt_bias[l])
    d_ff = ffn_w_in.shape[2] // 2
    return dict(
        w_perm=_col_tiles(w_perm, sbw), gs=gs, e=gs.T, qg=qg, kg=kg, bmat=bmat, arow=arow, cmat=cmat, prm=prm,
        wglu=ssm_w_glu[l].astype(BF16), w_out=_col_tiles(w_out[l].astype(BF16), _tile(w_out.shape[2], OUT_TILE)),
        wg=ffn_w_in[l][:, :d_ff].astype(BF16), wv=ffn_w_in[l][:, d_ff:].astype(BF16),
        wd=ffn_w_out[l].astype(BF16), n_rest=n_rest)


def kernel(x_prompt, x_sample, cache_k, cache_v, page_table, state_ssm_re, state_ssm_im, state_delta, state_delta_conv, state_ffn_conv, norm_mix, w_in, sb_q_norm, sb_k_norm, sb_logit_bias, ssm_lambda_re, ssm_lambda_im, ssm_log_step, ssm_b_re, ssm_b_im, ssm_c_re, ssm_c_im, ssm_d, ssm_w_glu, ssm_b_glu, dn_conv_w, dn_a_log, dn_dt_bias, dn_norm, norm_out_sb, norm_out_ssm, w_out, norm_ffn, ffn_w_in, ffn_conv_w, ffn_w_out):
    depth = w_in.shape[0]
    bp, tp, d = x_prompt.shape
    bd, td, _ = x_sample.shape
    sb_heads, sb_hd = cache_k.shape[-2], cache_k.shape[-1]
    g, n = ssm_lambda_re.shape[1:]
    ch = ssm_b_re.shape[-1]
    dn_heads, dn_hd = dn_a_log.shape[1], dn_norm.shape[1]
    dims = dict(sbw=sb_heads * sb_hd, sb_heads=sb_heads, sb_hd=sb_hd, ssmw=g * ch, ssm_groups=g, ssm_state=n,
                ssm_ch=ch, dnw=dn_heads * dn_hd, dn_heads=dn_heads)
    sbw, ssmw, dnw, ns = dims["sbw"], dims["ssmw"], dims["dnw"], g * n
    assert ssmw % dnw == 0 and dnw % LANES == 0 and sbw % LANES == 0 and LANES % sb_hd == 0
    d_ff = ffn_conv_w.shape[2]
    col0 = ssmw // dnw
    q_scale = sb_hd ** -0.5 * LOG2E

    groups = {
        "p": dict(x=x_prompt.reshape(bp * tp, d), b=bp, t=tp, kbuf=None, vbuf=None, outs=[]),
        "s": dict(x=x_sample.reshape(bd * td, d), b=bd, t=td, kbuf=None, vbuf=None, outs=[]),
    }
    for l in range(depth):
        disc = _s5_discretize(ssm_lambda_re[l], ssm_lambda_im[l], ssm_log_step[l], ssm_b_re[l], ssm_b_im[l])
        lw = _layer_weights(l, w_in, sb_q_norm, sb_k_norm, dict(disc=disc, c_re=ssm_c_re, c_im=ssm_c_im),
                            dn_a_log, dn_dt_bias, ssm_w_glu, w_out, ffn_w_in, ffn_w_out, dims)
        for name, grp in groups.items():
            b, t, x = grp["b"], grp["t"], grp["x"]
            q, kbuf, vbuf, rest = _inproj(x, norm_mix[l][None, :], lw["w_perm"], lw["gs"], lw["e"], lw["qg"],
                                          lw["kg"], grp["kbuf"], grp["vbuf"], l, depth, sb_hd, q_scale, t,
                                          time_minor=(name == "p"))
            grp["kbuf"], grp["vbuf"] = kbuf, vbuf
            rest3 = rest.reshape(b, t, lw["n_rest"])
            if name == "p":
                o_a = _sb_prompt(q, kbuf, vbuf, sb_logit_bias[l], l, b, t, sb_hd)
                h0 = jnp.zeros((b, 2 * ns), F32)
                dn_s0 = jnp.zeros((b, dn_heads, dn_hd, dn_hd), F32)
                dn_c0 = jnp.zeros((b, dn_conv_w.shape[1] - 1, 3 * dnw), F32)
                ffn_c0 = jnp.zeros((b, ffn_conv_w.shape[1] - 1, d_ff), F32)
            else:
                o_a = _sb_sample(q, kbuf, vbuf, cache_k, cache_v, page_table, sb_logit_bias[l], l, b, t, sb_hd)
                h0 = jnp.concatenate([state_ssm_re[l].reshape(b, ns), state_ssm_im[l].reshape(b, ns)], axis=1)
                dn_s0, dn_c0, ffn_c0 = state_delta[l], state_delta_conv[l], state_ffn_conv[l]
            u_tm = rest3[:, :, :ssmw].transpose(1, 0, 2)
            y_tm, h_fin = _s5(u_tm, lw["bmat"], lw["arow"], h0, lw["cmat"], ssm_d[l][None, :], lw["wglu"],
                              ssm_b_glu[l][None, :], ch)
            o_b = y_tm.transpose(1, 0, 2).reshape(b * t, ssmw)
            o_c, dn_s, dn_c = _deltanet(rest3, col0, dn_conv_w[l], lw["prm"], dn_norm[l][None, :], dn_c0, dn_s0,
                                        dn_heads, dn_hd)
            x = _outproj(o_a, o_b, o_c.reshape(b * t, dnw), norm_out_sb[l][None, :], norm_out_ssm[l][None, :],
                         lw["w_out"], x)
            x, ffn_c = _ffn(x, norm_ffn[l][None, :], lw["wg"], lw["wv"], lw["wd"], ffn_conv_w[l], ffn_c0, t)
            grp["x"] = x
            grp["outs"].append((h_fin[:, :ns].reshape(b, g, n), h_fin[:, ns:].reshape(b, g, n), dn_s, dn_c, ffn_c))

    res = {}
    for name, grp in groups.items():
        b, t = grp["b"], grp["t"]
        sre, sim, dn_s, dn_c, ffn_c = [jnp.stack(z) for z in zip(*grp["outs"])]
        if name == "p":
            heads_last = lambda a: a.reshape(depth, b, sb_heads, sb_hd, t).transpose(0, 1, 4, 2, 3)
        else:
            heads_last = lambda a: a.reshape(depth, b, t, sb_heads, sb_hd)
        res[name] = dict(
            y=grp["x"].reshape(b, t, d), k=heads_last(grp["kbuf"]), v=heads_last(grp["vbuf"]),
            sre=sre, sim=sim, dn_s=dn_s, dn_c=dn_c, ffn_c=ffn_c)
    p, s = res["p"], res["s"]
    return (p["y"], s["y"], p["k"], p["v"], s["k"], s["v"], p["sre"], p["sim"], s["sre"], s["sim"],
            p["dn_s"], s["dn_s"], p["dn_c"], s["dn_c"], p["ffn_c"], s["ffn_c"])
```
